```python
import math, functools
import jax, jax.numpy as jnp
from jax import lax
import numpy as np

D_MODEL = 1024
BATCH = 4
SEQ = 8192
DEPTH = 4

CTX_LEN = 256
GRID_W = 64

HEAD_DIM = 64
CHUNK = 64
GDN_HEADS = 4
GDN_WIDTH = GDN_HEADS * HEAD_DIM
SHORT_CONV = 3
S5_WIDTH = 256
S5_GROUP = 16
S5_GROUPS = S5_WIDTH // S5_GROUP
S5_STATE = 64
HGRN_HEADS = 4
HGRN_WIDTH = HGRN_HEADS * HEAD_DIM
RET_HEADS = 4
RET_WIDTH = RET_HEADS * HEAD_DIM
ROPE_BASE = 10000.0
D_MIX = GDN_WIDTH + S5_WIDTH + HGRN_WIDTH + RET_WIDTH

PROJ_SIZES = (
    3 * GDN_WIDTH,
    GDN_WIDTH,
    2 * GDN_HEADS,
    2 * GDN_HEADS,
    S5_WIDTH,
    HGRN_WIDTH,
    2 * HGRN_WIDTH,
    HGRN_WIDTH,
    HGRN_WIDTH,
    3 * RET_WIDTH,
    RET_WIDTH,
)
D_PROJ = sum(PROJ_SIZES)

FF_DENSE = 2816
N_EXPERTS = 8
TOP_K = 2
FF_EXPERT = 3584
MOE_BLOCK = 128
N_DENSE_LAYERS = (DEPTH + 1) // 2
N_MOE_LAYERS = DEPTH // 2

DEEPNORM_ALPHA = (2.0 * DEPTH) ** 0.25
DEEPNORM_BETA = (8.0 * DEPTH) ** -0.25
LN_EPS = 1e-5
RMS_EPS = 1e-6

kernel_name = "hybrid_parallel_mixer_diffusion_trunk"


def _f32(t):
    return t.astype(jnp.float32)


def _layer_norm(x, g, b):
    xf = _f32(x)
    mu = jnp.mean(xf, -1, keepdims=True)
    var = jnp.mean(jnp.square(xf - mu), -1, keepdims=True)
    return ((xf - mu) * lax.rsqrt(var + LN_EPS) * _f32(g) + _f32(b)).astype(x.dtype)


def _rms_norm(x, w=None):
    y = x * lax.rsqrt(jnp.mean(jnp.square(x), -1, keepdims=True) + RMS_EPS)
    return y if w is None else y * w


def _l2norm(t):
    return t * lax.rsqrt(jnp.sum(t * t, -1, keepdims=True) + RMS_EPS)


def _heads(t, n_heads):
    b, l, _ = t.shape
    return t.reshape(b, l, n_heads, -1).transpose(0, 2, 1, 3)


def _merge(t):
    b, h, l, d = t.shape
    return t.transpose(0, 2, 1, 3).reshape(b, l, h * d)


def _chunk(t):
    return t.reshape(t.shape[:2] + (t.shape[2] // CHUNK, CHUNK) + t.shape[3:])


def _scan_chunks(step, s0, xs):
    s, o = lax.scan(step, s0, tuple(jnp.moveaxis(t, 2, 0) for t in xs))
    o = jnp.moveaxis(o, 0, 2)
    return o.reshape(o.shape[:2] + (-1,) + o.shape[4:]), s


def _short_conv(x, w):
    pad = SHORT_CONV // 2
    return lax.conv_general_dilated(x, w[:, None, :], window_strides=(1,), padding=((pad, pad),),
                                    dimension_numbers=('NWC', 'WIO', 'NWC'),
                                    feature_group_count=x.shape[-1])


def _axial_rotary(t, rows, cols):
    half = t.shape[-1] // 2
    quarter = half // 2
    inv_freq = ROPE_BASE ** (-jnp.arange(quarter, dtype=jnp.float32) / quarter)
    ang = jnp.concatenate([rows[:, None] * inv_freq, cols[:, None] * inv_freq], axis=-1)
    cos, sin = jnp.cos(ang), jnp.sin(ang)
    t1, t2 = t[..., :half], t[..., half:]
    return jnp.concatenate([t1 * cos - t2 * sin, t1 * sin + t2 * cos], axis=-1)


def _bidirectional(run_f, run_b, ctx_f, ctx_b, lat_f, lat_b, axis, need_ctx):
    flip = lambda xs: tuple(jnp.flip(t, axis) for t in xs)
    o_cf, s_cf = run_f(*ctx_f, None)
    o_cb, s_cb = run_b(*flip(ctx_b), None)
    o_lf, _ = run_f(*lat_f, s_cf)
    o_lb, _ = run_b(*flip(lat_b), s_cb)
    o_lat = o_lf + jnp.flip(o_lb, axis)
    o_ctx = (o_cf + jnp.flip(o_cb, axis)) if need_ctx else None
    return o_ctx, o_lat


def _gdn_direction(q, k, v, g, beta, s0):
    bsz, nh, _, dk = q.shape
    dv = v.shape[-1]
    q, k, v, g, beta = (_chunk(t) for t in (q, k, v, g, beta))
    gc = jnp.cumsum(g, axis=-1)
    tri = jnp.tril(jnp.ones((CHUNK, CHUNK), bool))
    decay = jnp.exp(jnp.where(tri, gc[..., :, None] - gc[..., None, :], -jnp.inf))
    kb = k * beta[..., None]
    a = jnp.einsum('bhnik,bhnjk->bhnij', kb, k) * decay
    eye = jnp.broadcast_to(jnp.eye(CHUNK, dtype=a.dtype), a.shape)
    t_inv = lax.linalg.triangular_solve(a, eye, left_side=True, lower=True, unit_diagonal=True)
    u = jnp.einsum('bhnij,bhnjv->bhniv', t_inv, v * beta[..., None])
    w = jnp.einsum('bhnij,bhnjk->bhnik', t_inv, kb * jnp.exp(gc)[..., None])
    qk = jnp.einsum('bhnik,bhnjk->bhnij', q, k) * decay
    if s0 is None:
        s0 = jnp.zeros((bsz, nh, dk, dv), q.dtype)

    def step(s, inp):
        q_c, k_c, u_c, w_c, qk_c, g_c = inp
        v_new = u_c - jnp.einsum('bhck,bhkv->bhcv', w_c, s)
        o = (jnp.einsum('bhck,bhkv->bhcv', q_c * jnp.exp(g_c)[..., None], s)
             + jnp.einsum('bhij,bhjv->bhiv', qk_c, v_new))
        g_last = g_c[..., -1:]
        s = (s * jnp.exp(g_last)[..., None]
             + jnp.einsum('bhck,bhcv->bhkv', k_c * jnp.exp(g_last - g_c)[..., None], v_new))
        return s, o

    return _scan_chunks(step, s0, (q, k, u, w, qk, gc))


def _gdn_mixer(ctx_parts, lat_parts, conv_w, a_log, dt_bias, norm_w, need_ctx):
    def per_dir(t):
        return t.reshape(t.shape[0], t.shape[1], 2, GDN_HEADS).transpose(2, 0, 3, 1)

    def prep(qkv, a, b):
        qkv = jax.nn.silu(_short_conv(qkv, conv_w))
        q, k, v = (_heads(t, GDN_HEADS) for t in jnp.split(qkv, 3, axis=-1))
        q = _l2norm(q) * HEAD_DIM ** -0.5
        k = _l2norm(k)
        g = -jnp.exp(a_log)[:, None, :, None] * jax.nn.softplus(per_dir(a) + dt_bias[:, None, :, None])
        beta = jax.nn.sigmoid(per_dir(b))
        return (q, k, v, g[0], beta[0]), (q, k, v, g[1], beta[1])

    c_qkv, c_z, c_a, c_b = ctx_parts
    l_qkv, l_z, l_a, l_b = lat_parts
    cf, cb = prep(c_qkv, c_a, c_b)
    lf, lb = prep(l_qkv, l_a, l_b)
    o_ctx, o_lat = _bidirectional(_gdn_direction, _gdn_direction, cf, cb, lf, lb, 2, need_ctx)

    def readout(o, z):
        return _merge(_rms_norm(o, norm_w) * jax.nn.silu(_heads(z, GDN_HEADS)))

    return (readout(o_ctx, c_z) if need_ctx else None), readout(o_lat, l_z)


def _s5_direction(lam_bar, b_bar, c_mat, u, h0):
    bu = jnp.einsum('blgh,gph->blgp', u.astype(jnp.complex64), b_bar)
    if h0 is not None:
        bu = bu.at[:, 0].add(lam_bar * h0)
    a = jnp.broadcast_to(lam_bar, bu.shape)

    def combine(e1, e2):
        return e1[0] * e2[0], e2[0] * e1[1] + e2[1]

    _, states = lax.associative_scan(combine, (a, bu), axis=1)
    y = jnp.real(jnp.einsum('blgp,ghp->blgh', states, c_mat))
    return y, states[:, -1]


def _s5_mixer(u_ctx, u_lat, a_re, a_im, log_step, b_re, b_im, c_re, c_im, d_skip, glu_w, glu_b,
              need_ctx):
    lam = lax.complex(a_re, a_im)
    lam_bar = jnp.exp(lam * jnp.exp(log_step)[..., None])
    b_bar = ((lam_bar - 1.0) / lam)[..., None] * lax.complex(b_re, b_im)
    c_mat = lax.complex(c_re, c_im)
    run_f = functools.partial(_s5_direction, lam_bar[0], b_bar[0], c_mat[0])
    run_b = functools.partial(_s5_direction, lam_bar[1], b_bar[1], c_mat[1])
    grp = lambda t: t.reshape(t.shape[:2] + (S5_GROUPS, S5_GROUP))
    uc, ul = grp(u_ctx), grp(u_lat)
    y_ctx, y_lat = _bidirectional(run_f, run_b, (uc,), (uc,), (ul,), (ul,), 1, need_ctx)

    def readout(y, u):
        y = jax.nn.gelu(y.reshape(u.shape) + d_skip * u)
        return y * jax.nn.sigmoid(y @ glu_w + glu_b)

    return (readout(y_ctx, u_ctx) if need_ctx else None), readout(y_lat, u_lat)


def _hgrn2_direction(q, k, v, logf, s0):
    bsz, nh, _, dk = q.shape
    dv = v.shape[-1]
    q, k, v, logf = (_chunk(t) for t in (q, k, v, logf))
    gc = jnp.cumsum(logf, axis=3)
    tri = jnp.tril(jnp.ones((CHUNK, CHUNK), bool))[:, :, None]
    if s0 is None:
        s0 = jnp.zeros((bsz, nh, dk, dv), q.dtype)

    def step(s, inp):
        q_c, k_c, v_c, g_c = inp
        rel = jnp.exp(jnp.where(tri, g_c[:, :, :, None, :] - g_c[:, :, None, :, :], -jnp.inf))
        att = jnp.einsum('bhik,bhjk,bhijk->bhij', q_c, k_c, rel)
        o = (jnp.einsum('bhck,bhkv->bhcv', q_c * jnp.exp(g_c), s)
             + jnp.einsum('bhij,bhjv->bhiv', att, v_c))
        g_last = g_c[:, :, -1:, :]
        s = (s * jnp.exp(g_last[:, :, 0, :])[..., None]
             + jnp.einsum('bhck,bhcv->bhkv', k_c * jnp.exp(g_last - g_c), v_c))
        return s, o

    return _scan_chunks(step, s0, (q, k, v, gc))


def _hgrn2_mixer(ctx_parts, lat_parts, lower_bound, norm_w, need_ctx):
    lb = lower_bound.reshape(HGRN_HEADS, 1, -1)

    def prep(q, f, i):
        q = jax.nn.silu(_heads(q, HGRN_HEADS))
        i = _heads(i, HGRN_HEADS)
        dirs = []
        for fz in jnp.split(f, 2, axis=-1):
            logf = jnp.logaddexp(jnp.log(lb), jnp.log1p(-lb) + jax.nn.log_sigmoid(_heads(fz, HGRN_HEADS)))
            dirs.append((q, -jnp.expm1(logf), i, logf))
        return dirs

    c_q, c_f, c_i, c_g = ctx_parts
    l_q, l_f, l_i, l_g = lat_parts
    cf, cb = prep(c_q, c_f, c_i)
    lf, lb_ = prep(l_q, l_f, l_i)
    o_ctx, o_lat = _bidirectional(_hgrn2_direction, _hgrn2_direction, cf, cb, lf, lb_, 2, need_ctx)

    def readout(o, g):
        return _merge(_rms_norm(o, norm_w) * jax.nn.silu(_heads(g, HGRN_HEADS)))

    return (readout(o_ctx, c_g) if need_ctx else None), readout(o_lat, l_g)


def _retention_direction(log_gamma, q, k, v, s0):
    bsz, nh, _, dk = q.shape
    dv = v.shape[-1]
    q, k, v = (_chunk(t) for t in (q, k, v))
    idx = jnp.arange(CHUNK, dtype=jnp.float32)
    tri = jnp.tril(jnp.ones((CHUNK, CHUNK), bool))
    decay = jnp.exp(jnp.where(tri, (idx[:, None] - idx[None, :]) * log_gamma[:, None, None], -jnp.inf))
    intra = jnp.einsum('bhnij,bhnjv->bhniv',
                       jnp.einsum('bhnik,bhnjk->bhnij', q, k) * decay[None, :, None], v)
    q_dec = jnp.exp((idx + 1.0) * log_gamma[:, None])
    k_dec = jnp.exp((CHUNK - 1.0 - idx) * log_gamma[:, None])
    c_dec = jnp.exp(CHUNK * log_gamma)
    if s0 is None:
        s0 = jnp.zeros((bsz, nh, dk, dv), q.dtype)

    def step(s, inp):
        q_c, k_c, v_c = inp
        o = jnp.einsum('bhck,bhkv->bhcv', q_c * q_dec[None, :, :, None], s)
        s = s * c_dec[None, :, None, None] + jnp.einsum('bhck,bhcv->bhkv', k_c * k_dec[None, :, :, None], v_c)
        return s, o

    inter, s = _scan_chunks(step, s0, (q, k, v))
    return intra.reshape(inter.shape) + inter, s


def _retention_mixer(ctx_parts, lat_parts, rows, cols, decay_param, need_ctx):
    log_gamma = -jnp.exp(decay_param)

    def prep(qkv, rotate):
        q, k, v = (_heads(t, RET_HEADS) for t in jnp.split(qkv, 3, axis=-1))
        if rotate:
            q, k = _axial_rotary(q, rows, cols), _axial_rotary(k, rows, cols)
        return (q, k * HEAD_DIM ** -0.5, v)

    c_qkv, c_g = ctx_parts
    l_qkv, l_g = lat_parts
    cin, lin = prep(c_qkv, False), prep(l_qkv, True)
    run_f = functools.partial(_retention_direction, log_gamma[0])
    run_b = functools.partial(_retention_direction, log_gamma[1])
    o_ctx, o_lat = _bidirectional(run_f, run_b, cin, cin, lin, lin, 2, need_ctx)

    def readout(o, g):
        return _merge(_rms_norm(o) * jax.nn.silu(_heads(g, RET_HEADS)))

    return (readout(o_ctx, c_g) if need_ctx else None), readout(o_lat, l_g)


def _swiglu(h, w1, w3, w2):
    return (jax.nn.silu(h @ w1) * (h @ w3)) @ w2


def _moe_swiglu(h, router, w1, w3, w2):
    shape = h.shape
    tok = h.reshape(-1, shape[-1])
    n_tok = tok.shape[0]
    n_asg = n_tok * TOP_K
    logits = _f32(tok @ router)
    top_val, top_idx = lax.top_k(logits, TOP_K)
    gates = jax.nn.softmax(top_val, axis=-1)
    flat_e = top_idx.reshape(-1)
    order = jnp.argsort(flat_e)
    sorted_e = flat_e[order]
    counts = jnp.bincount(flat_e, length=N_EXPERTS)
    padded = (counts + MOE_BLOCK - 1) // MOE_BLOCK * MOE_BLOCK
    start = jnp.cumsum(counts) - counts
    pad_end = jnp.cumsum(padded)
    pad_start = pad_end - padded
    dest = pad_start[sorted_e] + jnp.arange(n_asg) - start[sorted_e]
    n_blocks = -(-n_asg // MOE_BLOCK) + N_EXPERTS
    n_rows = n_blocks * MOE_BLOCK
    row_tok = jnp.full((n_rows,), n_tok, jnp.int32).at[dest].set((order // TOP_K).astype(jnp.int32))
    row_gate = jnp.zeros((n_rows,), jnp.float32).at[dest].set(gates.reshape(-1)[order])
    block_expert = jnp.minimum(
        jnp.sum(jnp.arange(n_blocks)[:, None] * MOE_BLOCK >= pad_end[None, :], axis=1), N_EXPERTS - 1)
    tok_pad = jnp.concatenate([tok, jnp.zeros((1, tok.shape[1]), tok.dtype)], axis=0)
    xb = tok_pad[row_tok].reshape(n_blocks, MOE_BLOCK, -1)

    def expert_block(args):
        xblk, e = args
        return (jax.nn.silu(xblk @ w1[e]) * (xblk @ w3[e])) @ w2[e]

    yb = lax.map(expert_block, (xb, block_expert)).reshape(n_rows, -1)
    out = jnp.zeros_like(tok_pad).at[row_tok].add(yb * row_gate[:, None].astype(yb.dtype))
    return out[:n_tok].reshape(shape)


def _split_proj(p):
    return jnp.split(p, np.cumsum(PROJ_SIZES)[:-1].tolist(), axis=-1)


def setup_inputs(seed: int = 0) -> dict:
    key = jax.random.key(seed)
    ks = iter(jax.random.split(key, 48))
    nrm = lambda shape, scale: jax.random.normal(next(ks), shape, jnp.float32) * scale
    unif = lambda shape, lo, hi: jax.random.uniform(next(ks), shape, jnp.float32, lo, hi)
    D, L = D_MODEL, DEPTH
    dt_init = jnp.exp(unif((L, 2, GDN_HEADS), math.log(1e-3), math.log(1e-1)))
    ret_init = jnp.log(-jnp.log1p(-(2.0 ** (-5.0 - jnp.arange(RET_HEADS, dtype=jnp.float32)))))
    return {
        "x": nrm((BATCH, SEQ, D), 1.0),
        "c": nrm((BATCH, D), 1.0),
        "ctx": nrm((BATCH, CTX_LEN, D), 1.0),
        "c_ctx": nrm((D,), 1.0),
        "ada_w": nrm((L, D, 6 * D), 0.5 * D ** -0.5),
        "ada_b": nrm((L, 6 * D), 0.01),
        "w_in": nrm((L, D, D_PROJ), D ** -0.5),
        "w_out": nrm((L, D_MIX, D), DEEPNORM_BETA * D_MIX ** -0.5),
        "ln_g": 1.0 + nrm((L, 2, D), 0.01),
        "ln_b": nrm((L, 2, D), 0.01),
        "gdn_conv_w": nrm((L, SHORT_CONV, 3 * GDN_WIDTH), SHORT_CONV ** -0.5),
        "gdn_a_log": jnp.log(unif((L, 2, GDN_HEADS), 1.0, 16.0)),
        "gdn_dt_bias": dt_init + jnp.log(-jnp.expm1(-dt_init)),
        "gdn_norm_w": 1.0 + nrm((L, HEAD_DIM), 0.01),
        "s5_a_re": -0.5 + nrm((L, 2, S5_GROUPS, S5_STATE), 0.01),
        "s5_a_im": jnp.pi * jnp.arange(S5_STATE, dtype=jnp.float32) + nrm((L, 2, S5_GROUPS, S5_STATE), 0.01),
        "s5_log_step": jnp.log(unif((L, 2, S5_GROUPS), 1e-3, 1e-1)),
        "s5_b_re": nrm((L, 2, S5_GROUPS, S5_STATE, S5_GROUP), (2.0 * S5_GROUP) ** -0.5),
        "s5_b_im": nrm((L, 2, S5_GROUPS, S5_STATE, S5_GROUP), (2.0 * S5_GROUP) ** -0.5),
        "s5_c_re": nrm((L, 2, S5_GROUPS, S5_GROUP, S5_STATE), (2.0 * S5_STATE) ** -0.5),
        "s5_c_im": nrm((L, 2, S5_GROUPS, S5_GROUP, S5_STATE), (2.0 * S5_STATE) ** -0.5),
        "s5_d": nrm((L, S5_WIDTH), 1.0),
        "s5_glu_w": nrm((L, S5_WIDTH, S5_WIDTH), S5_WIDTH ** -0.5),
        "s5_glu_b": nrm((L, S5_WIDTH), 0.01),
        "hgrn_lower_bounds": nrm((L, HGRN_WIDTH), 0.1),
        "hgrn_norm_w": 1.0 + nrm((L, HEAD_DIM), 0.01),
        "ret_decay": ret_init + nrm((L, 2, RET_HEADS), 0.05),
        "ffn_w1": nrm((N_DENSE_LAYERS, D, FF_DENSE), D ** -0.5),
        "ffn_w3": nrm((N_DENSE_LAYERS, D, FF_DENSE), D ** -0.5),
        "ffn_w2": nrm((N_DENSE_LAYERS, FF_DENSE, D), DEEPNORM_BETA * FF_DENSE ** -0.5),
        "moe_router": nrm((N_MOE_LAYERS, D, N_EXPERTS), D ** -0.5),
        "moe_w1": nrm((N_MOE_LAYERS, N_EXPERTS, D, FF_EXPERT), D ** -0.5),
        "moe_w3": nrm((N_MOE_LAYERS, N_EXPERTS, D, FF_EXPERT), D ** -0.5),
        "moe_w2": nrm((N_MOE_LAYERS, N_EXPERTS, FF_EXPERT, D), DEEPNORM_BETA * FF_EXPERT ** -0.5),
    }


def reference(x, c, ctx, c_ctx, ada_w, ada_b, w_in, w_out, ln_g, ln_b,
              gdn_conv_w, gdn_a_log, gdn_dt_bias, gdn_norm_w,
              s5_a_re, s5_a_im, s5_log_step, s5_b_re, s5_b_im, s5_c_re, s5_c_im,
              s5_d, s5_glu_w, s5_glu_b,
              hgrn_lower_bounds, hgrn_norm_w, ret_decay,
              ffn_w1, ffn_w3, ffn_w2, moe_router, moe_w1, moe_w3, moe_w2):
    dt = x.dtype
    n_lat = x.shape[1]
    ROWS = n_lat // GRID_W
    rows = jnp.repeat(jnp.arange(ROWS, dtype=jnp.float32), GRID_W)
    cols = jnp.tile(jnp.arange(GRID_W, dtype=jnp.float32), ROWS)
    lb_all = jnp.cumsum(jax.nn.softmax(_f32(hgrn_lower_bounds), axis=0), axis=0)
    lb_all = lb_all - lb_all[0]

    for layer in range(DEPTH):
        need_ctx = layer < DEPTH - 1
        mod_lat = jax.nn.silu(c) @ ada_w[layer] + ada_b[layer]
        mod_ctx = jax.nn.silu(c_ctx) @ ada_w[layer] + ada_b[layer]
        sh1, sc1, gt1, sh2, sc2, gt2 = jnp.split(mod_lat[:, None, :], 6, axis=-1)
        csh1, csc1, cgt1, csh2, csc2, cgt2 = jnp.split(mod_ctx, 6, axis=-1)

        p_lat = _split_proj(_f32((x * (1.0 + sc1) + sh1) @ w_in[layer]))
        p_ctx = _split_proj(_f32((ctx * (1.0 + csc1) + csh1) @ w_in[layer]))
        gdn_c, gdn_l = _gdn_mixer(p_ctx[0:4], p_lat[0:4], _f32(gdn_conv_w[layer]), _f32(gdn_a_log[layer]),
                                  _f32(gdn_dt_bias[layer]), _f32(gdn_norm_w[layer]), need_ctx)
        s5_c, s5_l = _s5_mixer(p_ctx[4], p_lat[4], _f32(s5_a_re[layer]), _f32(s5_a_im[layer]),
                               _f32(s5_log_step[layer]), _f32(s5_b_re[layer]), _f32(s5_b_im[layer]),
                               _f32(s5_c_re[layer]), _f32(s5_c_im[layer]), _f32(s5_d[layer]),
                               _f32(s5_glu_w[layer]), _f32(s5_glu_b[layer]), need_ctx)
        hg_c, hg_l = _hgrn2_mixer(p_ctx[5:9], p_lat[5:9], lb_all[layer], _f32(hgrn_norm_w[layer]), need_ctx)
        rt_c, rt_l = _retention_mixer(p_ctx[9:11], p_lat[9:11], rows, cols, _f32(ret_decay[layer]), need_ctx)

        mix_lat = jnp.concatenate([gdn_l, s5_l, hg_l, rt_l], axis=-1).astype(dt)
        x = _layer_norm(DEEPNORM_ALPHA * x + gt1 * (mix_lat @ w_out[layer]), ln_g[layer, 0], ln_b[layer, 0])
        if need_ctx:
            mix_ctx = jnp.concatenate([gdn_c, s5_c, hg_c, rt_c], axis=-1).astype(dt)
            ctx = _layer_norm(DEEPNORM_ALPHA * ctx + cgt1 * (mix_ctx @ w_out[layer]),
                              ln_g[layer, 0], ln_b[layer, 0])

        j = layer // 2
        if layer % 2 == 0:
            ffn = functools.partial(_swiglu, w1=ffn_w1[j], w3=ffn_w3[j], w2=ffn_w2[j])
        else:
            ffn = functools.partial(_moe_swiglu, router=moe_router[j], w1=moe_w1[j], w3=moe_w3[j], w2=moe_w2[j])
        x = _layer_norm(DEEPNORM_ALPHA * x + gt2 * ffn(x * (1.0 + sc2) + sh2), ln_g[layer, 1], ln_b[layer, 1])
        if need_ctx:
            ctx = _layer_norm(DEEPNORM_ALPHA * ctx + cgt2 * ffn(ctx * (1.0 + csc2) + csh2),
                              ln_g[layer, 1], ln_b[layer, 1])
    return x
```

```python
import functools
import math

import numpy as np
import jax
import jax.numpy as jnp
from jax import lax
from jax.experimental import pallas as pl
from jax.experimental.pallas import tpu as pltpu

F32 = jnp.float32
BF16 = jnp.bfloat16
HI = lax.Precision.HIGHEST

HEAD = 64
NH = 4
WM = NH * HEAD
TT = 256
CH = 64
S5C = 16
S5G = 16
S5P = 64
S5GB = 4
N_EXP = 8
MOE_RB = 512
LN_EPS = 1e-5
RMS_EPS = 1e-6
ROPE_BASE = 10000.0
VMEM_LIMIT = 56 * 1024 * 1024


def _cp(*sem):
    return pltpu.CompilerParams(dimension_semantics=sem, vmem_limit_bytes=VMEM_LIMIT)


def _sigmoid(x):
    return 1.0 / (1.0 + jnp.exp(-x))


def _silu(x):
    return x * _sigmoid(x)


def _softplus(x):
    return jnp.maximum(x, 0.0) + jnp.log1p(jnp.exp(-jnp.abs(x)))


def _dot(a, b, precision=None):
    return jnp.dot(a, b, preferred_element_type=F32, precision=precision)


def _dot_nt(a, b, precision=None):
    return lax.dot_general(a, b, (((1,), (1,)), ((), ())), preferred_element_type=F32, precision=precision)


def _dot_tn(a, b, precision=None):
    return lax.dot_general(a, b, (((0,), (0,)), ((), ())), preferred_element_type=F32, precision=precision)


def _layer_norm(y, g, b):
    mu = jnp.mean(y, axis=-1, keepdims=True)
    yc = y - mu
    var = jnp.mean(yc * yc, axis=-1, keepdims=True)
    return yc * lax.rsqrt(var + LN_EPS) * g + b


def _rev_index(n, n_ctx, n_all):
    return jnp.where(n < n_ctx, n_ctx - 1 - n, n_all + n_ctx - 1 - n)


def _ada_kernel(c_ref, w_ref, b_ref, o_ref):
    o_ref[0] = _dot(_silu(c_ref[...]), w_ref[0], HI) + b_ref[0]


def _ada_mod(cs, ada_w, ada_b):
    depth, d, d6 = ada_w.shape
    tn = 1024
    return pl.pallas_call(
        _ada_kernel,
        grid=(depth, d6 // tn),
        in_specs=[pl.BlockSpec((8, d), lambda l, j: (0, 0)),
                  pl.BlockSpec((1, d, tn), lambda l, j: (l, 0, j)),
                  pl.BlockSpec((1, 1, tn), lambda l, j: (l, 0, j))],
        out_specs=pl.BlockSpec((1, 8, tn), lambda l, j: (l, 0, j)),
        out_shape=jax.ShapeDtypeStruct((depth, 8, d6), F32),
        compiler_params=_cp("parallel", "parallel"),
        name="ada_mod",
    )(cs, ada_w, ada_b.reshape(depth, 1, d6))


W_PG, W_AB, W_S5, W_PH, W_PR = 1024, 128, 256, 1280, 1536
P_OFF = np.cumsum([0, W_PG, W_AB, W_S5, W_PH, W_PR])


def _inproj_kernel(x_ref, mod_ref, w_ref, pg_ref, pab_ref, ps_ref, ph_ref, pr_ref):
    m = mod_ref[0, 0]
    h = (x_ref[0] * (1.0 + m[1:2]) + m[0:1]).astype(BF16)
    for k, o_ref in enumerate((pg_ref, pab_ref, ps_ref, ph_ref, pr_ref)):
        o_ref[0] = _dot(h, w_ref[:, P_OFF[k]:P_OFF[k + 1]])


def _inproj(xs, mod, w):
    b, t, d = xs.shape
    widths = (W_PG, W_AB, W_S5, W_PH, W_PR)
    return pl.pallas_call(
        _inproj_kernel,
        grid=(b, t // TT),
        in_specs=[pl.BlockSpec((1, TT, d), lambda bi, i: (bi, i, 0)),
                  pl.BlockSpec((1, 1, 6, d), lambda bi, i: (bi, jnp.minimum(i, 1), 0, 0)),
                  pl.BlockSpec(w.shape, lambda bi, i: (0, 0))],
        out_specs=[pl.BlockSpec((1, TT, wd), lambda bi, i: (bi, i, 0)) for wd in widths],
        out_shape=[jax.ShapeDtypeStruct((b, t, wd), F32) for wd in widths],
        compiler_params=_cp("parallel", "parallel"),
        name="inproj",
    )(xs, mod, w)


def _gdn_prep_kernel(p_ref, pv_ref, nx_ref, cw_ref, ind_ref, o_ref):
    i = pl.program_id(1)
    nt = pl.num_programs(1)
    x = p_ref[0]
    prev = jnp.where(i >= 2, pv_ref[0][7:8], 0.0)
    nxt = jnp.where((i >= 1) & (i < nt - 1), nx_ref[0][0:1], 0.0)
    row = lax.broadcasted_iota(jnp.int32, (TT, 1), 0)
    xm = jnp.where(row == 0, prev, pltpu.roll(x, 1, 0))
    xp = jnp.where(row == TT - 1, nxt, pltpu.roll(x, TT - 1, 0))
    w = cw_ref[...]
    y = _silu(w[0:1] * xm + w[1:2] * x + w[2:3] * xp)
    q, k, v = y[:, :WM], y[:, WM:2 * WM], y[:, 2 * WM:]
    ind = ind_ref[...]
    q = q * lax.rsqrt(_dot(q * q, ind, HI) + RMS_EPS) * HEAD ** -0.5
    k = k * lax.rsqrt(_dot(k * k, ind, HI) + RMS_EPS)
    o_ref[0] = jnp.concatenate([q, k, v], axis=1)


def _gdn_prep(pg, conv_w, ind_sum):
    b, t, _ = pg.shape
    w3 = 3 * WM
    n8 = t // 8
    return pl.pallas_call(
        _gdn_prep_kernel,
        grid=(b, t // TT),
        in_specs=[pl.BlockSpec((1, TT, w3), lambda bi, i: (bi, i, 0)),
                  pl.BlockSpec((1, 8, w3), lambda bi, i: (bi, jnp.maximum(i * (TT // 8) - 1, 0), 0)),
                  pl.BlockSpec((1, 8, w3), lambda bi, i: (bi, jnp.minimum((i + 1) * (TT // 8), n8 - 1), 0)),
                  pl.BlockSpec((3, w3), lambda bi, i: (0, 0)),
                  pl.BlockSpec((WM, WM), lambda bi, i: (0, 0))],
        out_specs=pl.BlockSpec((1, TT, w3), lambda bi, i: (bi, i, 0)),
        out_shape=jax.ShapeDtypeStruct((b, t, w3), F32),
        compiler_params=_cp("parallel", "parallel"),
        name="gdn_prep",
    )(pg, pg, pg, conv_w, ind_sum)


def _gdn_direction(d, qkv, ab, abt, tri, na_ref, dtb_ref, s_ref):
    row = lax.broadcasted_iota(jnp.int32, (CH, CH), 0)
    col = lax.broadcasted_iota(jnp.int32, (CH, CH), 1)
    incl = (row >= col) if d == 0 else (row <= col)
    strict = (row > col) if d == 0 else (row < col)
    last = CH - 1 if d == 0 else 0
    a_col, b_col = ab[:, 4 * d:4 * d + 4], ab[:, 8 + 4 * d:12 + 4 * d]
    a_row, b_row = abt[4 * d:4 * d + 4, :], abt[8 + 4 * d:12 + 4 * d, :]
    g_cols, g_rows = [], []
    for h in range(NH):
        g_cols.append(na_ref[d, h] * _softplus(a_col[:, h:h + 1] + dtb_ref[d, h]))
        g_rows.append(na_ref[d, h] * _softplus(a_row[h:h + 1, :] + dtb_ref[d, h]))
    g_col = jnp.concatenate(g_cols, axis=1)
    g_row = jnp.concatenate(g_rows, axis=0)
    tri_c = tri if d == 0 else tri.T
    gc_col = _dot(tri_c, g_col, HI)
    gc_row = _dot_nt(g_row, tri_c, HI)
    outs = []
    for h in range(NH):
        q = qkv[:, h * HEAD:(h + 1) * HEAD]
        k = qkv[:, WM + h * HEAD:WM + (h + 1) * HEAD]
        v = qkv[:, 2 * WM + h * HEAD:2 * WM + (h + 1) * HEAD]
        beta = _sigmoid(b_col[:, h:h + 1])
        gcc = gc_col[:, h:h + 1]
        gcr = gc_row[h:h + 1, :]
        decay = jnp.exp(jnp.where(incl, gcc - gcr, -jnp.inf))
        kb = k * beta
        a = _dot_nt(kb, k) * jnp.where(strict, decay, 0.0)
        x = jnp.concatenate([v * beta, kb * jnp.exp(gcc)], axis=1)
        x = x - _dot(a, x, HI)
        p = a
        for _ in range(5):
            p = _dot(p, p, HI)
            x = x + _dot(p, x, HI)
        u, w = x[:, :HEAD], x[:, HEAD:]
        s = s_ref[d, h]
        v_new = u - _dot(w, s)
        o = _dot(q * jnp.exp(gcc), s) + _dot(_dot_nt(q, k) * decay, v_new)
        g_last = gcr[:, last:last + 1]
        s_ref[d, h] = s * jnp.exp(g_last) + _dot_tn(k * jnp.exp(g_last - gcc), v_new)
        outs.append(o)
    return jnp.concatenate(outs, axis=1)


def _gdn_scan_kernel(na_ref, dtb_ref, qf_ref, qb_ref, af_ref, ab_ref, atf_ref, atb_ref, tri_ref,
                     of_ref, ob_ref, s_ref):
    @pl.when(pl.program_id(1) == 0)
    def _():
        s_ref[...] = jnp.zeros_like(s_ref)

    tri = tri_ref[...]
    of_ref[0] = _gdn_direction(0, qf_ref[0], af_ref[0], atf_ref[0, 0], tri, na_ref, dtb_ref, s_ref)
    ob_ref[0] = _gdn_direction(1, qb_ref[0], ab_ref[0], atb_ref[0, 0], tri, na_ref, dtb_ref, s_ref)


def _gdn_scan(gq, pab, pabt, neg_a, dt_bias, tri, n_ctx):
    b, t, _ = gq.shape
    nc = t // CH
    fwd3 = lambda bi, n: (bi, n, 0)
    bwd3 = lambda bi, n: (bi, _rev_index(n, n_ctx, nc), 0)
    fwd4 = lambda bi, n: (bi, n, 0, 0)
    bwd4 = lambda bi, n: (bi, _rev_index(n, n_ctx, nc), 0, 0)
    smem = pl.BlockSpec(memory_space=pltpu.SMEM)
    return pl.pallas_call(
        _gdn_scan_kernel,
        grid=(b, nc),
        in_specs=[smem, smem,
                  pl.BlockSpec((1, CH, 3 * WM), fwd3), pl.BlockSpec((1, CH, 3 * WM), bwd3),
                  pl.BlockSpec((1, CH, W_AB), fwd3), pl.BlockSpec((1, CH, W_AB), bwd3),
                  pl.BlockSpec((1, 1, 16, CH), fwd4), pl.BlockSpec((1, 1, 16, CH), bwd4),
                  pl.BlockSpec((CH, CH), lambda bi, n: (0, 0))],
        out_specs=[pl.BlockSpec((1, CH, WM), fwd3), pl.BlockSpec((1, CH, WM), bwd3)],
        out_shape=[jax.ShapeDtypeStruct((b, t, WM), F32)] * 2,
        scratch_shapes=[pltpu.VMEM((2, NH, HEAD, HEAD), F32)],
        compiler_params=_cp("parallel", "arbitrary"),
        name="gdn_scan",
    )(neg_a, dt_bias, gq, gq, pab, pab, pabt, pabt, tri)


HG_LEVELS = (32, 16, 8, 4, 2, 1)


def _hgrn_constants():
    idx = np.arange(CH)
    i, t = idx[:, None], idx[None, :]
    blocks = [(t <= i), (t > i)]
    masks = []
    for s in HG_LEVELS:
        m = (idx // (2 * s)) * 2 * s + s
        sec = (idx % (2 * s)) >= s
        mi = m[:, None]
        blocks.append(sec[:, None] & (t >= mi) & (t <= i))
        blocks.append((~sec)[:, None] & (t > i) & (t <= mi - 1))
        same = (idx[:, None] // (2 * s)) == (idx[None, :] // (2 * s))
        masks.append(same & sec[:, None] & (~sec)[None, :])
    mall_f = np.concatenate(blocks, axis=0).astype(np.float32)
    masks_f = np.stack(masks).astype(np.float32)
    nb = len(blocks)
    mall_b = mall_f.reshape(nb, CH, CH)[:, ::-1, ::-1].reshape(nb * CH, CH)
    masks_b = masks_f[:, ::-1, ::-1]
    return np.stack([mall_f, mall_b]), np.stack([masks_f, masks_b])


def _hgrn_direction(d, ph, lbp, mall, masks, ind, st_ref):
    last = CH - 1 if d == 0 else 0
    q = _silu(ph[:, :WM])
    fz = ph[:, WM * (1 + d):WM * (2 + d)]
    v = ph[:, 3 * WM:4 * WM]
    log_lb, log_1m_lb, one_m_lb = lbp[0:1], lbp[1:2], lbp[2:3]
    lsig = jnp.minimum(fz, 0.0) - jnp.log1p(jnp.exp(-jnp.abs(fz)))
    bb = log_1m_lb + lsig
    logf = jnp.maximum(log_lb, bb) + jnp.log1p(jnp.exp(-jnp.abs(log_lb - bb)))
    k = one_m_lb / (1.0 + jnp.exp(fz))
    e = jnp.exp(_dot(mall, logf, HI))
    qg = q * e[0:CH]
    kd = k * e[CH:2 * CH]
    e_last = e[last:last + 1]
    diag = _dot(q * k, ind, HI)
    outs = []
    for h in range(NH):
        sl = slice(h * HEAD, (h + 1) * HEAD)
        att = None
        for lv in range(len(HG_LEVELS)):
            qt = q[:, sl] * e[(2 + 2 * lv) * CH:(3 + 2 * lv) * CH, sl]
            kt = k[:, sl] * e[(3 + 2 * lv) * CH:(4 + 2 * lv) * CH, sl]
            term = _dot_nt(qt, kt) * masks[lv]
            att = term if att is None else att + term
        st = st_ref[d, h]
        o = _dot(att, v[:, sl]) + diag[:, sl] * v[:, sl] + _dot_nt(qg[:, sl], st)
        st_ref[d, h] = st * e_last[:, sl] + _dot_tn(v[:, sl], kd[:, sl])
        outs.append(o)
    return jnp.concatenate(outs, axis=1)


def _hgrn_scan_kernel(pf_ref, pb_ref, lbp_ref, mall_ref, mask_ref, ind_ref, of_ref, ob_ref, st_ref):
    @pl.when(pl.program_id(1) == 0)
    def _():
        st_ref[...] = jnp.zeros_like(st_ref)

    lbp = lbp_ref[...]
    ind = ind_ref[...]
    of_ref[0] = _hgrn_direction(0, pf_ref[0], lbp, mall_ref[0], mask_ref[0], ind, st_ref)
    ob_ref[0] = _hgrn_direction(1, pb_ref[0], lbp, mall_ref[1], mask_ref[1], ind, st_ref)


def _hgrn_scan(ph, lbp, mall, masks, ind_sum, n_ctx):
    b, t, _ = ph.shape
    nc = t // CH
    fwd3 = lambda bi, n: (bi, n, 0)
    bwd3 = lambda bi, n: (bi, _rev_index(n, n_ctx, nc), 0)
    return pl.pallas_call(
        _hgrn_scan_kernel,
        grid=(b, nc),
        in_specs=[pl.BlockSpec((1, CH, W_PH), fwd3), pl.BlockSpec((1, CH, W_PH), bwd3),
                  pl.BlockSpec(lbp.shape, lambda bi, n: (0, 0)),
                  pl.BlockSpec(mall.shape, lambda bi, n: (0, 0, 0)),
                  pl.BlockSpec(masks.shape, lambda bi, n: (0, 0, 0, 0)),
                  pl.BlockSpec((WM, WM), lambda bi, n: (0, 0))],
        out_specs=[pl.BlockSpec((1, CH, WM), fwd3), pl.BlockSpec((1, CH, WM), bwd3)],
        out_shape=[jax.ShapeDtypeStruct((b, t, WM), F32)] * 2,
        scratch_shapes=[pltpu.VMEM((2, NH, HEAD, HEAD), F32)],
        compiler_params=_cp("parallel", "arbitrary"),
        name="hgrn_scan",
    )(ph, ph, lbp, mall, masks, ind_sum)


def _ret_direction(d, pr, cos, sin, dec_ref, qd, kd, cdec_ref, s_ref):
    q = (pr[:, :WM] * cos + pr[:, 4 * WM:5 * WM] * sin)
    k = (pr[:, WM:2 * WM] * cos + pr[:, 5 * WM:6 * WM] * sin) * HEAD ** -0.5
    v = pr[:, 2 * WM:3 * WM]
    q_in = q * qd
    k_in = k * kd
    outs = []
    for h in range(NH):
        sl = slice(h * HEAD, (h + 1) * HEAD)
        s = s_ref[d, h]
        att = _dot_nt(q[:, sl], k[:, sl]) * dec_ref[d, h]
        outs.append(_dot(att, v[:, sl]) + _dot(q_in[:, sl], s))
        s_ref[d, h] = s * cdec_ref[d, h] + _dot_tn(k_in[:, sl], v[:, sl])
    return jnp.concatenate(outs, axis=1)


def _ret_scan_kernel(cdec_ref, pf_ref, pb_ref, cf_ref, sf_ref, cb_ref, sb_ref, dec_ref, qd_ref, kd_ref,
                     of_ref, ob_ref, s_ref):
    @pl.when(pl.program_id(1) == 0)
    def _():
        s_ref[...] = jnp.zeros_like(s_ref)

    of_ref[0] = _ret_direction(0, pf_ref[0], cf_ref[...], sf_ref[...], dec_ref, qd_ref[0], kd_ref[0], cdec_ref, s_ref)
    ob_ref[0] = _ret_direction(1, pb_ref[0], cb_ref[...], sb_ref[...], dec_ref, qd_ref[1], kd_ref[1], cdec_ref, s_ref)


def _ret_scan(pr, cos_t, sin_t, dec, qdec, kdec, cdec):
    b, t, _ = pr.shape
    nt = t // TT
    fwd3 = lambda bi, n: (bi, n, 0)
    bwd3 = lambda bi, n: (bi, _rev_index(n, 1, nt), 0)
    fwd2 = lambda bi, n: (n, 0)
    bwd2 = lambda bi, n: (_rev_index(n, 1, nt), 0)
    return pl.pallas_call(
        _ret_scan_kernel,
        grid=(b, nt),
        in_specs=[pl.BlockSpec(memory_space=pltpu.SMEM),
                  pl.BlockSpec((1, TT, W_PR), fwd3), pl.BlockSpec((1, TT, W_PR), bwd3),
                  pl.BlockSpec((TT, WM), fwd2), pl.BlockSpec((TT, WM), fwd2),
                  pl.BlockSpec((TT, WM), bwd2), pl.BlockSpec((TT, WM), bwd2),
                  pl.BlockSpec(dec.shape, lambda bi, n: (0, 0, 0, 0)),
                  pl.BlockSpec(qdec.shape, lambda bi, n: (0, 0, 0)),
                  pl.BlockSpec(kdec.shape, lambda bi, n: (0, 0, 0))],
        out_specs=[pl.BlockSpec((1, TT, WM), fwd3), pl.BlockSpec((1, TT, WM), bwd3)],
        out_shape=[jax.ShapeDtypeStruct((b, t, WM), F32)] * 2,
        scratch_shapes=[pltpu.VMEM((2, NH, HEAD, HEAD), F32)],
        compiler_params=_cp("parallel", "arbitrary"),
        name="ret_scan",
    )(cdec, pr, pr, cos_t, sin_t, cos_t, sin_t, dec, qdec, kdec)


def _s5_kernel(u_ref, wz_ref, wy_ref, a1_ref, a2_ref, y_ref, z_ref, hp_ref, *, n_ctx):
    nch = u_ref.shape[2]
    for g in range(S5GB):
        z_ref[g] = _dot(u_ref[0, g], wz_ref[g], HI)
    a1 = a1_ref[...]
    a2 = a2_ref[...]

    def step(s, hs):
        tiles = (s, _rev_index(s, n_ctx // 8, nch // 8))
        new = []
        for g in range(S5GB):
            for d in range(2):
                h = hs[2 * g + d]
                r0 = pl.multiple_of(tiles[d] * 8, 8)
                lanes = slice(128 * d, 128 * (d + 1))
                z = z_ref[g, pl.ds(r0, 8), lanes]
                entering = [None] * 8
                for j in (range(8) if d == 0 else range(7, -1, -1)):
                    entering[j] = h
                    h = a1[g, d:d + 1] * h + a2[g, d:d + 1] * pltpu.roll(h, S5P, 1) + z[j:j + 1]
                hp_ref[g, pl.ds(r0, 8), lanes] = jnp.concatenate(entering, axis=0)
                new.append(h)
        return tuple(new)

    lax.fori_loop(0, nch // 8, step, tuple(jnp.zeros((1, 128), F32) for _ in range(2 * S5GB)))
    for g in range(S5GB):
        lhs = jnp.concatenate([u_ref[0, g], hp_ref[g]], axis=1)
        y_ref[0, g] = _dot(lhs, wy_ref[g], HI)


def _s5_scan(u4, wz, wy, a1, a2, n_ctx):
    b, g, nch, w = u4.shape
    return pl.pallas_call(
        functools.partial(_s5_kernel, n_ctx=n_ctx),
        grid=(b, g // S5GB),
        in_specs=[pl.BlockSpec((1, S5GB, nch, w), lambda bi, gi: (bi, gi, 0, 0)),
                  pl.BlockSpec((S5GB, w, w), lambda bi, gi: (gi, 0, 0)),
                  pl.BlockSpec((S5GB, 2 * w, w), lambda bi, gi: (gi, 0, 0)),
                  pl.BlockSpec((S5GB, 2, 128), lambda bi, gi: (gi, 0, 0)),
                  pl.BlockSpec((S5GB, 2, 128), lambda bi, gi: (gi, 0, 0))],
        out_specs=pl.BlockSpec((1, S5GB, nch, w), lambda bi, gi: (bi, gi, 0, 0)),
        out_shape=jax.ShapeDtypeStruct(u4.shape, F32),
        scratch_shapes=[pltpu.VMEM((S5GB, nch, w), F32), pltpu.VMEM((S5GB, nch, w), F32)],
        compiler_params=_cp("parallel", "parallel"),
        name="s5_scan",
    )(u4, wz, wy, a1, a2)


def _s5_weights(a_re, a_im, log_step, b_re, b_im, c_re, c_im):
    step = jnp.exp(log_step)[..., None]
    e_re, e_im = a_re * step, a_im * step
    def lam_pow(n):
        n = n[..., None, None, None] if n.ndim else n
        mag = jnp.exp(e_re * n)
        return mag * jnp.cos(e_im * n), mag * jnp.sin(e_im * n)
    l1r, l1i = lam_pow(jnp.asarray(1.0, F32))
    den = a_re * a_re + a_im * a_im
    fr = ((l1r - 1.0) * a_re + l1i * a_im) / den
    fi = (l1i * a_re - (l1r - 1.0) * a_im) / den
    bbr = fr[..., None] * b_re - fi[..., None] * b_im
    bbi = fr[..., None] * b_im + fi[..., None] * b_re
    j = jnp.arange(S5C, dtype=F32)
    es = functools.partial(jnp.einsum, precision=HI)

    def build(d):
        cr, ci = c_re[d], c_im[d]
        br, bi = bbr[d], bbi[d]
        sel = lambda x: x[:, d] if x.ndim == 4 else x
        pr, pi = lam_pow(jnp.arange(S5C + 1, dtype=F32))
        pr, pi = pr[:, d], pi[:, d]
        cl_r = cr[None] * pr[:, :, None, :] - ci[None] * pi[:, :, None, :]
        cl_i = cr[None] * pi[:, :, None, :] + ci[None] * pr[:, :, None, :]
        kk = es('ngop,gpi->ngoi', cl_r[:S5C], br) - es('ngop,gpi->ngoi', cl_i[:S5C], bi)
        ji, jo = jnp.arange(S5C)[:, None], jnp.arange(S5C)[None, :]
        lag = (jo - ji) if d == 0 else (ji - jo)
        kt = kk[jnp.clip(lag, 0, S5C - 1)]
        kt = jnp.where((lag >= 0)[:, :, None, None, None], kt, 0.0)
        toep = kt.transpose(2, 0, 4, 1, 3).reshape(S5G, S5C * 16, S5C * 16)
        m_idx = (jnp.arange(S5C) + 1) if d == 0 else (S5C - jnp.arange(S5C))
        wo_r = cl_r[m_idx].transpose(1, 3, 0, 2).reshape(S5G, S5P, S5C * 16)
        wo_i = -cl_i[m_idx].transpose(1, 3, 0, 2).reshape(S5G, S5P, S5C * 16)
        wout = jnp.concatenate([wo_r, wo_i], axis=1)
        e_idx = (S5C - 1 - jnp.arange(S5C)) if d == 0 else jnp.arange(S5C)
        lr, li = pr[e_idx], pi[e_idx]
        wi_r = (lr[..., None] * br[None] - li[..., None] * bi[None])
        wi_i = (lr[..., None] * bi[None] + li[..., None] * br[None])
        win = jnp.concatenate([wi_r.transpose(1, 0, 3, 2).reshape(S5G, S5C * 16, S5P),
                               wi_i.transpose(1, 0, 3, 2).reshape(S5G, S5C * 16, S5P)], axis=2)
        ar, ai = pr[S5C], pi[S5C]
        a1 = jnp.concatenate([ar, ar], axis=1)
        a2 = jnp.concatenate([-ai, ai], axis=1)
        return toep, wout, win, a1, a2

    tf, of, wf, a1f, a2f = build(0)
    tb, ob, wb, a1b, a2b = build(1)
    wz = jnp.concatenate([wf, wb], axis=2)
    wy = jnp.concatenate([tf + tb, of, ob], axis=1)
    return wz, wy, jnp.stack([a1f, a1b], axis=1), jnp.stack([a2f, a2b], axis=1)


def _gelu_tanh(x):
    return 0.5 * x * (1.0 + jnp.tanh(math.sqrt(2.0 / math.pi) * (x + 0.044715 * x * x * x)))


def _outproj_kernel(x_ref, mod_ref, gf_ref, gb_ref, gz_ref, ys_ref, us_ref, hf_ref, hb_ref, hg_ref,
                    rf_ref, rb_ref, rg_ref, vec_ref, glu_ref, ind_ref, w_ref, ln_ref, o_ref, *, alpha):
    ind = ind_ref[...]
    vec = vec_ref[...]

    def head_rms(o):
        return o * lax.rsqrt(_dot(o * o, ind, HI) * (1.0 / HEAD) + RMS_EPS)

    m_gdn = head_rms(gf_ref[0] + gb_ref[0]) * vec[0:1] * _silu(gz_ref[0])
    u = us_ref[0]
    ys = _gelu_tanh(ys_ref[0] + vec[2:3] * u)
    m_s5 = ys * _sigmoid(_dot(ys, glu_ref[...]) + vec[3:4])
    m_hg = head_rms(hf_ref[0] + hb_ref[0]) * vec[1:2] * _silu(hg_ref[0])
    m_rt = head_rms(rf_ref[0] + rb_ref[0]) * _silu(rg_ref[0])
    acc = None
    for k, mk in enumerate((m_gdn, m_s5, m_hg, m_rt)):
        part = _dot(mk.astype(BF16), w_ref[k * WM:(k + 1) * WM, :])
        acc = part if acc is None else acc + part
    m = mod_ref[0, 0]
    ln = ln_ref[...]
    o_ref[0] = _layer_norm(alpha * x_ref[0] + m[2:3] * acc, ln[0:1], ln[1:2])


def _outproj(xs, mod, g_of, g_ob, pg, ys, ps, h_of, h_ob, ph, r_of, r_ob, pr, vec, glu_w, ind_sum, w_out, ln, alpha):
    b, t, d = xs.shape
    tile = lambda c: pl.BlockSpec((1, TT, WM), lambda bi, i, c=c: (bi, i, c))
    full = lambda a: pl.BlockSpec(a.shape, lambda bi, i: (0,) * a.ndim)
    return pl.pallas_call(
        functools.partial(_outproj_kernel, alpha=alpha),
        grid=(b, t // TT),
        in_specs=[pl.BlockSpec((1, TT, d), lambda bi, i: (bi, i, 0)),
                  pl.BlockSpec((1, 1, 6, d), lambda bi, i: (bi, jnp.minimum(i, 1), 0, 0)),
                  tile(0), tile(0), tile(3), tile(0), tile(0), tile(0), tile(0), tile(4),
                  tile(0), tile(0), tile(3),
                  full(vec), full(glu_w), full(ind_sum), full(w_out), full(ln)],
        out_specs=pl.BlockSpec((1, TT, d), lambda bi, i: (bi, i, 0)),
        out_shape=jax.ShapeDtypeStruct(xs.shape, F32),
        compiler_params=_cp("parallel", "parallel"),
        name="outproj",
    )(xs, mod, g_of, g_ob, pg, ys, ps, h_of, h_ob, ph, r_of, r_ob, pr, vec, glu_w, ind_sum, w_out, ln)


def _ffn_kernel(x_ref, mod_ref, w1_ref, w3_ref, w2_ref, ln_ref, o_ref, *, alpha):
    m = mod_ref[0, 0]
    x = x_ref[0]
    h = (x * (1.0 + m[4:5]) + m[3:4]).astype(BF16)
    act = (_silu(_dot(h, w1_ref[...])) * _dot(h, w3_ref[...])).astype(BF16)
    y = _dot(act, w2_ref[...])
    ln = ln_ref[...]
    o_ref[0] = _layer_norm(alpha * x + m[5:6] * y, ln[0:1], ln[1:2])


def _ffn(xs, mod, w1, w3, w2, ln, alpha):
    b, t, d = xs.shape
    full = lambda a: pl.BlockSpec(a.shape, lambda bi, i: (0,) * a.ndim)
    return pl.pallas_call(
        functools.partial(_ffn_kernel, alpha=alpha),
        grid=(b, t // TT),
        in_specs=[pl.BlockSpec((1, TT, d), lambda bi, i: (bi, i, 0)),
                  pl.BlockSpec((1, 1, 6, d), lambda bi, i: (bi, jnp.minimum(i, 1), 0, 0)),
                  full(w1), full(w3), full(w2), full(ln)],
        out_specs=pl.BlockSpec((1, TT, d), lambda bi, i: (bi, i, 0)),
        out_shape=jax.ShapeDtypeStruct(xs.shape, F32),
        compiler_params=_cp("parallel", "parallel"),
        name="ffn_dense",
    )(xs, mod, w1, w3, w2, ln)


def _router_kernel(x_ref, mod_ref, r_ref, h_ref, rt_ref):
    m = mod_ref[0, 0]
    h = x_ref[0] * (1.0 + m[4:5]) + m[3:4]
    h_ref[0] = h
    lane = lax.broadcasted_iota(jnp.int32, (TT, 128), 1)
    logits = jnp.where(lane < N_EXP, _dot(h, r_ref[...], HI), -jnp.inf)
    m1 = jnp.max(logits, axis=-1, keepdims=True)
    i1 = jnp.min(jnp.where(logits == m1, lane, 128), axis=-1, keepdims=True)
    rest = jnp.where(lane == i1, -jnp.inf, logits)
    m2 = jnp.max(rest, axis=-1, keepdims=True)
    i2 = jnp.min(jnp.where(rest == m2, lane, 128), axis=-1, keepdims=True)
    e = jnp.exp(m2 - m1)
    g1 = 1.0 / (1.0 + e)
    g2 = e / (1.0 + e)
    rt_ref[0] = jnp.where(lane == 0, i1.astype(F32),
                          jnp.where(lane == 1, i2.astype(F32),
                                    jnp.where(lane == 2, g1, jnp.where(lane == 3, g2, 0.0))))


def _router(xs, mod, router_pad):
    b, t, d = xs.shape
    return pl.pallas_call(
        _router_kernel,
        grid=(b, t // TT),
        in_specs=[pl.BlockSpec((1, TT, d), lambda bi, i: (bi, i, 0)),
                  pl.BlockSpec((1, 1, 6, d), lambda bi, i: (bi, jnp.minimum(i, 1), 0, 0)),
                  pl.BlockSpec(router_pad.shape, lambda bi, i: (0, 0))],
        out_specs=[pl.BlockSpec((1, TT, d), lambda bi, i: (bi, i, 0)),
                   pl.BlockSpec((1, TT, 128), lambda bi, i: (bi, i, 0))],
        out_shape=[jax.ShapeDtypeStruct(xs.shape, F32), jax.ShapeDtypeStruct((b, t, 128), F32)],
        compiler_params=_cp("parallel", "parallel"),
        name="moe_router",
    )(xs, mod, router_pad)


def _row_copy(src_hbm, row, dst_ref, slot, sem):
    return pltpu.make_async_copy(src_hbm.at[pl.ds(row, 1), :], dst_ref.at[pl.ds(slot, 1), :], sem)


def _gather_kernel(idx_ref, src_hbm, o_ref, sem, *, rows):
    base = pl.program_id(0) * rows

    def start(r, c):
        _row_copy(src_hbm, idx_ref[base + r], o_ref, r, sem).start()
        return c

    def wait(r, c):
        _row_copy(src_hbm, 0, o_ref, r, sem).wait()
        return c

    lax.fori_loop(0, rows, start, 0)
    lax.fori_loop(0, rows, wait, 0)


def _gather_rows(src, idx, rows):
    n_out = idx.shape[0]
    d = src.shape[1]
    return pl.pallas_call(
        functools.partial(_gather_kernel, rows=rows),
        grid_spec=pltpu.PrefetchScalarGridSpec(
            num_scalar_prefetch=1,
            grid=(n_out // rows,),
            in_specs=[pl.BlockSpec(memory_space=pl.ANY)],
            out_specs=pl.BlockSpec((rows, d), lambda j, idx_ref: (j, 0)),
            scratch_shapes=[pltpu.SemaphoreType.DMA(())]),
        out_shape=jax.ShapeDtypeStruct((n_out, d), src.dtype),
        compiler_params=_cp("arbitrary"),
        name="moe_gather",
    )(idx, src)


def _experts_kernel(be_ref, x_ref, w1_ref, w3_ref, w2_ref, y_ref):
    f = pl.program_id(1)
    x = x_ref[...].astype(BF16)
    act = (_silu(_dot(x, w1_ref[0])) * _dot(x, w3_ref[0])).astype(BF16)
    y = _dot(act, w2_ref[0])

    @pl.when(f == 0)
    def _():
        y_ref[...] = y

    @pl.when(f != 0)
    def _():
        y_ref[...] += y


def _experts(xs_sorted, block_expert, w1, w3, w2):
    n_rows, d = xs_sorted.shape
    ff = w1.shape[2]
    nf = 2
    tf = ff // nf
    return pl.pallas_call(
        _experts_kernel,
        grid_spec=pltpu.PrefetchScalarGridSpec(
            num_scalar_prefetch=1,
            grid=(n_rows // MOE_RB, nf),
            in_specs=[pl.BlockSpec((MOE_RB, d), lambda j, f, be: (j, 0)),
                      pl.BlockSpec((1, d, tf), lambda j, f, be: (be[j], 0, f)),
                      pl.BlockSpec((1, d, tf), lambda j, f, be: (be[j], 0, f)),
                      pl.BlockSpec((1, tf, d), lambda j, f, be: (be[j], f, 0))],
            out_specs=pl.BlockSpec((MOE_RB, d), lambda j, f, be: (j, 0))),
        out_shape=jax.ShapeDtypeStruct((n_rows, d), F32),
        compiler_params=_cp("parallel", "arbitrary"),
        name="moe_experts",
    )(block_expert, xs_sorted, w1, w3, w2)


def _combine_kernel(dest_ref, x_ref, mod_ref, rt_ref, ln_ref, y_hbm, o_ref, buf, sem, *, alpha, tiles_per_batch):
    tile = pl.program_id(0) * tiles_per_batch + pl.program_id(1)
    base = tile * (2 * TT)

    def start(r, c):
        _row_copy(y_hbm, dest_ref[base + r], buf, r, sem).start()
        return c

    def wait(r, c):
        _row_copy(y_hbm, 0, buf, r, sem).wait()
        return c

    lax.fori_loop(0, 2 * TT, start, 0)
    lax.fori_loop(0, 2 * TT, wait, 0)
    rt = rt_ref[0]
    y = rt[:, 2:3] * buf[0:TT, :] + rt[:, 3:4] * buf[TT:2 * TT, :]
    m = mod_ref[0, 0]
    ln = ln_ref[...]
    o_ref[0] = _layer_norm(alpha * x_ref[0] + m[5:6] * y, ln[0:1], ln[1:2])


def _combine(xs, mod, rt, ln, ys_sorted, dest_tiles, alpha):
    b, t, d = xs.shape
    return pl.pallas_call(
        functools.partial(_combine_kernel, alpha=alpha, tiles_per_batch=t // TT),
        grid_spec=pltpu.PrefetchScalarGridSpec(
            num_scalar_prefetch=1,
            grid=(b, t // TT),
            in_specs=[pl.BlockSpec((1, TT, d), lambda bi, i, dr: (bi, i, 0)),
                      pl.BlockSpec((1, 1, 6, d), lambda bi, i, dr: (bi, jnp.minimum(i, 1), 0, 0)),
                      pl.BlockSpec((1, TT, 128), lambda bi, i, dr: (bi, i, 0)),
                      pl.BlockSpec(ln.shape, lambda bi, i, dr: (0, 0)),
                      pl.BlockSpec(memory_space=pl.ANY)],
            out_specs=pl.BlockSpec((1, TT, d), lambda bi, i, dr: (bi, i, 0)),
            scratch_shapes=[pltpu.VMEM((2 * TT, d), F32), pltpu.SemaphoreType.DMA(())]),
        out_shape=jax.ShapeDtypeStruct(xs.shape, F32),
        compiler_params=_cp("arbitrary", "arbitrary"),
        name="moe_combine",
    )(dest_tiles, xs, mod, rt, ln, ys_sorted)


def _moe(xs, mod, router_pad, w1, w3, w2, ln, alpha):
    b, t, d = xs.shape
    n_tok = b * t
    h, rt = _router(xs, mod, router_pad)
    e_idx = rt[..., 0:2].astype(jnp.int32).reshape(n_tok * 2)
    onehot = (e_idx[:, None] == jnp.arange(N_EXP, dtype=jnp.int32)[None, :]).astype(jnp.int32)
    csum = jnp.cumsum(onehot, axis=0)
    counts = csum[-1]
    rank = jnp.take_along_axis(csum, e_idx[:, None], axis=1)[:, 0] - 1
    padded = (counts + MOE_RB - 1) // MOE_RB * MOE_RB
    pad_end = jnp.cumsum(padded)
    pad_start = pad_end - padded
    dest = pad_start[e_idx] + rank
    n_blocks = -(-(n_tok * 2) // MOE_RB) + N_EXP
    n_rows = n_blocks * MOE_RB
    row_tok = jnp.zeros((n_rows,), jnp.int32).at[dest].set(jnp.arange(n_tok * 2, dtype=jnp.int32) // 2)
    block_expert = jnp.minimum(
        jnp.sum(jnp.arange(n_blocks, dtype=jnp.int32)[:, None] * MOE_RB >= pad_end[None, :], axis=1),
        N_EXP - 1).astype(jnp.int32)
    xs_sorted = _gather_rows(h.reshape(n_tok, d), row_tok, MOE_RB)
    ys_sorted = _experts(xs_sorted, block_expert, w1, w3, w2)
    dest_tiles = dest.reshape(n_tok // TT, TT, 2).transpose(0, 2, 1).reshape(n_tok * 2)
    return _combine(xs, mod, rt, ln, ys_sorted, dest_tiles, alpha)


def _ret_constants(decay_param):
    log_gamma = -jnp.exp(decay_param)
    idx = jnp.arange(TT, dtype=F32)
    diff = idx[:, None] - idx[None, :]
    lg = log_gamma[:, :, None, None]
    dec_f = jnp.exp(jnp.where(diff >= 0, diff * lg[0], -jnp.inf))
    dec_b = jnp.exp(jnp.where(diff <= 0, -diff * lg[1], -jnp.inf))
    dec = jnp.stack([dec_f, dec_b])
    rep = lambda a: jnp.repeat(a, HEAD, axis=-1)
    qdec = jnp.stack([rep(jnp.exp((idx[:, None] + 1.0) * log_gamma[0][None, :])),
                      rep(jnp.exp((TT - idx[:, None]) * log_gamma[1][None, :]))])
    kdec = jnp.stack([rep(jnp.exp((TT - 1.0 - idx[:, None]) * log_gamma[0][None, :])),
                      rep(jnp.exp(idx[:, None] * log_gamma[1][None, :]))])
    cdec = jnp.exp(TT * log_gamma)
    return dec, qdec, kdec, cdec


def _rotary_tables(n_ctx_tok, n_lat, grid_w):
    rows = jnp.repeat(jnp.arange(n_lat // grid_w, dtype=F32), grid_w)
    cols = jnp.tile(jnp.arange(grid_w, dtype=F32), n_lat // grid_w)
    quarter = HEAD // 4
    inv_freq = ROPE_BASE ** (-jnp.arange(quarter, dtype=F32) / quarter)
    ang = jnp.concatenate([rows[:, None] * inv_freq, cols[:, None] * inv_freq], axis=-1)
    cos, sin = jnp.cos(ang), jnp.sin(ang)
    cos_h = jnp.concatenate([cos, cos], axis=-1)
    sin_h = jnp.concatenate([-sin, sin], axis=-1)
    cos_t = jnp.concatenate([jnp.ones((n_ctx_tok, HEAD), F32), cos_h], axis=0)
    sin_t = jnp.concatenate([jnp.zeros((n_ctx_tok, HEAD), F32), sin_h], axis=0)
    return jnp.tile(cos_t, (1, NH)), jnp.tile(sin_t, (1, NH))


def _swap_halves_cols(w):
    dm = w.shape[0]
    return w.reshape(dm, NH, 2, HEAD // 2)[:, :, ::-1, :].reshape(dm, WM)


def _prep_w_in(w):
    dm = w.shape[0]
    r0 = 3344 - 768
    rq, rk = w[:, r0:r0 + WM], w[:, r0 + WM:r0 + 2 * WM]
    return jnp.concatenate([w[:, :1040], jnp.zeros((dm, W_AB - 16), w.dtype), w[:, 1040:],
                            _swap_halves_cols(rq), _swap_halves_cols(rk)], axis=1).astype(BF16)


def kernel(x, c, ctx, c_ctx, ada_w, ada_b, w_in, w_out, ln_g, ln_b, gdn_conv_w, gdn_a_log, gdn_dt_bias, gdn_norm_w, s5_a_re, s5_a_im, s5_log_step, s5_b_re, s5_b_im, s5_c_re, s5_c_im, s5_d, s5_glu_w, s5_glu_b, hgrn_lower_bounds, hgrn_norm_w, ret_decay, ffn_w1, ffn_w3, ffn_w2, moe_router, moe_w1, moe_w3, moe_w2):
    bsz, n_lat, d = x.shape
    n_ctx_tok = ctx.shape[1]
    depth = ada_w.shape[0]
    grid_w = 64
    assert n_ctx_tok == TT and n_lat % TT == 0 and bsz <= 7
    t = n_ctx_tok + n_lat
    alpha = (2.0 * depth) ** 0.25

    xs = jnp.concatenate([ctx, x], axis=1)
    cs = jnp.concatenate([c, c_ctx[None, :], jnp.zeros((8 - bsz - 1, d), F32)], axis=0)
    mod_all = _ada_mod(cs, ada_w, ada_b)
    lat_mod = mod_all[:, :bsz].reshape(depth, bsz, 6, d)
    ctx_mod = jnp.broadcast_to(mod_all[:, bsz].reshape(depth, 1, 6, d), (depth, bsz, 6, d))
    mod_tab = jnp.stack([ctx_mod, lat_mod], axis=2)

    lb_all = jnp.cumsum(jax.nn.softmax(hgrn_lower_bounds.astype(F32), axis=0), axis=0)
    lb_all = lb_all - lb_all[0]
    ind_sum = jnp.asarray(np.kron(np.eye(NH), np.ones((HEAD, HEAD))), F32)
    tri = jnp.asarray(np.tril(np.ones((CH, CH))), F32)
    mall_np, masks_np = _hgrn_constants()
    mall, masks = jnp.asarray(mall_np), jnp.asarray(masks_np)
    cos_t, sin_t = _rotary_tables(n_ctx_tok, n_lat, grid_w)
    nch = t // S5C

    for layer in range(depth):
        mod = mod_tab[layer]
        pg, pab, ps, ph, pr = _inproj(xs, mod, _prep_w_in(w_in[layer]))

        gq = _gdn_prep(pg, gdn_conv_w[layer], ind_sum)
        pabt = pab[..., :16].reshape(bsz, t // CH, CH, 16).transpose(0, 1, 3, 2)
        g_of, g_ob = _gdn_scan(gq, pab, pabt, -jnp.exp(gdn_a_log[layer]), gdn_dt_bias[layer], tri, n_ctx_tok // CH)

        wz, wy, a1, a2 = _s5_weights(s5_a_re[layer], s5_a_im[layer], s5_log_step[layer], s5_b_re[layer],
                                     s5_b_im[layer], s5_c_re[layer], s5_c_im[layer])
        u4 = ps.reshape(bsz, nch, S5C, S5G, 16).transpose(0, 3, 1, 2, 4).reshape(bsz, S5G, nch, S5C * 16)
        y4 = _s5_scan(u4, wz, wy, a1, a2, n_ctx_tok // S5C)
        ys = y4.reshape(bsz, S5G, nch, S5C, 16).transpose(0, 2, 3, 1, 4).reshape(bsz, t, WM)

        lb = lb_all[layer][None, :]
        lbp = jnp.concatenate([jnp.log(lb), jnp.log1p(-lb), 1.0 - lb, jnp.zeros((5, WM), F32)], axis=0)
        h_of, h_ob = _hgrn_scan(ph, lbp, mall, masks, ind_sum, n_ctx_tok // CH)

        dec, qdec, kdec, cdec = _ret_constants(ret_decay[layer])
        r_of, r_ob = _ret_scan(pr, cos_t, sin_t, dec, qdec, kdec, cdec)

        vec = jnp.concatenate([jnp.tile(gdn_norm_w[layer], NH)[None], jnp.tile(hgrn_norm_w[layer], NH)[None],
                               s5_d[layer][None], s5_glu_b[layer][None], jnp.zeros((4, WM), F32)], axis=0)
        xs = _outproj(xs, mod, g_of, g_ob, pg, ys, ps, h_of, h_ob, ph, r_of, r_ob, pr, vec, s5_glu_w[layer],
                      ind_sum, w_out[layer].astype(BF16), jnp.stack([ln_g[layer, 0], ln_b[layer, 0]]), alpha)

        j = layer // 2
        ln2 = jnp.stack([ln_g[layer, 1], ln_b[layer, 1]])
        if layer % 2 == 0:
            xs = _ffn(xs, mod, ffn_w1[j].astype(BF16), ffn_w3[j].astype(BF16), ffn_w2[j].astype(BF16), ln2, alpha)
        else:
            router_pad = jnp.concatenate([moe_router[j], jnp.zeros((d, 128 - N_EXP), F32)], axis=1)
            xs = _moe(xs, mod, router_pad, moe_w1[j].astype(BF16), moe_w3[j].astype(BF16),
                      moe_w2[j].astype(BF16), ln2, alpha)
    return xs[:, n_ctx_tok:, :]
```

```python
import functools
import math

import numpy as np
import jax
import jax.numpy as jnp
from jax import lax
from jax.experimental import pallas as pl
from jax.experimental.pallas import tpu as pltpu

F32 = jnp.float32
BF16 = jnp.bfloat16
HI = lax.Precision.HIGHEST

HEAD = 64
NH = 4
WM = NH * HEAD
TT = 256
CH = 64
S5C = 16
S5G = 16
S5P = 64
S5GB = 4
N_EXP = 8
MOE_RB = 512
LN_EPS = 1e-5
RMS_EPS = 1e-6
ROPE_BASE = 10000.0
VMEM_LIMIT = 56 * 1024 * 1024


def _cp(*sem):
    return pltpu.CompilerParams(dimension_semantics=sem, vmem_limit_bytes=VMEM_LIMIT)


def _sigmoid(x):
    return 1.0 / (1.0 + jnp.exp(-x))


def _silu(x):
    return x * _sigmoid(x)


def _softplus(x):
    return jnp.maximum(x, 0.0) + jnp.log1p(jnp.exp(-jnp.abs(x)))


def _dot(a, b, precision=None):
    return jnp.dot(a, b, preferred_element_type=F32, precision=precision)


def _dot_nt(a, b, precision=None):
    return lax.dot_general(a, b, (((1,), (1,)), ((), ())), preferred_element_type=F32, precision=precision)


def _dot_tn(a, b, precision=None):
    return lax.dot_general(a, b, (((0,), (0,)), ((), ())), preferred_element_type=F32, precision=precision)


def _layer_norm(y, g, b):
    mu = jnp.mean(y, axis=-1, keepdims=True)
    yc = y - mu
    var = jnp.mean(yc * yc, axis=-1, keepdims=True)
    return yc * lax.rsqrt(var + LN_EPS) * g + b


def _rev_index(n, n_ctx, n_all):
    return jnp.where(n < n_ctx, n_ctx - 1 - n, n_all + n_ctx - 1 - n)


def _ada_kernel(c_ref, w_ref, b_ref, o_ref):
    o_ref[0] = _dot(_silu(c_ref[...]), w_ref[0], HI) + b_ref[0]


def _ada_mod(cs, ada_w, ada_b):
    depth, d, d6 = ada_w.shape
    tn = 1024
    return pl.pallas_call(
        _ada_kernel,
        grid=(depth, d6 // tn),
        in_specs=[pl.BlockSpec((8, d), lambda l, j: (0, 0)),
                  pl.BlockSpec((1, d, tn), lambda l, j: (l, 0, j)),
                  pl.BlockSpec((1, 1, tn), lambda l, j: (l, 0, j))],
        out_specs=pl.BlockSpec((1, 8, tn), lambda l, j: (l, 0, j)),
        out_shape=jax.ShapeDtypeStruct((depth, 8, d6), F32),
        compiler_params=_cp("parallel", "parallel"),
        name="ada_mod",
    )(cs, ada_w, ada_b.reshape(depth, 1, d6))


W_PG, W_AB, W_S5, W_PH, W_PR = 1024, 128, 256, 1280, 1536
P_OFF = np.cumsum([0, W_PG, W_AB, W_S5, W_PH, W_PR])


def _inproj_kernel(x_ref, mod_ref, w_ref, pg_ref, pab_ref, ps_ref, ph_ref, pr_ref):
    m = mod_ref[0, 0]
    h = (x_ref[0] * (1.0 + m[1:2]) + m[0:1]).astype(BF16)
    for k, o_ref in enumerate((pg_ref, pab_ref, ps_ref, ph_ref, pr_ref)):
        o_ref[0] = _dot(h, w_ref[:, P_OFF[k]:P_OFF[k + 1]])


def _inproj(xs, mod, w):
    b, t, d = xs.shape
    widths = (W_PG, W_AB, W_S5, W_PH, W_PR)
    return pl.pallas_call(
        _inproj_kernel,
        grid=(b, t // TT),
        in_specs=[pl.BlockSpec((1, TT, d), lambda bi, i: (bi, i, 0)),
                  pl.BlockSpec((1, 1, 6, d), lambda bi, i: (bi, jnp.minimum(i, 1), 0, 0)),
                  pl.BlockSpec(w.shape, lambda bi, i: (0, 0))],
        out_specs=[pl.BlockSpec((1, TT, wd), lambda bi, i: (bi, i, 0)) for wd in widths],
        out_shape=[jax.ShapeDtypeStruct((b, t, wd), F32) for wd in widths],
        compiler_params=_cp("parallel", "parallel"),
        name="inproj",
    )(xs, mod, w)


def _pieces(x, n):
    out, r = [], x
    for i in range(n):
        p = r.astype(BF16)
        out.append(p)
        if i + 1 < n:
            r = r - p.astype(F32)
    return out


def _dot_pieces(a_parts, b_parts, dot=_dot):
    n = max(len(a_parts), len(b_parts))
    acc = None
    for i, ap in enumerate(a_parts):
        for j, bp in enumerate(b_parts):
            if i + j < n:
                t = dot(ap, bp)
                acc = t if acc is None else acc + t
    return acc


def _sum01_l(m01, x, n):
    return _dot_pieces([m01], _pieces(x, n))


def _sum01_r(x, m01, n, dot=_dot):
    return _dot_pieces(_pieces(x, n), [m01], dot)


GDN_INV_PIECES = (2, 2, 2, 2, 1, 1)


def _gdn_local_kernel(na_ref, dtb_ref, p_ref, pv_ref, nx_ref, cw_ref, ind_ref, ab_ref, abt_ref, tri_ref, trit_ref,
                      ones_ref, uf_ref, ub_ref, wf_ref, wb_ref, qkf_ref, qkb_ref, qgf_ref, qgb_ref, kdf_ref, kdb_ref,
                      eg_ref):
    i = pl.program_id(1)
    nt = pl.num_programs(1)
    x = p_ref[0]
    prev = jnp.where(i >= 2, pv_ref[0][7:8], 0.0)
    nxt = jnp.where((i >= 1) & (i < nt - 1), nx_ref[0][0:1], 0.0)
    row1 = lax.broadcasted_iota(jnp.int32, (TT, 1), 0)
    xm = jnp.where(row1 == 0, prev, pltpu.roll(x, 1, 0))
    xp = jnp.where(row1 == TT - 1, nxt, pltpu.roll(x, TT - 1, 0))
    cw = cw_ref[...]
    y = _silu(cw[0:1] * xm + cw[1:2] * x + cw[2:3] * xp)
    q, k, v = y[:, :WM], y[:, WM:2 * WM], y[:, 2 * WM:]
    ind = ind_ref[...]
    q = q * lax.rsqrt(_sum01_r(q * q, ind, 2) + RMS_EPS) * HEAD ** -0.5
    k = k * lax.rsqrt(_sum01_r(k * k, ind, 2) + RMS_EPS)

    ab = ab_ref[0]
    abt = abt_ref[0]
    ones_bd = ones_ref[...]
    row = lax.broadcasted_iota(jnp.int32, (TT, TT), 0)
    col = lax.broadcasted_iota(jnp.int32, (TT, TT), 1)
    same = (row // CH) == (col // CH)
    out_refs = ((uf_ref, wf_ref, qkf_ref, qgf_ref, kdf_ref), (ub_ref, wb_ref, qkb_ref, qgb_ref, kdb_ref))
    e_last = []
    for d in range(2):
        incl = same & ((row >= col) if d == 0 else (row <= col))
        strict = same & ((row > col) if d == 0 else (row < col))
        tri_c = tri_ref[...] if d == 0 else trit_ref[...]
        a_col, b_col = ab[:, 4 * d:4 * d + 4], ab[:, 8 + 4 * d:12 + 4 * d]
        a_row = abt[4 * d:4 * d + 4, :]
        g_col = jnp.concatenate([na_ref[d, h] * _softplus(a_col[:, h:h + 1] + dtb_ref[d, h]) for h in range(NH)], axis=1)
        g_row = jnp.concatenate([na_ref[d, h] * _softplus(a_row[h:h + 1, :] + dtb_ref[d, h]) for h in range(NH)], axis=0)
        gc_col = _sum01_l(tri_c, g_col, 3)
        gc_row = _sum01_r(g_row, tri_c, 3, _dot_nt)
        gl_col = _sum01_l(ones_bd, g_col, 3)
        e_last.append(jnp.exp(gl_col))
        us, ws, qks, qgs, kds = [], [], [], [], []
        for h in range(NH):
            sl = slice(h * HEAD, (h + 1) * HEAD)
            qh, kh, vh = q[:, sl], k[:, sl], v[:, sl]
            beta = _sigmoid(b_col[:, h:h + 1])
            gcc = gc_col[:, h:h + 1]
            decay = jnp.exp(jnp.where(incl, gcc - gc_row[h:h + 1, :], -jnp.inf))
            kb = kh * beta
            a = _dot_nt(kb, kh) * jnp.where(strict, decay, 0.0)
            xx = jnp.concatenate([vh * beta, kb * jnp.exp(gcc)], axis=1)
            pp = _pieces(a, GDN_INV_PIECES[0])
            xx = xx - _dot_pieces(pp, _pieces(xx, GDN_INV_PIECES[0]))
            for npc in GDN_INV_PIECES[1:]:
                pp = _pieces(_dot_pieces(pp[:npc], pp[:npc]), npc)
                xx = xx + _dot_pieces(pp, _pieces(xx, npc))
            us.append(xx[:, :HEAD])
            ws.append(xx[:, HEAD:])
            qk = _dot_nt(qh, kh) * decay
            qks.append(qk[:, 0:CH] + qk[:, CH:2 * CH] + qk[:, 2 * CH:3 * CH] + qk[:, 3 * CH:4 * CH])
            qgs.append(qh * jnp.exp(gcc))
            kds.append(kh * jnp.exp(gl_col[:, h:h + 1] - gcc))
        u_ref, w_ref, qk_ref, qg_ref, kd_ref = out_refs[d]
        u_ref[0] = jnp.concatenate(us, axis=1)
        w_ref[0] = jnp.concatenate(ws, axis=1).astype(BF16)
        qk_ref[0] = jnp.concatenate(qks, axis=1).astype(BF16)
        qg_ref[0] = jnp.concatenate(qgs, axis=1).astype(BF16)
        kd_ref[0] = jnp.concatenate(kds, axis=1).astype(BF16)
    eg_ref[0] = jnp.concatenate(e_last + [jnp.zeros((TT, 128 - 2 * NH), F32)], axis=1)


def _gdn_local(pg, pab, pabt, conv_w, ind_sum, neg_a, dt_bias, tri_bd, ones_bd):
    b, t, _ = pg.shape
    w3 = 3 * WM
    n8 = t // 8
    smem = pl.BlockSpec(memory_space=pltpu.SMEM)
    full = lambda a: pl.BlockSpec(a.shape, lambda bi, i: (0,) * a.ndim)
    tile = pl.BlockSpec((1, TT, WM), lambda bi, i: (bi, i, 0))
    f32o = jax.ShapeDtypeStruct((b, t, WM), F32)
    b16o = jax.ShapeDtypeStruct((b, t, WM), BF16)
    return pl.pallas_call(
        _gdn_local_kernel,
        grid=(b, t // TT),
        in_specs=[smem, smem,
                  pl.BlockSpec((1, TT, w3), lambda bi, i: (bi, i, 0)),
                  pl.BlockSpec((1, 8, w3), lambda bi, i: (bi, jnp.maximum(i * (TT // 8) - 1, 0), 0)),
                  pl.BlockSpec((1, 8, w3), lambda bi, i: (bi, jnp.minimum((i + 1) * (TT // 8), n8 - 1), 0)),
                  pl.BlockSpec((3, w3), lambda bi, i: (0, 0)),
                  full(ind_sum),
                  pl.BlockSpec((1, TT, W_AB), lambda bi, i: (bi, i, 0)),
                  pl.BlockSpec((1, 16, TT), lambda bi, i: (bi, 0, i)),
                  full(tri_bd), full(tri_bd), full(ones_bd)],
        out_specs=[tile] * 10 + [pl.BlockSpec((1, TT, 128), lambda bi, i: (bi, i, 0))],
        out_shape=[f32o, f32o] + [b16o] * 8 + [jax.ShapeDtypeStruct((b, t, 128), F32)],
        compiler_params=_cp("parallel", "parallel"),
        name="gdn_local",
    )(neg_a, dt_bias, pg, pg, pg, conv_w, ind_sum, pab, pabt, tri_bd, tri_bd.T, ones_bd)


def _gdn_scan_kernel(uf_ref, ub_ref, wf_ref, wb_ref, qkf_ref, qkb_ref, qgf_ref, qgb_ref, kdf_ref, kdb_ref,
                     egf_ref, egb_ref, of_ref, ob_ref, s_ref):
    @pl.when(pl.program_id(0) == 0)
    def _():
        s_ref[...] = jnp.zeros_like(s_ref)

    dirs = ((uf_ref, wf_ref, qkf_ref, qgf_ref, kdf_ref, egf_ref, of_ref),
            (ub_ref, wb_ref, qkb_ref, qgb_ref, kdb_ref, egb_ref, ob_ref))
    nb = uf_ref.shape[0]
    old = {(b, d, h): s_ref[b, d, h] for b in range(nb) for d in range(2) for h in range(NH)}
    new = {}
    for b in range(nb):
        for d, (u_ref, w_ref, qk_ref, qg_ref, kd_ref, eg_ref, o_ref) in enumerate(dirs):
            u, w, qk, qg, kd = u_ref[b], w_ref[b], qk_ref[b], qg_ref[b], kd_ref[b]
            eg = eg_ref[b][0:1, :]
            outs = []
            for h in range(NH):
                sl = slice(h * HEAD, (h + 1) * HEAD)
                s = old[b, d, h]
                sb = s.astype(BF16)
                v_new = u[:, sl] - _dot(w[:, sl], sb)
                vb = v_new.astype(BF16)
                outs.append(_dot(qg[:, sl], sb) + _dot(qk[:, sl], vb))
                new[b, d, h] = s * eg[:, 4 * d + h:4 * d + h + 1] + _dot_tn(kd[:, sl], vb)
            o_ref[b] = jnp.concatenate(outs, axis=1)
    for key, val in new.items():
        s_ref[key] = val


def _gdn_scan(loc, n_ctx):
    uf, ub, wf, wb, qkf, qkb, qgf, qgb, kdf, kdb, eg = loc
    b, t, _ = uf.shape
    nc = t // CH
    fwd = lambda n: (0, n, 0)
    bwd = lambda n: (0, _rev_index(n, n_ctx, nc), 0)
    blk = lambda im, w=WM: pl.BlockSpec((b, CH, w), im)
    return pl.pallas_call(
        _gdn_scan_kernel,
        grid=(nc,),
        in_specs=[blk(fwd), blk(bwd)] * 5 + [blk(fwd, 128), blk(bwd, 128)],
        out_specs=[blk(fwd), blk(bwd)],
        out_shape=[jax.ShapeDtypeStruct((b, t, WM), F32)] * 2,
        scratch_shapes=[pltpu.VMEM((b, 2, NH, HEAD, HEAD), F32)],
        compiler_params=_cp("arbitrary"),
        name="gdn_scan",
    )(uf, ub, wf, wb, qkf, qkb, qgf, qgb, kdf, kdb, eg, eg)


HG_LEVELS = (32, 16, 8, 4, 2, 1)


def _hgrn_constants():
    idx = np.arange(CH)
    i, t = idx[:, None], idx[None, :]
    blocks = [(t <= i), (t > i)]
    masks = []
    for s in HG_LEVELS:
        m = (idx // (2 * s)) * 2 * s + s
        sec = (idx % (2 * s)) >= s
        mi = m[:, None]
        blocks.append(sec[:, None] & (t >= mi) & (t <= i))
        blocks.append((~sec)[:, None] & (t > i) & (t <= mi - 1))
        same = (idx[:, None] // (2 * s)) == (idx[None, :] // (2 * s))
        masks.append(same & sec[:, None] & (~sec)[None, :])
    mall_f = np.concatenate(blocks, axis=0).astype(np.float32)
    masks_f = np.stack(masks).astype(np.float32)
    nb = len(blocks)
    mall_b = mall_f.reshape(nb, CH, CH)[:, ::-1, ::-1].reshape(nb * CH, CH)
    masks_b = masks_f[:, ::-1, ::-1]
    eye = np.eye(TT // CH, dtype=np.float32)
    bd = lambda m: np.stack([np.kron(eye, m[lv]) for lv in range(len(HG_LEVELS))])
    return np.stack([mall_f, mall_b]), np.stack([bd(masks_f), bd(masks_b)])


def _hgrn_local_kernel(ph_ref, lbp_ref, mall_ref, mask_ref, ind_ref, ones_ref,
                       oi_ref, qgf_ref, qgb_ref, kvf_ref, kvb_ref, ecf_ref, ecb_ref):
    ph = ph_ref[0]
    lbp = lbp_ref[...]
    ind = ind_ref[...]
    ones = ones_ref[...]
    log_lb, log_1m_lb, one_m_lb = lbp[0:1], lbp[1:2], lbp[2:3]
    q = _silu(ph[:, :WM])
    v = ph[:, 3 * WM:4 * WM]
    ncl = TT // CH
    o_sum = None
    for d, (qg_ref, kv_ref, ec_ref) in enumerate(((qgf_ref, kvf_ref, ecf_ref), (qgb_ref, kvb_ref, ecb_ref))):
        fz = ph[:, WM * (1 + d):WM * (2 + d)]
        lsig = jnp.minimum(fz, 0.0) - jnp.log1p(jnp.exp(-jnp.abs(fz)))
        bb = log_1m_lb + lsig
        logf = jnp.maximum(log_lb, bb) + jnp.log1p(jnp.exp(-jnp.abs(log_lb - bb)))
        k = one_m_lb / (1.0 + jnp.exp(fz))
        e_c = [jnp.exp(_sum01_l(mall_ref[d], logf[c * CH:(c + 1) * CH], 3)) for c in range(ncl)]
        blk = lambda r: jnp.concatenate([e_c[c][r * CH:(r + 1) * CH] for c in range(ncl)], axis=0)
        qg_ref[0] = (q * blk(0)).astype(BF16)
        kd = k * blk(1)
        o_d = _sum01_r(q * k, ind, 2) * v
        q_lv = [q * blk(2 + 2 * lv) for lv in range(len(HG_LEVELS))]
        k_lv = [k * blk(3 + 2 * lv) for lv in range(len(HG_LEVELS))]
        outs = []
        for h in range(NH):
            sl = slice(h * HEAD, (h + 1) * HEAD)
            att = None
            for lv in range(len(HG_LEVELS)):
                term = _dot_nt(q_lv[lv][:, sl], k_lv[lv][:, sl]) * mask_ref[d, lv]
                att = term if att is None else att + term
            outs.append(_dot(att, v[:, sl]))
        o_d = o_d + jnp.concatenate(outs, axis=1)
        o_sum = o_d if o_sum is None else o_sum + o_d
        for c in range(ncl):
            rows = slice(c * CH, (c + 1) * CH)
            kv = _dot_tn(kd[rows], v[rows])
            tot = _sum01_r(logf[rows], ones, 3, _dot_tn)
            kv_ref[0, c] = jnp.concatenate([kv[h * HEAD:(h + 1) * HEAD, h * HEAD:(h + 1) * HEAD] for h in range(NH)], axis=1)
            ec_ref[0, c] = jnp.concatenate([jnp.exp(tot[h * HEAD:(h + 1) * HEAD]) for h in range(NH)], axis=1)
    oi_ref[0] = o_sum


def _hgrn_local(ph, lbp, mall, masks, ind_sum, ones_c):
    b, t, _ = ph.shape
    ncl = TT // CH
    full = lambda a: pl.BlockSpec(a.shape, lambda bi, i: (0,) * a.ndim)
    tile = pl.BlockSpec((1, TT, WM), lambda bi, i: (bi, i, 0))
    st_spec = pl.BlockSpec((1, ncl, HEAD, WM), lambda bi, i: (bi, i, 0, 0))
    st_shape = jax.ShapeDtypeStruct((b, t // CH, HEAD, WM), F32)
    return pl.pallas_call(
        _hgrn_local_kernel,
        grid=(b, t // TT),
        in_specs=[pl.BlockSpec((1, TT, W_PH), lambda bi, i: (bi, i, 0)),
                  full(lbp), full(mall), full(masks), full(ind_sum), full(ones_c)],
        out_specs=[tile, tile, tile, st_spec, st_spec, st_spec, st_spec],
        out_shape=[jax.ShapeDtypeStruct((b, t, WM), F32), jax.ShapeDtypeStruct((b, t, WM), BF16),
                   jax.ShapeDtypeStruct((b, t, WM), BF16), st_shape, st_shape, st_shape, st_shape],
        compiler_params=_cp("parallel", "parallel"),
        name="hgrn_local",
    )(ph, lbp, mall, masks, ind_sum, ones_c)


def _hgrn_scan_kernel(qgf_ref, qgb_ref, kvf_ref, kvb_ref, ecf_ref, ecb_ref, of_ref, ob_ref, s_ref):
    @pl.when(pl.program_id(0) == 0)
    def _():
        s_ref[...] = jnp.zeros_like(s_ref)

    dirs = ((qgf_ref, kvf_ref, ecf_ref, of_ref), (qgb_ref, kvb_ref, ecb_ref, ob_ref))
    for b in range(qgf_ref.shape[0]):
        for d, (qg_ref, kv_ref, ec_ref, o_ref) in enumerate(dirs):
            s = s_ref[b, d]
            sb = s.astype(BF16)
            qg = qg_ref[b]
            o_ref[b] = jnp.concatenate(
                [_dot(qg[:, h * HEAD:(h + 1) * HEAD], sb[:, h * HEAD:(h + 1) * HEAD]) for h in range(NH)], axis=1)
            s_ref[b, d] = s * ec_ref[b, 0] + kv_ref[b, 0]


def _hgrn_scan(qgf, qgb, kvf, kvb, ecf, ecb, n_ctx):
    b, t, _ = qgf.shape
    nc = t // CH
    fwd3 = lambda n: (0, n, 0)
    bwd3 = lambda n: (0, _rev_index(n, n_ctx, nc), 0)
    fwd4 = lambda n: (0, n, 0, 0)
    bwd4 = lambda n: (0, _rev_index(n, n_ctx, nc), 0, 0)
    tok = lambda im: pl.BlockSpec((b, CH, WM), im)
    st = lambda im: pl.BlockSpec((b, 1, HEAD, WM), im)
    return pl.pallas_call(
        _hgrn_scan_kernel,
        grid=(nc,),
        in_specs=[tok(fwd3), tok(bwd3), st(fwd4), st(bwd4), st(fwd4), st(bwd4)],
        out_specs=[tok(fwd3), tok(bwd3)],
        out_shape=[jax.ShapeDtypeStruct((b, t, WM), F32)] * 2,
        scratch_shapes=[pltpu.VMEM((b, 2, HEAD, WM), F32)],
        compiler_params=_cp("arbitrary"),
        name="hgrn_scan",
    )(qgf, qgb, kvf, kvb, ecf, ecb)


def _ret_direction(d, pr, cos, sin, dec_ref, qd, kd, cdec_ref, s_ref):
    q = (pr[:, :WM] * cos + pr[:, 4 * WM:5 * WM] * sin)
    k = (pr[:, WM:2 * WM] * cos + pr[:, 5 * WM:6 * WM] * sin) * HEAD ** -0.5
    v = pr[:, 2 * WM:3 * WM]
    q_in = q * qd
    k_in = k * kd
    outs = []
    for h in range(NH):
        sl = slice(h * HEAD, (h + 1) * HEAD)
        s = s_ref[d, h]
        att = _dot_nt(q[:, sl], k[:, sl]) * dec_ref[d, h]
        outs.append(_dot(att, v[:, sl]) + _dot(q_in[:, sl], s))
        s_ref[d, h] = s * cdec_ref[d, h] + _dot_tn(k_in[:, sl], v[:, sl])
    return jnp.concatenate(outs, axis=1)


def _ret_scan_kernel(cdec_ref, pf_ref, pb_ref, cf_ref, sf_ref, cb_ref, sb_ref, dec_ref, qd_ref, kd_ref,
                     of_ref, ob_ref, s_ref):
    @pl.when(pl.program_id(1) == 0)
    def _():
        s_ref[...] = jnp.zeros_like(s_ref)

    of_ref[0] = _ret_direction(0, pf_ref[0], cf_ref[...], sf_ref[...], dec_ref, qd_ref[0], kd_ref[0], cdec_ref, s_ref)
    ob_ref[0] = _ret_direction(1, pb_ref[0], cb_ref[...], sb_ref[...], dec_ref, qd_ref[1], kd_ref[1], cdec_ref, s_ref)


def _ret_scan(pr, cos_t, sin_t, dec, qdec, kdec, cdec):
    b, t, _ = pr.shape
    nt = t // TT
    fwd3 = lambda bi, n: (bi, n, 0)
    bwd3 = lambda bi, n: (bi, _rev_index(n, 1, nt), 0)
    fwd2 = lambda bi, n: (n, 0)
    bwd2 = lambda bi, n: (_rev_index(n, 1, nt), 0)
    return pl.pallas_call(
        _ret_scan_kernel,
        grid=(b, nt),
        in_specs=[pl.BlockSpec(memory_space=pltpu.SMEM),
                  pl.BlockSpec((1, TT, W_PR), fwd3), pl.BlockSpec((1, TT, W_PR), bwd3),
                  pl.BlockSpec((TT, WM), fwd2), pl.BlockSpec((TT, WM), fwd2),
                  pl.BlockSpec((TT, WM), bwd2), pl.BlockSpec((TT, WM), bwd2),
                  pl.BlockSpec(dec.shape, lambda bi, n: (0, 0, 0, 0)),
                  pl.BlockSpec(qdec.shape, lambda bi, n: (0, 0, 0)),
                  pl.BlockSpec(kdec.shape, lambda bi, n: (0, 0, 0))],
        out_specs=[pl.BlockSpec((1, TT, WM), fwd3), pl.BlockSpec((1, TT, WM), bwd3)],
        out_shape=[jax.ShapeDtypeStruct((b, t, WM), F32)] * 2,
        scratch_shapes=[pltpu.VMEM((2, NH, HEAD, HEAD), F32)],
        compiler_params=_cp("parallel", "arbitrary"),
        name="ret_scan",
    )(cdec, pr, pr, cos_t, sin_t, cos_t, sin_t, dec, qdec, kdec)


def _s5_kernel(u_ref, wz_ref, wy_ref, a1_ref, a2_ref, y_ref, z_ref, hp_ref, *, n_ctx):
    nch = u_ref.shape[2]
    for g in range(S5GB):
        z_ref[g] = _dot_pieces(_pieces(u_ref[0, g], 2), [wz_ref[0, g], wz_ref[1, g]])
    a1 = a1_ref[...]
    a2 = a2_ref[...]

    def step(s, hs):
        tiles = (s, _rev_index(s, n_ctx // 8, nch // 8))
        new = []
        for g in range(S5GB):
            for d in range(2):
                h = hs[2 * g + d]
                r0 = pl.multiple_of(tiles[d] * 8, 8)
                lanes = slice(128 * d, 128 * (d + 1))
                z = z_ref[g, pl.ds(r0, 8), lanes]
                entering = [None] * 8
                for j in (range(8) if d == 0 else range(7, -1, -1)):
                    entering[j] = h
                    h = a1[g, d:d + 1] * h + a2[g, d:d + 1] * pltpu.roll(h, S5P, 1) + z[j:j + 1]
                hp_ref[g, pl.ds(r0, 8), lanes] = jnp.concatenate(entering, axis=0)
                new.append(h)
        return tuple(new)

    lax.fori_loop(0, nch // 8, step, tuple(jnp.zeros((1, 128), F32) for _ in range(2 * S5GB)))
    for g in range(S5GB):
        lhs = jnp.concatenate([u_ref[0, g], hp_ref[g]], axis=1)
        y_ref[0, g] = _dot_pieces(_pieces(lhs, 2), [wy_ref[0, g], wy_ref[1, g]])


def _s5_scan(u4, wz, wy, a1, a2, n_ctx):
    b, g, nch, w = u4.shape
    return pl.pallas_call(
        functools.partial(_s5_kernel, n_ctx=n_ctx),
        grid=(b, g // S5GB),
        in_specs=[pl.BlockSpec((1, S5GB, nch, w), lambda bi, gi: (bi, gi, 0, 0)),
                  pl.BlockSpec((2, S5GB, w, w), lambda bi, gi: (0, gi, 0, 0)),
                  pl.BlockSpec((2, S5GB, 2 * w, w), lambda bi, gi: (0, gi, 0, 0)),
                  pl.BlockSpec((S5GB, 2, 128), lambda bi, gi: (gi, 0, 0)),
                  pl.BlockSpec((S5GB, 2, 128), lambda bi, gi: (gi, 0, 0))],
        out_specs=pl.BlockSpec((1, S5GB, nch, w), lambda bi, gi: (bi, gi, 0, 0)),
        out_shape=jax.ShapeDtypeStruct(u4.shape, F32),
        scratch_shapes=[pltpu.VMEM((S5GB, nch, w), F32), pltpu.VMEM((S5GB, nch, w), F32)],
        compiler_params=_cp("parallel", "parallel"),
        name="s5_scan",
    )(u4, wz, wy, a1, a2)


def _s5_weights(a_re, a_im, log_step, b_re, b_im, c_re, c_im):
    step = jnp.exp(log_step)[..., None]
    e_re, e_im = a_re * step, a_im * step
    def lam_pow(n):
        n = n[..., None, None, None] if n.ndim else n
        mag = jnp.exp(e_re * n)
        return mag * jnp.cos(e_im * n), mag * jnp.sin(e_im * n)
    l1r, l1i = lam_pow(jnp.asarray(1.0, F32))
    den = a_re * a_re + a_im * a_im
    fr = ((l1r - 1.0) * a_re + l1i * a_im) / den
    fi = (l1i * a_re - (l1r - 1.0) * a_im) / den
    bbr = fr[..., None] * b_re - fi[..., None] * b_im
    bbi = fr[..., None] * b_im + fi[..., None] * b_re
    j = jnp.arange(S5C, dtype=F32)
    es = functools.partial(jnp.einsum, precision=HI)

    def build(d):
        cr, ci = c_re[d], c_im[d]
        br, bi = bbr[d], bbi[d]
        sel = lambda x: x[:, d] if x.ndim == 4 else x
        pr, pi = lam_pow(jnp.arange(S5C + 1, dtype=F32))
        pr, pi = pr[:, d], pi[:, d]
        cl_r = cr[None] * pr[:, :, None, :] - ci[None] * pi[:, :, None, :]
        cl_i = cr[None] * pi[:, :, None, :] + ci[None] * pr[:, :, None, :]
        kk = es('ngop,gpi->ngoi', cl_r[:S5C], br) - es('ngop,gpi->ngoi', cl_i[:S5C], bi)
        ji, jo = jnp.arange(S5C)[:, None], jnp.arange(S5C)[None, :]
        lag = (jo - ji) if d == 0 else (ji - jo)
        kt = kk[jnp.clip(lag, 0, S5C - 1)]
        kt = jnp.where((lag >= 0)[:, :, None, None, None], kt, 0.0)
        toep = kt.transpose(2, 0, 4, 1, 3).reshape(S5G, S5C * 16, S5C * 16)
        m_idx = (jnp.arange(S5C) + 1) if d == 0 else (S5C - jnp.arange(S5C))
        wo_r = cl_r[m_idx].transpose(1, 3, 0, 2).reshape(S5G, S5P, S5C * 16)
        wo_i = -cl_i[m_idx].transpose(1, 3, 0, 2).reshape(S5G, S5P, S5C * 16)
        wout = jnp.concatenate([wo_r, wo_i], axis=1)
        e_idx = (S5C - 1 - jnp.arange(S5C)) if d == 0 else jnp.arange(S5C)
        lr, li = pr[e_idx], pi[e_idx]
        wi_r = (lr[..., None] * br[None] - li[..., None] * bi[None])
        wi_i = (lr[..., None] * bi[None] + li[..., None] * br[None])
        win = jnp.concatenate([wi_r.transpose(1, 0, 3, 2).reshape(S5G, S5C * 16, S5P),
                               wi_i.transpose(1, 0, 3, 2).reshape(S5G, S5C * 16, S5P)], axis=2)
        ar, ai = pr[S5C], pi[S5C]
        a1 = jnp.concatenate([ar, ar], axis=1)
        a2 = jnp.concatenate([-ai, ai], axis=1)
        return toep, wout, win, a1, a2

    tf, of, wf, a1f, a2f = build(0)
    tb, ob, wb, a1b, a2b = build(1)
    wz = jnp.concatenate([wf, wb], axis=2)
    wy = jnp.concatenate([tf + tb, of, ob], axis=1)
    split = lambda w: jnp.stack([w.astype(BF16), (w - w.astype(BF16).astype(F32)).astype(BF16)])
    return split(wz), split(wy), jnp.stack([a1f, a1b], axis=1), jnp.stack([a2f, a2b], axis=1)


def _gelu_tanh(x):
    return 0.5 * x * (1.0 + jnp.tanh(math.sqrt(2.0 / math.pi) * (x + 0.044715 * x * x * x)))


def _outproj_kernel(x_ref, mod_ref, gf_ref, gb_ref, gz_ref, ys_ref, us_ref, hi_ref, hf_ref, hb_ref, hg_ref,
                    rf_ref, rb_ref, rg_ref, vec_ref, glu_ref, ind_ref, w_ref, ln_ref, o_ref, *, alpha):
    ind = ind_ref[...]
    vec = vec_ref[...]

    def head_rms(o):
        return o * lax.rsqrt(_sum01_r(o * o, ind, 2) * (1.0 / HEAD) + RMS_EPS)

    m_gdn = head_rms(gf_ref[0] + gb_ref[0]) * vec[0:1] * _silu(gz_ref[0])
    u = us_ref[0]
    ys = _gelu_tanh(ys_ref[0] + vec[2:3] * u)
    m_s5 = ys * _sigmoid(_dot(ys, glu_ref[...]) + vec[3:4])
    m_hg = head_rms(hi_ref[0] + hf_ref[0] + hb_ref[0]) * vec[1:2] * _silu(hg_ref[0])
    m_rt = head_rms(rf_ref[0] + rb_ref[0]) * _silu(rg_ref[0])
    acc = None
    for k, mk in enumerate((m_gdn, m_s5, m_hg, m_rt)):
        part = _dot(mk.astype(BF16), w_ref[k * WM:(k + 1) * WM, :])
        acc = part if acc is None else acc + part
    m = mod_ref[0, 0]
    ln = ln_ref[...]
    o_ref[0] = _layer_norm(alpha * x_ref[0] + m[2:3] * acc, ln[0:1], ln[1:2])


def _outproj(xs, mod, g_of, g_ob, pg, ys, ps, h_oi, h_of, h_ob, ph, r_of, r_ob, pr, vec, glu_w, ind_sum, w_out, ln,
             alpha):
    b, t, d = xs.shape
    tile = lambda c: pl.BlockSpec((1, TT, WM), lambda bi, i, c=c: (bi, i, c))
    full = lambda a: pl.BlockSpec(a.shape, lambda bi, i: (0,) * a.ndim)
    return pl.pallas_call(
        functools.partial(_outproj_kernel, alpha=alpha),
        grid=(b, t // TT),
        in_specs=[pl.BlockSpec((1, TT, d), lambda bi, i: (bi, i, 0)),
                  pl.BlockSpec((1, 1, 6, d), lambda bi, i: (bi, jnp.minimum(i, 1), 0, 0)),
                  tile(0), tile(0), tile(3), tile(0), tile(0), tile(0), tile(0), tile(0), tile(4),
                  tile(0), tile(0), tile(3),
                  full(vec), full(glu_w), full(ind_sum), full(w_out), full(ln)],
        out_specs=pl.BlockSpec((1, TT, d), lambda bi, i: (bi, i, 0)),
        out_shape=jax.ShapeDtypeStruct(xs.shape, F32),
        compiler_params=_cp("parallel", "parallel"),
        name="outproj",
    )(xs, mod, g_of, g_ob, pg, ys, ps, h_oi, h_of, h_ob, ph, r_of, r_ob, pr, vec, glu_w, ind_sum, w_out, ln)


def _ffn_kernel(x_ref, mod_ref, w1_ref, w3_ref, w2_ref, ln_ref, o_ref, *, alpha):
    m = mod_ref[0, 0]
    x = x_ref[0]
    h = (x * (1.0 + m[4:5]) + m[3:4]).astype(BF16)
    act = (_silu(_dot(h, w1_ref[...])) * _dot(h, w3_ref[...])).astype(BF16)
    y = _dot(act, w2_ref[...])
    ln = ln_ref[...]
    o_ref[0] = _layer_norm(alpha * x + m[5:6] * y, ln[0:1], ln[1:2])


def _ffn(xs, mod, w1, w3, w2, ln, alpha):
    b, t, d = xs.shape
    full = lambda a: pl.BlockSpec(a.shape, lambda bi, i: (0,) * a.ndim)
    return pl.pallas_call(
        functools.partial(_ffn_kernel, alpha=alpha),
        grid=(b, t // TT),
        in_specs=[pl.BlockSpec((1, TT, d), lambda bi, i: (bi, i, 0)),
                  pl.BlockSpec((1, 1, 6, d), lambda bi, i: (bi, jnp.minimum(i, 1), 0, 0)),
                  full(w1), full(w3), full(w2), full(ln)],
        out_specs=pl.BlockSpec((1, TT, d), lambda bi, i: (bi, i, 0)),
        out_shape=jax.ShapeDtypeStruct(xs.shape, F32),
        compiler_params=_cp("parallel", "parallel"),
        name="ffn_dense",
    )(xs, mod, w1, w3, w2, ln)


def _router_kernel(x_ref, mod_ref, r_ref, h_ref, rt_ref):
    m = mod_ref[0, 0]
    h = x_ref[0] * (1.0 + m[4:5]) + m[3:4]
    h_ref[0] = h
    lane = lax.broadcasted_iota(jnp.int32, (TT, 128), 1)
    logits = jnp.where(lane < N_EXP, _dot(h, r_ref[...], HI), -jnp.inf)
    m1 = jnp.max(logits, axis=-1, keepdims=True)
    i1 = jnp.min(jnp.where(logits == m1, lane, 128), axis=-1, keepdims=True)
    rest = jnp.where(lane == i1, -jnp.inf, logits)
    m2 = jnp.max(rest, axis=-1, keepdims=True)
    i2 = jnp.min(jnp.where(rest == m2, lane, 128), axis=-1, keepdims=True)
    e = jnp.exp(m2 - m1)
    g1 = 1.0 / (1.0 + e)
    g2 = e / (1.0 + e)
    rt_ref[0] = jnp.where(lane == 0, i1.astype(F32),
                          jnp.where(lane == 1, i2.astype(F32),
                                    jnp.where(lane == 2, g1, jnp.where(lane == 3, g2, 0.0))))


def _router(xs, mod, router_pad):
    b, t, d = xs.shape
    return pl.pallas_call(
        _router_kernel,
        grid=(b, t // TT),
        in_specs=[pl.BlockSpec((1, TT, d), lambda bi, i: (bi, i, 0)),
                  pl.BlockSpec((1, 1, 6, d), lambda bi, i: (bi, jnp.minimum(i, 1), 0, 0)),
                  pl.BlockSpec(router_pad.shape, lambda bi, i: (0, 0))],
        out_specs=[pl.BlockSpec((1, TT, d), lambda bi, i: (bi, i, 0)),
                   pl.BlockSpec((1, TT, 128), lambda bi, i: (bi, i, 0))],
        out_shape=[jax.ShapeDtypeStruct(xs.shape, F32), jax.ShapeDtypeStruct((b, t, 128), F32)],
        compiler_params=_cp("parallel", "parallel"),
        name="moe_router",
    )(xs, mod, router_pad)


def _row_copy(src_hbm, row, dst_ref, slot, sem):
    return pltpu.make_async_copy(src_hbm.at[pl.ds(row, 1), :], dst_ref.at[pl.ds(slot, 1), :], sem)


def _gather_kernel(idx_ref, src_hbm, o_ref, sem, *, rows):
    base = pl.program_id(0) * rows

    def start(r, c):
        _row_copy(src_hbm, idx_ref[base + r], o_ref, r, sem).start()
        return c

    def wait(r, c):
        _row_copy(src_hbm, 0, o_ref, r, sem).wait()
        return c

    lax.fori_loop(0, rows, start, 0, unroll=8)
    lax.fori_loop(0, rows, wait, 0, unroll=8)


def _gather_rows(src, idx, rows):
    n_out = idx.shape[0]
    d = src.shape[1]
    return pl.pallas_call(
        functools.partial(_gather_kernel, rows=rows),
        grid_spec=pltpu.PrefetchScalarGridSpec(
            num_scalar_prefetch=1,
            grid=(n_out // rows,),
            in_specs=[pl.BlockSpec(memory_space=pl.ANY)],
            out_specs=pl.BlockSpec((rows, d), lambda j, idx_ref: (j, 0)),
            scratch_shapes=[pltpu.SemaphoreType.DMA(())]),
        out_shape=jax.ShapeDtypeStruct((n_out, d), src.dtype),
        compiler_params=_cp("arbitrary"),
        name="moe_gather",
    )(idx, src)


def _experts_kernel(be_ref, x_ref, w1_ref, w3_ref, w2_ref, y_ref):
    f = pl.program_id(1)
    x = x_ref[...].astype(BF16)
    act = (_silu(_dot(x, w1_ref[0])) * _dot(x, w3_ref[0])).astype(BF16)
    y = _dot(act, w2_ref[0])

    @pl.when(f == 0)
    def _():
        y_ref[...] = y

    @pl.when(f != 0)
    def _():
        y_ref[...] += y


def _experts(xs_sorted, block_expert, w1, w3, w2):
    n_rows, d = xs_sorted.shape
    ff = w1.shape[2]
    nf = 2
    tf = ff // nf
    return pl.pallas_call(
        _experts_kernel,
        grid_spec=pltpu.PrefetchScalarGridSpec(
            num_scalar_prefetch=1,
            grid=(n_rows // MOE_RB, nf),
            in_specs=[pl.BlockSpec((MOE_RB, d), lambda j, f, be: (j, 0)),
                      pl.BlockSpec((1, d, tf), lambda j, f, be: (be[j], 0, f)),
                      pl.BlockSpec((1, d, tf), lambda j, f, be: (be[j], 0, f)),
                      pl.BlockSpec((1, tf, d), lambda j, f, be: (be[j], f, 0))],
            out_specs=pl.BlockSpec((MOE_RB, d), lambda j, f, be: (j, 0))),
        out_shape=jax.ShapeDtypeStruct((n_rows, d), F32),
        compiler_params=_cp("parallel", "arbitrary"),
        name="moe_experts",
    )(block_expert, xs_sorted, w1, w3, w2)


def _combine_kernel(dest_ref, x_ref, mod_ref, rt_ref, ln_ref, y_hbm, o_ref, buf, sem, *, alpha, tiles_per_batch):
    tile = pl.program_id(0) * tiles_per_batch + pl.program_id(1)
    base = tile * (2 * TT)

    def start(r, c):
        _row_copy(y_hbm, dest_ref[base + r], buf, r, sem).start()
        return c

    def wait(r, c):
        _row_copy(y_hbm, 0, buf, r, sem).wait()
        return c

    lax.fori_loop(0, 2 * TT, start, 0, unroll=8)
    lax.fori_loop(0, 2 * TT, wait, 0, unroll=8)
    rt = rt_ref[0]
    y = rt[:, 2:3] * buf[0:TT, :] + rt[:, 3:4] * buf[TT:2 * TT, :]
    m = mod_ref[0, 0]
    ln = ln_ref[...]
    o_ref[0] = _layer_norm(alpha * x_ref[0] + m[5:6] * y, ln[0:1], ln[1:2])


def _combine(xs, mod, rt, ln, ys_sorted, dest_tiles, alpha):
    b, t, d = xs.shape
    return pl.pallas_call(
        functools.partial(_combine_kernel, alpha=alpha, tiles_per_batch=t // TT),
        grid_spec=pltpu.PrefetchScalarGridSpec(
            num_scalar_prefetch=1,
            grid=(b, t // TT),
            in_specs=[pl.BlockSpec((1, TT, d), lambda bi, i, dr: (bi, i, 0)),
                      pl.BlockSpec((1, 1, 6, d), lambda bi, i, dr: (bi, jnp.minimum(i, 1), 0, 0)),
                      pl.BlockSpec((1, TT, 128), lambda bi, i, dr: (bi, i, 0)),
                      pl.BlockSpec(ln.shape, lambda bi, i, dr: (0, 0)),
                      pl.BlockSpec(memory_space=pl.ANY)],
            out_specs=pl.BlockSpec((1, TT, d), lambda bi, i, dr: (bi, i, 0)),
            scratch_shapes=[pltpu.VMEM((2 * TT, d), F32), pltpu.SemaphoreType.DMA(())]),
        out_shape=jax.ShapeDtypeStruct(xs.shape, F32),
        compiler_params=_cp("arbitrary", "arbitrary"),
        name="moe_combine",
    )(dest_tiles, xs, mod, rt, ln, ys_sorted)


def _moe(xs, mod, router_pad, w1, w3, w2, ln, alpha):
    b, t, d = xs.shape
    n_tok = b * t
    h, rt = _router(xs, mod, router_pad)
    e_idx = rt[..., 0:2].astype(jnp.int32).reshape(n_tok * 2)
    onehot = (e_idx[:, None] == jnp.arange(N_EXP, dtype=jnp.int32)[None, :]).astype(jnp.int32)
    csum = jnp.cumsum(onehot, axis=0)
    counts = csum[-1]
    rank = jnp.take_along_axis(csum, e_idx[:, None], axis=1)[:, 0] - 1
    padded = (counts + MOE_RB - 1) // MOE_RB * MOE_RB
    pad_end = jnp.cumsum(padded)
    pad_start = pad_end - padded
    dest = pad_start[e_idx] + rank
    n_blocks = -(-(n_tok * 2) // MOE_RB) + N_EXP
    n_rows = n_blocks * MOE_RB
    row_tok = jnp.zeros((n_rows,), jnp.int32).at[dest].set(jnp.arange(n_tok * 2, dtype=jnp.int32) // 2)
    block_expert = jnp.minimum(
        jnp.sum(jnp.arange(n_blocks, dtype=jnp.int32)[:, None] * MOE_RB >= pad_end[None, :], axis=1),
        N_EXP - 1).astype(jnp.int32)
    xs_sorted = _gather_rows(h.reshape(n_tok, d), row_tok, MOE_RB)
    ys_sorted = _experts(xs_sorted, block_expert, w1, w3, w2)
    dest_tiles = dest.reshape(n_tok // TT, TT, 2).transpose(0, 2, 1).reshape(n_tok * 2)
    return _combine(xs, mod, rt, ln, ys_sorted, dest_tiles, alpha)


def _ret_constants(decay_param):
    log_gamma = -jnp.exp(decay_param)
    idx = jnp.arange(TT, dtype=F32)
    diff = idx[:, None] - idx[None, :]
    lg = log_gamma[:, :, None, None]
    dec_f = jnp.exp(jnp.where(diff >= 0, diff * lg[0], -jnp.inf))
    dec_b = jnp.exp(jnp.where(diff <= 0, -diff * lg[1], -jnp.inf))
    dec = jnp.stack([dec_f, dec_b])
    rep = lambda a: jnp.repeat(a, HEAD, axis=-1)
    qdec = jnp.stack([rep(jnp.exp((idx[:, None] + 1.0) * log_gamma[0][None, :])),
                      rep(jnp.exp((TT - idx[:, None]) * log_gamma[1][None, :]))])
    kdec = jnp.stack([rep(jnp.exp((TT - 1.0 - idx[:, None]) * log_gamma[0][None, :])),
                      rep(jnp.exp(idx[:, None] * log_gamma[1][None, :]))])
    cdec = jnp.exp(TT * log_gamma)
    return dec, qdec, kdec, cdec


def _rotary_tables(n_ctx_tok, n_lat, grid_w):
    rows = jnp.repeat(jnp.arange(n_lat // grid_w, dtype=F32), grid_w)
    cols = jnp.tile(jnp.arange(grid_w, dtype=F32), n_lat // grid_w)
    quarter = HEAD // 4
    inv_freq = ROPE_BASE ** (-jnp.arange(quarter, dtype=F32) / quarter)
    ang = jnp.concatenate([rows[:, None] * inv_freq, cols[:, None] * inv_freq], axis=-1)
    cos, sin = jnp.cos(ang), jnp.sin(ang)
    cos_h = jnp.concatenate([cos, cos], axis=-1)
    sin_h = jnp.concatenate([-sin, sin], axis=-1)
    cos_t = jnp.concatenate([jnp.ones((n_ctx_tok, HEAD), F32), cos_h], axis=0)
    sin_t = jnp.concatenate([jnp.zeros((n_ctx_tok, HEAD), F32), sin_h], axis=0)
    return jnp.tile(cos_t, (1, NH)), jnp.tile(sin_t, (1, NH))


def _swap_halves_cols(w):
    dm = w.shape[0]
    return w.reshape(dm, NH, 2, HEAD // 2)[:, :, ::-1, :].reshape(dm, WM)


def _prep_w_in(w):
    dm = w.shape[0]
    r0 = 3344 - 768
    rq, rk = w[:, r0:r0 + WM], w[:, r0 + WM:r0 + 2 * WM]
    return jnp.concatenate([w[:, :1040], jnp.zeros((dm, W_AB - 16), w.dtype), w[:, 1040:],
                            _swap_halves_cols(rq), _swap_halves_cols(rk)], axis=1).astype(BF16)


def kernel(x, c, ctx, c_ctx, ada_w, ada_b, w_in, w_out, ln_g, ln_b, gdn_conv_w, gdn_a_log, gdn_dt_bias, gdn_norm_w, s5_a_re, s5_a_im, s5_log_step, s5_b_re, s5_b_im, s5_c_re, s5_c_im, s5_d, s5_glu_w, s5_glu_b, hgrn_lower_bounds, hgrn_norm_w, ret_decay, ffn_w1, ffn_w3, ffn_w2, moe_router, moe_w1, moe_w3, moe_w2):
    bsz, n_lat, d = x.shape
    n_ctx_tok = ctx.shape[1]
    depth = ada_w.shape[0]
    grid_w = 64
    assert n_ctx_tok == TT and n_lat % TT == 0 and bsz <= 7
    t = n_ctx_tok + n_lat
    alpha = (2.0 * depth) ** 0.25

    xs = jnp.concatenate([ctx, x], axis=1)
    cs = jnp.concatenate([c, c_ctx[None, :], jnp.zeros((8 - bsz - 1, d), F32)], axis=0)
    mod_all = _ada_mod(cs, ada_w, ada_b)
    lat_mod = mod_all[:, :bsz].reshape(depth, bsz, 6, d)
    ctx_mod = jnp.broadcast_to(mod_all[:, bsz].reshape(depth, 1, 6, d), (depth, bsz, 6, d))
    mod_tab = jnp.stack([ctx_mod, lat_mod], axis=2)

    lb_all = jnp.cumsum(jax.nn.softmax(hgrn_lower_bounds.astype(F32), axis=0), axis=0)
    lb_all = lb_all - lb_all[0]
    ind_sum = jnp.asarray(np.kron(np.eye(NH), np.ones((HEAD, HEAD))), BF16)
    chunks_eye = np.eye(TT // CH)
    tri_bd = jnp.asarray(np.kron(chunks_eye, np.tril(np.ones((CH, CH)))), BF16)
    ones_bd = jnp.asarray(np.kron(chunks_eye, np.ones((CH, CH))), BF16)
    ones_c = jnp.ones((CH, HEAD), BF16)
    mall_np, masks_np = _hgrn_constants()
    mall, masks = jnp.asarray(mall_np, BF16), jnp.asarray(masks_np)
    cos_t, sin_t = _rotary_tables(n_ctx_tok, n_lat, grid_w)
    nch = t // S5C

    for layer in range(depth):
        mod = mod_tab[layer]
        pg, pab, ps, ph, pr = _inproj(xs, mod, _prep_w_in(w_in[layer]))

        pabt = pab[..., :16].transpose(0, 2, 1)
        g_loc = _gdn_local(pg, pab, pabt, gdn_conv_w[layer], ind_sum, -jnp.exp(gdn_a_log[layer]),
                           gdn_dt_bias[layer], tri_bd, ones_bd)
        g_of, g_ob = _gdn_scan(g_loc, n_ctx_tok // CH)

        wz, wy, a1, a2 = _s5_weights(s5_a_re[layer], s5_a_im[layer], s5_log_step[layer], s5_b_re[layer],
                                     s5_b_im[layer], s5_c_re[layer], s5_c_im[layer])
        u4 = ps.reshape(bsz, nch, S5C, S5G, 16).transpose(0, 3, 1, 2, 4).reshape(bsz, S5G, nch, S5C * 16)
        y4 = _s5_scan(u4, wz, wy, a1, a2, n_ctx_tok // S5C)
        ys = y4.reshape(bsz, S5G, nch, S5C, 16).transpose(0, 2, 3, 1, 4).reshape(bsz, t, WM)

        lb = lb_all[layer][None, :]
        lbp = jnp.concatenate([jnp.log(lb), jnp.log1p(-lb), 1.0 - lb, jnp.zeros((5, WM), F32)], axis=0)
        h_oi, h_qgf, h_qgb, h_kvf, h_kvb, h_ecf, h_ecb = _hgrn_local(ph, lbp, mall, masks, ind_sum, ones_c)
        h_of, h_ob = _hgrn_scan(h_qgf, h_qgb, h_kvf, h_kvb, h_ecf, h_ecb, n_ctx_tok // CH)

        dec, qdec, kdec, cdec = _ret_constants(ret_decay[layer])
        r_of, r_ob = _ret_scan(pr, cos_t, sin_t, dec, qdec, kdec, cdec)

        vec = jnp.concatenate([jnp.tile(gdn_norm_w[layer], NH)[None], jnp.tile(hgrn_norm_w[layer], NH)[None],
                               s5_d[layer][None], s5_glu_b[layer][None], jnp.zeros((4, WM), F32)], axis=0)
        xs = _outproj(xs, mod, g_of, g_ob, pg, ys, ps, h_oi, h_of, h_ob, ph, r_of, r_ob, pr, vec, s5_glu_w[layer],
                      ind_sum, w_out[layer].astype(BF16), jnp.stack([ln_g[layer, 0], ln_b[layer, 0]]), alpha)

        j = layer // 2
        ln2 = jnp.stack([ln_g[layer, 1], ln_b[layer, 1]])
        if layer % 2 == 0:
            xs = _ffn(xs, mod, ffn_w1[j].astype(BF16), ffn_w3[j].astype(BF16), ffn_w2[j].astype(BF16), ln2, alpha)
        else:
            router_pad = jnp.concatenate([moe_router[j], jnp.zeros((d, 128 - N_EXP), F32)], axis=1)
            xs = _moe(xs, mod, router_pad, moe_w1[j].astype(BF16), moe_w3[j].astype(BF16),
                      moe_w2[j].astype(BF16), ln2, alpha)
    return xs[:, n_ctx_tok:, :]
```

```python
import functools
import math

import numpy as np
import jax
import jax.numpy as jnp
from jax import lax
from jax.experimental import pallas as pl
from jax.experimental.pallas import tpu as pltpu

F32 = jnp.float32
BF16 = jnp.bfloat16
HI = lax.Precision.HIGHEST

HEAD = 64
NH = 4
WM = NH * HEAD
TT = 256
CH = 64
S5C = 16
S5G = 16
S5P = 64
S5GB = 8
N_EXP = 8
MOE_RB = 512
LN_EPS = 1e-5
RMS_EPS = 1e-6
ROPE_BASE = 10000.0
VMEM_LIMIT = 56 * 1024 * 1024


def _cp(*sem):
    return pltpu.CompilerParams(dimension_semantics=sem, vmem_limit_bytes=VMEM_LIMIT)


def _sigmoid(x):
    return 1.0 / (1.0 + jnp.exp(-x))


def _silu(x):
    return x * _sigmoid(x)


def _softplus(x):
    return jnp.maximum(x, 0.0) + jnp.log1p(jnp.exp(-jnp.abs(x)))


def _dot(a, b, precision=None):
    return jnp.dot(a, b, preferred_element_type=F32, precision=precision)


def _dot_nt(a, b, precision=None):
    return lax.dot_general(a, b, (((1,), (1,)), ((), ())), preferred_element_type=F32, precision=precision)


def _dot_tn(a, b, precision=None):
    return lax.dot_general(a, b, (((0,), (0,)), ((), ())), preferred_element_type=F32, precision=precision)


def _layer_norm(y, g, b):
    mu = jnp.mean(y, axis=-1, keepdims=True)
    yc = y - mu
    var = jnp.mean(yc * yc, axis=-1, keepdims=True)
    return yc * lax.rsqrt(var + LN_EPS) * g + b


def _rev_index(n, n_ctx, n_all):
    return jnp.where(n < n_ctx, n_ctx - 1 - n, n_all + n_ctx - 1 - n)


def _ada_kernel(c_ref, w_ref, b_ref, o_ref):
    o_ref[0] = _dot(_silu(c_ref[...]), w_ref[0], HI) + b_ref[0]


def _ada_mod(cs, ada_w, ada_b):
    depth, d, d6 = ada_w.shape
    tn = 1024
    return pl.pallas_call(
        _ada_kernel,
        grid=(depth, d6 // tn),
        in_specs=[pl.BlockSpec((8, d), lambda l, j: (0, 0)),
                  pl.BlockSpec((1, d, tn), lambda l, j: (l, 0, j)),
                  pl.BlockSpec((1, 1, tn), lambda l, j: (l, 0, j))],
        out_specs=pl.BlockSpec((1, 8, tn), lambda l, j: (l, 0, j)),
        out_shape=jax.ShapeDtypeStruct((depth, 8, d6), F32),
        compiler_params=_cp("parallel", "parallel"),
        name="ada_mod",
    )(cs, ada_w, ada_b.reshape(depth, 1, d6))


W_PG, W_AB, W_S5, W_PH, W_PR = 1024, 128, 256, 1280, 1536
P_OFF = np.cumsum([0, W_PG, W_AB, W_S5, W_PH, W_PR])


def _inproj_kernel(x_ref, mod_ref, w_ref, pg_ref, pab_ref, ps_ref, ph_ref, pr_ref):
    m = mod_ref[0, 0]
    h = (x_ref[0] * (1.0 + m[1:2]) + m[0:1]).astype(BF16)
    for k, o_ref in enumerate((pg_ref, pab_ref, ps_ref, ph_ref, pr_ref)):
        o_ref[0] = _dot(h, w_ref[:, P_OFF[k]:P_OFF[k + 1]])


def _inproj(xs, mod, w):
    b, t, d = xs.shape
    widths = (W_PG, W_AB, W_S5, W_PH, W_PR)
    return pl.pallas_call(
        _inproj_kernel,
        grid=(b, t // TT),
        in_specs=[pl.BlockSpec((1, TT, d), lambda bi, i: (bi, i, 0)),
                  pl.BlockSpec((1, 1, 6, d), lambda bi, i: (bi, jnp.minimum(i, 1), 0, 0)),
                  pl.BlockSpec(w.shape, lambda bi, i: (0, 0))],
        out_specs=[pl.BlockSpec((1, TT, wd), lambda bi, i: (bi, i, 0)) for wd in widths],
        out_shape=[jax.ShapeDtypeStruct((b, t, wd), F32) for wd in widths],
        compiler_params=_cp("parallel", "parallel"),
        name="inproj",
    )(xs, mod, w)


def _pieces(x, n):
    out, r = [], x
    for i in range(n):
        p = r.astype(BF16)
        out.append(p)
        if i + 1 < n:
            r = r - p.astype(F32)
    return out


def _dot_pieces(a_parts, b_parts, dot=_dot):
    n = max(len(a_parts), len(b_parts))
    acc = None
    for i, ap in enumerate(a_parts):
        for j, bp in enumerate(b_parts):
            if i + j < n:
                t = dot(ap, bp)
                acc = t if acc is None else acc + t
    return acc


def _sum01_l(m01, x, n):
    return _dot_pieces([m01], _pieces(x, n))


def _sum01_r(x, m01, n, dot=_dot):
    return _dot_pieces(_pieces(x, n), [m01], dot)


GDN_PIECES = 2
GDN_BASE = 16


def _unit_tri_inverse(a, eye, masks):
    n = GDN_PIECES
    d = jnp.where(masks[0], a, 0.0)
    t = jnp.where(eye, 1.0, 0.0) - d
    p = _pieces(d, n)
    size = 2
    while size < GDN_BASE:
        p = _pieces(_dot_pieces(p, p), n)
        t = t + _dot_pieces(_pieces(t, n), p)
        size *= 2
    for off_mask in masks[1:]:
        tp = _pieces(t, n)
        t = t - _dot_pieces(tp, _pieces(_dot_pieces(_pieces(jnp.where(off_mask, a, 0.0), n), tp), n))
    return t


def _gdn_local_kernel(na_ref, dtb_ref, p_ref, pv_ref, nx_ref, cw_ref, ind_ref, ab_ref, abt_ref, tri_ref, trit_ref,
                      ones_ref, uf_ref, ub_ref, wf_ref, wb_ref, qkf_ref, qkb_ref, qgf_ref, qgb_ref, kdf_ref, kdb_ref,
                      eg_ref):
    i = pl.program_id(1)
    nt = pl.num_programs(1)
    x = p_ref[0]
    prev = jnp.where(i >= 2, pv_ref[0][7:8], 0.0)
    nxt = jnp.where((i >= 1) & (i < nt - 1), nx_ref[0][0:1], 0.0)
    row1 = lax.broadcasted_iota(jnp.int32, (TT, 1), 0)
    xm = jnp.where(row1 == 0, prev, pltpu.roll(x, 1, 0))
    xp = jnp.where(row1 == TT - 1, nxt, pltpu.roll(x, TT - 1, 0))
    cw = cw_ref[...]
    y = _silu(cw[0:1] * xm + cw[1:2] * x + cw[2:3] * xp)
    q, k, v = y[:, :WM], y[:, WM:2 * WM], y[:, 2 * WM:]
    ind = ind_ref[...]
    q = q * lax.rsqrt(_sum01_r(q * q, ind, 2) + RMS_EPS) * HEAD ** -0.5
    k = k * lax.rsqrt(_sum01_r(k * k, ind, 2) + RMS_EPS)

    ab = ab_ref[0]
    abt = abt_ref[0]
    ones_bd = ones_ref[...]
    row = lax.broadcasted_iota(jnp.int32, (TT, TT), 0)
    col = lax.broadcasted_iota(jnp.int32, (TT, TT), 1)
    same = (row // CH) == (col // CH)
    eye = row == col
    in_block = lambda n: (row // n) == (col // n)
    inv_masks, n = [in_block(GDN_BASE)], GDN_BASE
    while n < CH:
        inv_masks.append(in_block(2 * n) & jnp.logical_not(in_block(n)))
        n *= 2
    out_refs =((uf_ref, wf_ref, qkf_ref, qgf_ref, kdf_ref), (ub_ref, wb_ref, qkb_ref, qgb_ref, kdb_ref))
    e_last = []
    for d in range(2):
        incl = same & ((row >= col) if d == 0 else (row <= col))
        strict = same & ((row > col) if d == 0 else (row < col))
        tri_c = tri_ref[...] if d == 0 else trit_ref[...]
        a_col, b_col = ab[:, 4 * d:4 * d + 4], ab[:, 8 + 4 * d:12 + 4 * d]
        a_row = abt[4 * d:4 * d + 4, :]
        g_col = jnp.concatenate([na_ref[d, h] * _softplus(a_col[:, h:h + 1] + dtb_ref[d, h]) for h in range(NH)], axis=1)
        g_row = jnp.concatenate([na_ref[d, h] * _softplus(a_row[h:h + 1, :] + dtb_ref[d, h]) for h in range(NH)], axis=0)
        gc_col = _sum01_l(tri_c, g_col, 3)
        gc_row = _sum01_r(g_row, tri_c, 3, _dot_nt)
        gl_col = _sum01_l(ones_bd, g_col, 3)
        e_last.append(jnp.exp(gl_col))
        us, ws, qks, qgs, kds = [], [], [], [], []
        for h in range(NH):
            sl = slice(h * HEAD, (h + 1) * HEAD)
            qh, kh, vh = q[:, sl], k[:, sl], v[:, sl]
            beta = _sigmoid(b_col[:, h:h + 1])
            gcc = gc_col[:, h:h + 1]
            decay = jnp.exp(jnp.where(incl, gcc - gc_row[h:h + 1, :], -jnp.inf))
            kb = kh * beta
            a = _dot_nt(kb, kh) * jnp.where(strict, decay, 0.0)
            xx = _dot_pieces(_pieces(_unit_tri_inverse(a, eye, inv_masks), GDN_PIECES),
                             _pieces(jnp.concatenate([vh * beta, kb * jnp.exp(gcc)], axis=1), GDN_PIECES))
            us.append(xx[:, :HEAD])
            ws.append(xx[:, HEAD:])
            qk = _dot_nt(qh, kh) * decay
            qks.append(qk[:, 0:CH] + qk[:, CH:2 * CH] + qk[:, 2 * CH:3 * CH] + qk[:, 3 * CH:4 * CH])
            qgs.append(qh * jnp.exp(gcc))
            kds.append(kh * jnp.exp(gl_col[:, h:h + 1] - gcc))
        u_ref, w_ref, qk_ref, qg_ref, kd_ref = out_refs[d]
        u_ref[0] = jnp.concatenate(us, axis=1)
        w_ref[0] = jnp.concatenate(ws, axis=1).astype(BF16)
        qk_ref[0] = jnp.concatenate(qks, axis=1).astype(BF16)
        qg_ref[0] = jnp.concatenate(qgs, axis=1).astype(BF16)
        kdt = jnp.concatenate(kds, axis=1).T
        for c in range(TT // CH):
            kd_ref[0, c] = kdt[:, c * CH:(c + 1) * CH].astype(BF16)
    eg_ref[0] =jnp.concatenate(e_last + [jnp.zeros((TT, 128 - 2 * NH), F32)], axis=1)


def _gdn_local(pg, pab, pabt, conv_w, ind_sum, neg_a, dt_bias, tri_bd, ones_bd):
    b, t, _ = pg.shape
    w3 = 3 * WM
    n8 = t // 8
    smem = pl.BlockSpec(memory_space=pltpu.SMEM)
    full = lambda a: pl.BlockSpec(a.shape, lambda bi, i: (0,) * a.ndim)
    tile = pl.BlockSpec((1, TT, WM), lambda bi, i: (bi, i, 0))
    f32o = jax.ShapeDtypeStruct((b, t, WM), F32)
    b16o = jax.ShapeDtypeStruct((b, t, WM), BF16)
    return pl.pallas_call(
        _gdn_local_kernel,
        grid=(b, t // TT),
        in_specs=[smem, smem,
                  pl.BlockSpec((1, TT, w3), lambda bi, i: (bi, i, 0)),
                  pl.BlockSpec((1, 8, w3), lambda bi, i: (bi, jnp.maximum(i * (TT // 8) - 1, 0), 0)),
                  pl.BlockSpec((1, 8, w3), lambda bi, i: (bi, jnp.minimum((i + 1) * (TT // 8), n8 - 1), 0)),
                  pl.BlockSpec((3, w3), lambda bi, i: (0, 0)),
                  full(ind_sum),
                  pl.BlockSpec((1, TT, W_AB), lambda bi, i: (bi, i, 0)),
                  pl.BlockSpec((1, 16, TT), lambda bi, i: (bi, 0, i)),
                  full(tri_bd), full(tri_bd), full(ones_bd)],
        out_specs=[tile] * 8 + [pl.BlockSpec((1, TT // CH, WM, CH), lambda bi, i: (bi, i, 0, 0))] * 2
                  + [pl.BlockSpec((1, TT, 128), lambda bi, i: (bi, i, 0))],
        out_shape=[f32o, f32o] + [b16o] * 6 + [jax.ShapeDtypeStruct((b, t // CH, WM, CH), BF16)] * 2
                  + [jax.ShapeDtypeStruct((b, t, 128), F32)],
        compiler_params=_cp("parallel", "parallel"),
        name="gdn_local",
    )(neg_a, dt_bias, pg, pg, pg, conv_w, ind_sum, pab, pabt, tri_bd, tri_bd.T, ones_bd)


def _gdn_scan_kernel(uf_ref, ub_ref, wf_ref, wb_ref, qkf_ref, qkb_ref, qgf_ref, qgb_ref, kdf_ref, kdb_ref,
                     egf_ref, egb_ref, mask_ref, of_ref, ob_ref, s_ref):
    @pl.when(pl.program_id(0) == 0)
    def _():
        s_ref[...] = jnp.zeros_like(s_ref)

    dirs = ((uf_ref, wf_ref, qkf_ref, qgf_ref, kdf_ref, egf_ref, of_ref),
            (ub_ref, wb_ref, qkb_ref, qgb_ref, kdb_ref, egb_ref, ob_ref))
    mask_b = mask_ref[...]
    mask_f = mask_b.astype(F32)
    for b in range(uf_ref.shape[0]):
        for d, (u_ref, w_ref, qk_ref, qg_ref, kdt_ref, eg_ref, o_ref) in enumerate(dirs):
            s = s_ref[b, d]
            sb = s.astype(BF16)
            ws = _dot(jnp.concatenate([w_ref[b], qg_ref[b]], axis=0), sb)
            vb = (u_ref[b] - ws[:CH]).astype(BF16)
            v_bd = jnp.concatenate([vb] * NH, axis=0) * mask_b
            o_ref[b] = ws[CH:] + _dot(qk_ref[b], v_bd)
            eg = eg_ref[b][0:1, :]
            e_row = jnp.concatenate(
                [jnp.broadcast_to(eg[:, 4 * d + h:4 * d + h + 1], (1, HEAD)) for h in range(NH)], axis=1)
            s_ref[b, d] = s * e_row + _dot(kdt_ref[b, 0], vb) * mask_f


def _gdn_scan(loc, head_mask, n_ctx):
    uf, ub, wf, wb, qkf, qkb, qgf, qgb, kdf, kdb, eg = loc
    b, t, _ = uf.shape
    nc = t // CH
    fwd = lambda n: (0, n, 0)
    bwd = lambda n: (0, _rev_index(n, n_ctx, nc), 0)
    fwd4 = lambda n: (0, n, 0, 0)
    bwd4 = lambda n: (0, _rev_index(n, n_ctx, nc), 0, 0)
    blk = lambda im, w=WM: pl.BlockSpec((b, CH, w), im)
    kdt = lambda im: pl.BlockSpec((b, 1, WM, CH), im)
    return pl.pallas_call(
        _gdn_scan_kernel,
        grid=(nc,),
        in_specs=[blk(fwd), blk(bwd)] * 4 + [kdt(fwd4), kdt(bwd4), blk(fwd, 128), blk(bwd, 128),
                                              pl.BlockSpec(head_mask.shape, lambda n: (0, 0))],
        out_specs=[blk(fwd), blk(bwd)],
        out_shape=[jax.ShapeDtypeStruct((b, t, WM), F32)] * 2,
        scratch_shapes=[pltpu.VMEM((b, 2, WM, WM), F32)],
        compiler_params=_cp("arbitrary"),
        name="gdn_scan",
    )(uf, ub, wf, wb, qkf, qkb, qgf, qgb, kdf, kdb, eg, eg, head_mask)


HG_LEVELS = (32, 16, 8, 4, 2, 1)


def _hgrn_constants():
    idx = np.arange(CH)
    i, t = idx[:, None], idx[None, :]
    blocks = [(t <= i), (t > i)]
    masks = []
    for s in HG_LEVELS:
        m = (idx // (2 * s)) * 2 * s + s
        sec = (idx % (2 * s)) >= s
        mi = m[:, None]
        blocks.append(sec[:, None] & (t >= mi) & (t <= i))
        blocks.append((~sec)[:, None] & (t > i) & (t <= mi - 1))
        same = (idx[:, None] // (2 * s)) == (idx[None, :] // (2 * s))
        masks.append(same & sec[:, None] & (~sec)[None, :])
    mall_f = np.concatenate(blocks, axis=0).astype(np.float32)
    masks_f = np.stack(masks).astype(np.float32)
    nb = len(blocks)
    mall_b = mall_f.reshape(nb, CH, CH)[:, ::-1, ::-1].reshape(nb * CH, CH)
    masks_b = masks_f[:, ::-1, ::-1]
    eye = np.eye(TT // CH, dtype=np.float32)
    bd = lambda m: np.stack([np.kron(eye, m[lv]) for lv in range(len(HG_LEVELS))])
    return np.stack([mall_f, mall_b]), np.stack([bd(masks_f), bd(masks_b)])


def _hgrn_local_kernel(ph_ref, lbp_ref, mall_ref, mask_ref, ind_ref, ones_ref,
                       oi_ref, qgf_ref, qgb_ref, kvf_ref, kvb_ref, ecf_ref, ecb_ref):
    ph = ph_ref[0]
    lbp = lbp_ref[...]
    ind = ind_ref[...]
    ones = ones_ref[...]
    log_lb, log_1m_lb, one_m_lb = lbp[0:1], lbp[1:2], lbp[2:3]
    q = _silu(ph[:, :WM])
    v = ph[:, 3 * WM:4 * WM]
    ncl = TT // CH
    o_sum = None
    for d, (qg_ref, kv_ref, ec_ref) in enumerate(((qgf_ref, kvf_ref, ecf_ref), (qgb_ref, kvb_ref, ecb_ref))):
        fz = ph[:, WM * (1 + d):WM * (2 + d)]
        lsig = jnp.minimum(fz, 0.0) - jnp.log1p(jnp.exp(-jnp.abs(fz)))
        bb = log_1m_lb + lsig
        logf = jnp.maximum(log_lb, bb) + jnp.log1p(jnp.exp(-jnp.abs(log_lb - bb)))
        k = one_m_lb / (1.0 + jnp.exp(fz))
        e_c = [jnp.exp(_sum01_l(mall_ref[d], logf[c * CH:(c + 1) * CH], 2)) for c in range(ncl)]
        blk = lambda r: jnp.concatenate([e_c[c][r * CH:(r + 1) * CH] for c in range(ncl)], axis=0)
        qg_ref[0] = (q * blk(0)).astype(BF16)
        kd = k * blk(1)
        o_d = _sum01_r(q * k, ind, 2) * v
        q_lv = [q * blk(2 + 2 * lv) for lv in range(len(HG_LEVELS))]
        k_lv = [k * blk(3 + 2 * lv) for lv in range(len(HG_LEVELS))]
        outs = []
        for h in range(NH):
            sl = slice(h * HEAD, (h + 1) * HEAD)
            att = None
            for lv in range(len(HG_LEVELS)):
                term = _dot_nt(q_lv[lv][:, sl], k_lv[lv][:, sl]) * mask_ref[d, lv]
                att = term if att is None else att + term
            outs.append(_dot(att, v[:, sl]))
        o_d = o_d + jnp.concatenate(outs, axis=1)
        o_sum = o_d if o_sum is None else o_sum + o_d
        for c in range(ncl):
            rows = slice(c * CH, (c + 1) * CH)
            kv = _dot_tn(kd[rows], v[rows])
            tot = _sum01_r(logf[rows], ones, 3, _dot_tn)
            kv_ref[0, c] = jnp.concatenate([kv[h * HEAD:(h + 1) * HEAD, h * HEAD:(h + 1) * HEAD] for h in range(NH)], axis=1)
            ec_ref[0, c] = jnp.concatenate([jnp.exp(tot[h * HEAD:(h + 1) * HEAD]) for h in range(NH)], axis=1)
    oi_ref[0] = o_sum


def _hgrn_local(ph, lbp, mall, masks, ind_sum, ones_c):
    b, t, _ = ph.shape
    ncl = TT // CH
    full = lambda a: pl.BlockSpec(a.shape, lambda bi, i: (0,) * a.ndim)
    tile = pl.BlockSpec((1, TT, WM), lambda bi, i: (bi, i, 0))
    st_spec = pl.BlockSpec((1, ncl, HEAD, WM), lambda bi, i: (bi, i, 0, 0))
    st_shape = jax.ShapeDtypeStruct((b, t // CH, HEAD, WM), F32)
    return pl.pallas_call(
        _hgrn_local_kernel,
        grid=(b, t // TT),
        in_specs=[pl.BlockSpec((1, TT, W_PH), lambda bi, i: (bi, i, 0)),
                  full(lbp), full(mall), full(masks), full(ind_sum), full(ones_c)],
        out_specs=[tile, tile, tile, st_spec, st_spec, st_spec, st_spec],
        out_shape=[jax.ShapeDtypeStruct((b, t, WM), F32), jax.ShapeDtypeStruct((b, t, WM), BF16),
                   jax.ShapeDtypeStruct((b, t, WM), BF16), st_shape, st_shape, st_shape, st_shape],
        compiler_params=_cp("parallel", "parallel"),
        name="hgrn_local",
    )(ph, lbp, mall, masks, ind_sum, ones_c)


def _hgrn_scan_kernel(qgf_ref, qgb_ref, kvf_ref, kvb_ref, ecf_ref, ecb_ref, of_ref, ob_ref, s_ref):
    @pl.when(pl.program_id(0) == 0)
    def _():
        s_ref[...] = jnp.zeros_like(s_ref)

    dirs = ((qgf_ref, kvf_ref, ecf_ref, of_ref), (qgb_ref, kvb_ref, ecb_ref, ob_ref))
    for b in range(qgf_ref.shape[0]):
        for d, (qg_ref, kv_ref, ec_ref, o_ref) in enumerate(dirs):
            s = s_ref[b, d]
            sb = s.astype(BF16)
            qg = qg_ref[b]
            o_ref[b] = jnp.concatenate(
                [_dot(qg[:, h * HEAD:(h + 1) * HEAD], sb[:, h * HEAD:(h + 1) * HEAD]) for h in range(NH)], axis=1)
            s_ref[b, d] = s * ec_ref[b, 0] + kv_ref[b, 0]


def _hgrn_scan(qgf, qgb, kvf, kvb, ecf, ecb, n_ctx):
    b, t, _ = qgf.shape
    nc = t // CH
    fwd3 = lambda n: (0, n, 0)
    bwd3 = lambda n: (0, _rev_index(n, n_ctx, nc), 0)
    fwd4 = lambda n: (0, n, 0, 0)
    bwd4 = lambda n: (0, _rev_index(n, n_ctx, nc), 0, 0)
    tok = lambda im: pl.BlockSpec((b, CH, WM), im)
    st = lambda im: pl.BlockSpec((b, 1, HEAD, WM), im)
    return pl.pallas_call(
        _hgrn_scan_kernel,
        grid=(nc,),
        in_specs=[tok(fwd3), tok(bwd3), st(fwd4), st(bwd4), st(fwd4), st(bwd4)],
        out_specs=[tok(fwd3), tok(bwd3)],
        out_shape=[jax.ShapeDtypeStruct((b, t, WM), F32)] * 2,
        scratch_shapes=[pltpu.VMEM((b, 2, HEAD, WM), F32)],
        compiler_params=_cp("arbitrary"),
        name="hgrn_scan",
    )(qgf, qgb, kvf, kvb, ecf, ecb)


def _ret_direction(d, pr, cos, sin, dec_ref, qd, kd, cdec_ref, s_ref):
    q = (pr[:, :WM] * cos + pr[:, 4 * WM:5 * WM] * sin)
    k = (pr[:, WM:2 * WM] * cos + pr[:, 5 * WM:6 * WM] * sin) * HEAD ** -0.5
    v = pr[:, 2 * WM:3 * WM]
    q_in = q * qd
    k_in = k * kd
    outs = []
    for h in range(NH):
        sl = slice(h * HEAD, (h + 1) * HEAD)
        s = s_ref[d, h]
        att = _dot_nt(q[:, sl], k[:, sl]) * dec_ref[d, h]
        outs.append(_dot(att, v[:, sl]) + _dot(q_in[:, sl], s))
        s_ref[d, h] = s * cdec_ref[d, h] + _dot_tn(k_in[:, sl], v[:, sl])
    return jnp.concatenate(outs, axis=1)


def _ret_scan_kernel(cdec_ref, pf_ref, pb_ref, cf_ref, sf_ref, cb_ref, sb_ref, dec_ref, qd_ref, kd_ref,
                     of_ref, ob_ref, s_ref):
    @pl.when(pl.program_id(1) == 0)
    def _():
        s_ref[...] = jnp.zeros_like(s_ref)

    of_ref[0] = _ret_direction(0, pf_ref[0], cf_ref[...], sf_ref[...], dec_ref, qd_ref[0], kd_ref[0], cdec_ref, s_ref)
    ob_ref[0] = _ret_direction(1, pb_ref[0], cb_ref[...], sb_ref[...], dec_ref, qd_ref[1], kd_ref[1], cdec_ref, s_ref)


def _ret_scan(pr, cos_t, sin_t, dec, qdec, kdec, cdec):
    b, t, _ = pr.shape
    nt = t // TT
    fwd3 = lambda bi, n: (bi, n, 0)
    bwd3 = lambda bi, n: (bi, _rev_index(n, 1, nt), 0)
    fwd2 = lambda bi, n: (n, 0)
    bwd2 = lambda bi, n: (_rev_index(n, 1, nt), 0)
    return pl.pallas_call(
        _ret_scan_kernel,
        grid=(b, nt),
        in_specs=[pl.BlockSpec(memory_space=pltpu.SMEM),
                  pl.BlockSpec((1, TT, W_PR), fwd3), pl.BlockSpec((1, TT, W_PR), bwd3),
                  pl.BlockSpec((TT, WM), fwd2), pl.BlockSpec((TT, WM), fwd2),
                  pl.BlockSpec((TT, WM), bwd2), pl.BlockSpec((TT, WM), bwd2),
                  pl.BlockSpec(dec.shape, lambda bi, n: (0, 0, 0, 0)),
                  pl.BlockSpec(qdec.shape, lambda bi, n: (0, 0, 0)),
                  pl.BlockSpec(kdec.shape, lambda bi, n: (0, 0, 0))],
        out_specs=[pl.BlockSpec((1, TT, WM), fwd3), pl.BlockSpec((1, TT, WM), bwd3)],
        out_shape=[jax.ShapeDtypeStruct((b, t, WM), F32)] * 2,
        scratch_shapes=[pltpu.VMEM((2, NH, HEAD, HEAD), F32)],
        compiler_params=_cp("parallel", "arbitrary"),
        name="ret_scan",
    )(cdec, pr, pr, cos_t, sin_t, cos_t, sin_t, dec, qdec, kdec)


def _s5_kernel(u_ref, wz_ref, wy_ref, a1_ref, a2_ref, y_ref, z_ref, hp_ref, *, n_ctx):
    nch = u_ref.shape[2]
    for g in range(S5GB):
        z_ref[g] = _dot_pieces(_pieces(u_ref[0, g], 2), [wz_ref[0, g], wz_ref[1, g]])
    a1 = a1_ref[...]
    a2 = a2_ref[...]

    def step(s, hs):
        tiles = (s, _rev_index(s, n_ctx // 8, nch // 8))
        new = []
        for g in range(S5GB):
            for d in range(2):
                h, hx = hs[2 * g + d]
                r0 = pl.multiple_of(tiles[d] * 8, 8)
                lanes = slice(128 * d, 128 * (d + 1))
                z = z_ref[g, pl.ds(r0, 8), lanes]
                zx = pltpu.roll(z, S5P, 1)
                c1, c2 = a1[g, d:d + 1], a2[g, d:d + 1]
                entering = [None] * 8
                for j in (range(8) if d == 0 else range(7, -1, -1)):
                    entering[j] = h
                    h, hx = c1 * h + c2 * hx + z[j:j + 1], c1 * hx - c2 * h + zx[j:j + 1]
                hp_ref[g, pl.ds(r0, 8), lanes] = jnp.concatenate(entering, axis=0)
                new.append((h, hx))
        return tuple(new)

    zero = jnp.zeros((1, 128), F32)
    lax.fori_loop(0, nch // 8, step, tuple((zero, zero) for _ in range(2 * S5GB)))
    for g in range(S5GB):
        lhs = jnp.concatenate([u_ref[0, g], hp_ref[g]], axis=1)
        y_ref[0, g] = _dot_pieces(_pieces(lhs, 2), [wy_ref[0, g], wy_ref[1, g]])


def _s5_scan(u4, wz, wy, a1, a2, n_ctx):
    b, g, nch, w = u4.shape
    return pl.pallas_call(
        functools.partial(_s5_kernel, n_ctx=n_ctx),
        grid=(b, g // S5GB),
        in_specs=[pl.BlockSpec((1, S5GB, nch, w), lambda bi, gi: (bi, gi, 0, 0)),
                  pl.BlockSpec((2, S5GB, w, w), lambda bi, gi: (0, gi, 0, 0)),
                  pl.BlockSpec((2, S5GB, 2 * w, w), lambda bi, gi: (0, gi, 0, 0)),
                  pl.BlockSpec((S5GB, 2, 128), lambda bi, gi: (gi, 0, 0)),
                  pl.BlockSpec((S5GB, 2, 128), lambda bi, gi: (gi, 0, 0))],
        out_specs=pl.BlockSpec((1, S5GB, nch, w), lambda bi, gi: (bi, gi, 0, 0)),
        out_shape=jax.ShapeDtypeStruct(u4.shape, F32),
        scratch_shapes=[pltpu.VMEM((S5GB, nch, w), F32), pltpu.VMEM((S5GB, nch, w), F32)],
        compiler_params=_cp("parallel", "parallel"),
        name="s5_scan",
    )(u4, wz, wy, a1, a2)


def _s5_weights(a_re, a_im, log_step, b_re, b_im, c_re, c_im):
    step = jnp.exp(log_step)[..., None]
    e_re, e_im = a_re * step, a_im * step
    def lam_pow(n):
        n = n[..., None, None, None] if n.ndim else n
        mag = jnp.exp(e_re * n)
        return mag * jnp.cos(e_im * n), mag * jnp.sin(e_im * n)
    l1r, l1i = lam_pow(jnp.asarray(1.0, F32))
    den = a_re * a_re + a_im * a_im
    fr = ((l1r - 1.0) * a_re + l1i * a_im) / den
    fi = (l1i * a_re - (l1r - 1.0) * a_im) / den
    bbr = fr[..., None] * b_re - fi[..., None] * b_im
    bbi = fr[..., None] * b_im + fi[..., None] * b_re
    j = jnp.arange(S5C, dtype=F32)
    es = functools.partial(jnp.einsum, precision=HI)

    def build(d):
        cr, ci = c_re[d], c_im[d]
        br, bi = bbr[d], bbi[d]
        sel = lambda x: x[:, d] if x.ndim == 4 else x
        pr, pi = lam_pow(jnp.arange(S5C + 1, dtype=F32))
        pr, pi = pr[:, d], pi[:, d]
        cl_r = cr[None] * pr[:, :, None, :] - ci[None] * pi[:, :, None, :]
        cl_i = cr[None] * pi[:, :, None, :] + ci[None] * pr[:, :, None, :]
        kk = es('ngop,gpi->ngoi', cl_r[:S5C], br) - es('ngop,gpi->ngoi', cl_i[:S5C], bi)
        ji, jo = jnp.arange(S5C)[:, None], jnp.arange(S5C)[None, :]
        lag = (jo - ji) if d == 0 else (ji - jo)
        kt = kk[jnp.clip(lag, 0, S5C - 1)]
        kt = jnp.where((lag >= 0)[:, :, None, None, None], kt, 0.0)
        toep = kt.transpose(2, 0, 4, 1, 3).reshape(S5G, S5C * 16, S5C * 16)
        m_idx = (jnp.arange(S5C) + 1) if d == 0 else (S5C - jnp.arange(S5C))
        wo_r = cl_r[m_idx].transpose(1, 3, 0, 2).reshape(S5G, S5P, S5C * 16)
        wo_i = -cl_i[m_idx].transpose(1, 3, 0, 2).reshape(S5G, S5P, S5C * 16)
        wout = jnp.concatenate([wo_r, wo_i], axis=1)
        e_idx = (S5C - 1 - jnp.arange(S5C)) if d == 0 else jnp.arange(S5C)
        lr, li = pr[e_idx], pi[e_idx]
        wi_r = (lr[..., None] * br[None] - li[..., None] * bi[None])
        wi_i = (lr[..., None] * bi[None] + li[..., None] * br[None])
        win = jnp.concatenate([wi_r.transpose(1, 0, 3, 2).reshape(S5G, S5C * 16, S5P),
                               wi_i.transpose(1, 0, 3, 2).reshape(S5G, S5C * 16, S5P)], axis=2)
        ar, ai = pr[S5C], pi[S5C]
        a1 = jnp.concatenate([ar, ar], axis=1)
        a2 = jnp.concatenate([-ai, ai], axis=1)
        return toep, wout, win, a1, a2

    tf, of, wf, a1f, a2f = build(0)
    tb, ob, wb, a1b, a2b = build(1)
    wz = jnp.concatenate([wf, wb], axis=2)
    wy = jnp.concatenate([tf + tb, of, ob], axis=1)
    split = lambda w: jnp.stack([w.astype(BF16), (w - w.astype(BF16).astype(F32)).astype(BF16)])
    return split(wz), split(wy), jnp.stack([a1f, a1b], axis=1), jnp.stack([a2f, a2b], axis=1)


def _gelu_tanh(x):
    return 0.5 * x * (1.0 + jnp.tanh(math.sqrt(2.0 / math.pi) * (x + 0.044715 * x * x * x)))


def _outproj_kernel(x_ref, mod_ref, gf_ref, gb_ref, gz_ref, ys_ref, us_ref, hi_ref, hf_ref, hb_ref, hg_ref,
                    rf_ref, rb_ref, rg_ref, vec_ref, glu_ref, ind_ref, w_ref, ln_ref, o_ref, *, alpha):
    ind = ind_ref[...]
    vec = vec_ref[...]

    def head_rms(o):
        return o * lax.rsqrt(_sum01_r(o * o, ind, 2) * (1.0 / HEAD) + RMS_EPS)

    m_gdn = head_rms(gf_ref[0] + gb_ref[0]) * vec[0:1] * _silu(gz_ref[0])
    u = us_ref[0]
    ys = _gelu_tanh(ys_ref[0] + vec[2:3] * u)
    m_s5 = ys * _sigmoid(_dot(ys, glu_ref[...]) + vec[3:4])
    m_hg = head_rms(hi_ref[0] + hf_ref[0] + hb_ref[0]) * vec[1:2] * _silu(hg_ref[0])
    m_rt = head_rms(rf_ref[0] + rb_ref[0]) * _silu(rg_ref[0])
    acc = None
    for k, mk in enumerate((m_gdn, m_s5, m_hg, m_rt)):
        part = _dot(mk.astype(BF16), w_ref[k * WM:(k + 1) * WM, :])
        acc = part if acc is None else acc + part
    m = mod_ref[0, 0]
    ln = ln_ref[...]
    o_ref[0] = _layer_norm(alpha * x_ref[0] + m[2:3] * acc, ln[0:1], ln[1:2])


def _outproj(xs, mod, g_of, g_ob, pg, ys, ps, h_oi, h_of, h_ob, ph, r_of, r_ob, pr, vec, glu_w, ind_sum, w_out, ln,
             alpha):
    b, t, d = xs.shape
    tile = lambda c: pl.BlockSpec((1, TT, WM), lambda bi, i, c=c: (bi, i, c))
    full = lambda a: pl.BlockSpec(a.shape, lambda bi, i: (0,) * a.ndim)
    return pl.pallas_call(
        functools.partial(_outproj_kernel, alpha=alpha),
        grid=(b, t // TT),
        in_specs=[pl.BlockSpec((1, TT, d), lambda bi, i: (bi, i, 0)),
                  pl.BlockSpec((1, 1, 6, d), lambda bi, i: (bi, jnp.minimum(i, 1), 0, 0)),
                  tile(0), tile(0), tile(3), tile(0), tile(0), tile(0), tile(0), tile(0), tile(4),
                  tile(0), tile(0), tile(3),
                  full(vec), full(glu_w), full(ind_sum), full(w_out), full(ln)],
        out_specs=pl.BlockSpec((1, TT, d), lambda bi, i: (bi, i, 0)),
        out_shape=jax.ShapeDtypeStruct(xs.shape, F32),
        compiler_params=_cp("parallel", "parallel"),
        name="outproj",
    )(xs, mod, g_of, g_ob, pg, ys, ps, h_oi, h_of, h_ob, ph, r_of, r_ob, pr, vec, glu_w, ind_sum, w_out, ln)


def _ffn_kernel(x_ref, mod_ref, w1_ref, w3_ref, w2_ref, ln_ref, o_ref, *, alpha):
    m = mod_ref[0, 0]
    x = x_ref[0]
    h = (x * (1.0 + m[4:5]) + m[3:4]).astype(BF16)
    act = (_silu(_dot(h, w1_ref[...])) * _dot(h, w3_ref[...])).astype(BF16)
    y = _dot(act, w2_ref[...])
    ln = ln_ref[...]
    o_ref[0] = _layer_norm(alpha * x + m[5:6] * y, ln[0:1], ln[1:2])


def _ffn(xs, mod, w1, w3, w2, ln, alpha):
    b, t, d = xs.shape
    full = lambda a: pl.BlockSpec(a.shape, lambda bi, i: (0,) * a.ndim)
    return pl.pallas_call(
        functools.partial(_ffn_kernel, alpha=alpha),
        grid=(b, t // TT),
        in_specs=[pl.BlockSpec((1, TT, d), lambda bi, i: (bi, i, 0)),
                  pl.BlockSpec((1, 1, 6, d), lambda bi, i: (bi, jnp.minimum(i, 1), 0, 0)),
                  full(w1), full(w3), full(w2), full(ln)],
        out_specs=pl.BlockSpec((1, TT, d), lambda bi, i: (bi, i, 0)),
        out_shape=jax.ShapeDtypeStruct(xs.shape, F32),
        compiler_params=_cp("parallel", "parallel"),
        name="ffn_dense",
    )(xs, mod, w1, w3, w2, ln)


def _router_kernel(x_ref, mod_ref, r_ref, h_ref, rt_ref):
    m = mod_ref[0, 0]
    h = x_ref[0] * (1.0 + m[4:5]) + m[3:4]
    h_ref[0] = h
    lane = lax.broadcasted_iota(jnp.int32, (TT, 128), 1)
    logits = jnp.where(lane < N_EXP, _dot(h, r_ref[...], HI), -jnp.inf)
    m1 = jnp.max(logits, axis=-1, keepdims=True)
    i1 = jnp.min(jnp.where(logits == m1, lane, 128), axis=-1, keepdims=True)
    rest = jnp.where(lane == i1, -jnp.inf, logits)
    m2 = jnp.max(rest, axis=-1, keepdims=True)
    i2 = jnp.min(jnp.where(rest == m2, lane, 128), axis=-1, keepdims=True)
    e = jnp.exp(m2 - m1)
    g1 = 1.0 / (1.0 + e)
    g2 = e / (1.0 + e)
    rt_ref[0] = jnp.where(lane == 0, i1.astype(F32),
                          jnp.where(lane == 1, i2.astype(F32),
                                    jnp.where(lane == 2, g1, jnp.where(lane == 3, g2, 0.0))))


def _router(xs, mod, router_pad):
    b, t, d = xs.shape
    return pl.pallas_call(
        _router_kernel,
        grid=(b, t // TT),
        in_specs=[pl.BlockSpec((1, TT, d), lambda bi, i: (bi, i, 0)),
                  pl.BlockSpec((1, 1, 6, d), lambda bi, i: (bi, jnp.minimum(i, 1), 0, 0)),
                  pl.BlockSpec(router_pad.shape, lambda bi, i: (0, 0))],
        out_specs=[pl.BlockSpec((1, TT, d), lambda bi, i: (bi, i, 0)),
                   pl.BlockSpec((1, TT, 128), lambda bi, i: (bi, i, 0))],
        out_shape=[jax.ShapeDtypeStruct(xs.shape, F32), jax.ShapeDtypeStruct((b, t, 128), F32)],
        compiler_params=_cp("parallel", "parallel"),
        name="moe_router",
    )(xs, mod, router_pad)


def _row_copy(src_hbm, row, dst_ref, slot, sem):
    return pltpu.make_async_copy(src_hbm.at[pl.ds(row, 1), :], dst_ref.at[pl.ds(slot, 1), :], sem)


def _dispatch_kernel(dest_ref, h_ref, init_hbm, o_hbm, sem, *, tiles_per_batch):
    del init_hbm
    tile = pl.program_id(0) * tiles_per_batch + pl.program_id(1)
    base = tile * (2 * TT)

    def start(r, c):
        t = lax.rem(r, TT)
        pltpu.make_async_copy(h_ref.at[0, pl.ds(t, 1), :], o_hbm.at[pl.ds(dest_ref[base + r], 1), :], sem).start()
        return c

    lax.fori_loop(0, 2 * TT, start, 0, unroll=8)
    for _ in range(2):
        pltpu.make_async_copy(h_ref.at[0], o_hbm.at[pl.ds(0, TT), :], sem).wait()


def _dispatch_rows(h, dest_tiles, n_rows):
    b, t, d = h.shape
    return pl.pallas_call(
        functools.partial(_dispatch_kernel, tiles_per_batch=t // TT),
        grid_spec=pltpu.PrefetchScalarGridSpec(
            num_scalar_prefetch=1,
            grid=(b, t // TT),
            in_specs=[pl.BlockSpec((1, TT, d), lambda bi, i, dr: (bi, i, 0)),
                      pl.BlockSpec(memory_space=pl.ANY)],
            out_specs=pl.BlockSpec(memory_space=pl.ANY),
            scratch_shapes=[pltpu.SemaphoreType.DMA(())]),
        out_shape=jax.ShapeDtypeStruct((n_rows, d), h.dtype),
        input_output_aliases={2: 0},
        compiler_params=_cp("arbitrary", "arbitrary"),
        name="moe_dispatch",
    )(dest_tiles, h, jnp.zeros((n_rows, d), h.dtype))


def _experts_kernel(be_ref, x_ref, w1_ref, w3_ref, w2_ref, y_ref):
    f = pl.program_id(1)
    x = x_ref[...].astype(BF16)
    act = (_silu(_dot(x, w1_ref[0])) * _dot(x, w3_ref[0])).astype(BF16)
    y = _dot(act, w2_ref[0])

    @pl.when(f == 0)
    def _():
        y_ref[...] = y

    @pl.when(f != 0)
    def _():
        y_ref[...] += y


def _experts(xs_sorted, block_expert, w1, w3, w2):
    n_rows, d = xs_sorted.shape
    ff = w1.shape[2]
    nf = 2
    tf = ff // nf
    return pl.pallas_call(
        _experts_kernel,
        grid_spec=pltpu.PrefetchScalarGridSpec(
            num_scalar_prefetch=1,
            grid=(n_rows // MOE_RB, nf),
            in_specs=[pl.BlockSpec((MOE_RB, d), lambda j, f, be: (j, 0)),
                      pl.BlockSpec((1, d, tf), lambda j, f, be: (be[j], 0, f)),
                      pl.BlockSpec((1, d, tf), lambda j, f, be: (be[j], 0, f)),
                      pl.BlockSpec((1, tf, d), lambda j, f, be: (be[j], f, 0))],
            out_specs=pl.BlockSpec((MOE_RB, d), lambda j, f, be: (j, 0))),
        out_shape=jax.ShapeDtypeStruct((n_rows, d), F32),
        compiler_params=_cp("parallel", "arbitrary"),
        name="moe_experts",
    )(block_expert, xs_sorted, w1, w3, w2)


def _combine_kernel(dest_ref, x_ref, mod_ref, rt_ref, ln_ref, y_hbm, o_ref, buf, sem, *, alpha, tiles_per_batch):
    tile = pl.program_id(0) * tiles_per_batch + pl.program_id(1)
    base = tile * (2 * TT)

    def start(r, c):
        _row_copy(y_hbm, dest_ref[base + r], buf, r, sem).start()
        return c

    lax.fori_loop(0, 2 * TT, start, 0, unroll=8)
    pltpu.make_async_copy(y_hbm.at[pl.ds(0, 2 * TT), :], buf, sem).wait()
    rt = rt_ref[0]
    y = rt[:, 2:3] * buf[0:TT, :] + rt[:, 3:4] * buf[TT:2 * TT, :]
    m = mod_ref[0, 0]
    ln = ln_ref[...]
    o_ref[0] = _layer_norm(alpha * x_ref[0] + m[5:6] * y, ln[0:1], ln[1:2])


def _combine(xs, mod, rt, ln, ys_sorted, dest_tiles, alpha):
    b, t, d = xs.shape
    return pl.pallas_call(
        functools.partial(_combine_kernel, alpha=alpha, tiles_per_batch=t // TT),
        grid_spec=pltpu.PrefetchScalarGridSpec(
            num_scalar_prefetch=1,
            grid=(b, t // TT),
            in_specs=[pl.BlockSpec((1, TT, d), lambda bi, i, dr: (bi, i, 0)),
                      pl.BlockSpec((1, 1, 6, d), lambda bi, i, dr: (bi, jnp.minimum(i, 1), 0, 0)),
                      pl.BlockSpec((1, TT, 128), lambda bi, i, dr: (bi, i, 0)),
                      pl.BlockSpec(ln.shape, lambda bi, i, dr: (0, 0)),
                      pl.BlockSpec(memory_space=pl.ANY)],
            out_specs=pl.BlockSpec((1, TT, d), lambda bi, i, dr: (bi, i, 0)),
            scratch_shapes=[pltpu.VMEM((2 * TT, d), F32), pltpu.SemaphoreType.DMA(())]),
        out_shape=jax.ShapeDtypeStruct(xs.shape, F32),
        compiler_params=_cp("arbitrary", "arbitrary"),
        name="moe_combine",
    )(dest_tiles, xs, mod, rt, ln, ys_sorted)


def _moe(xs, mod, router_pad, w1, w3, w2, ln, alpha):
    b, t, d = xs.shape
    n_tok = b * t
    h, rt = _router(xs, mod, router_pad)
    e_idx = rt[..., 0:2].astype(jnp.int32).reshape(n_tok * 2)
    onehot = (e_idx[:, None] == jnp.arange(N_EXP, dtype=jnp.int32)[None, :]).astype(jnp.int32)
    csum = jnp.cumsum(onehot, axis=0)
    counts = csum[-1]
    padded = (counts + MOE_RB - 1) // MOE_RB * MOE_RB
    pad_end = jnp.cumsum(padded)
    pad_start = pad_end - padded
    dest = jnp.sum(onehot * (csum - 1 + pad_start[None, :]), axis=1)
    n_blocks = -(-(n_tok * 2) // MOE_RB) + N_EXP
    n_rows = n_blocks * MOE_RB
    block_expert = jnp.minimum(
        jnp.sum(jnp.arange(n_blocks, dtype=jnp.int32)[:, None] * MOE_RB >= pad_end[None, :], axis=1),
        N_EXP - 1).astype(jnp.int32)
    dest_tiles = dest.reshape(n_tok // TT, TT, 2).transpose(0, 2, 1).reshape(n_tok * 2)
    xs_sorted = _dispatch_rows(h, dest_tiles, n_rows)
    ys_sorted = _experts(xs_sorted, block_expert, w1, w3, w2)
    return _combine(xs, mod, rt, ln, ys_sorted, dest_tiles, alpha)


def _ret_constants(decay_param):
    log_gamma = -jnp.exp(decay_param)
    idx = jnp.arange(TT, dtype=F32)
    diff = idx[:, None] - idx[None, :]
    lg = log_gamma[:, :, None, None]
    dec_f = jnp.exp(jnp.where(diff >= 0, diff * lg[0], -jnp.inf))
    dec_b = jnp.exp(jnp.where(diff <= 0, -diff * lg[1], -jnp.inf))
    dec = jnp.stack([dec_f, dec_b])
    rep = lambda a: jnp.repeat(a, HEAD, axis=-1)
    qdec = jnp.stack([rep(jnp.exp((idx[:, None] + 1.0) * log_gamma[0][None, :])),
                      rep(jnp.exp((TT - idx[:, None]) * log_gamma[1][None, :]))])
    kdec = jnp.stack([rep(jnp.exp((TT - 1.0 - idx[:, None]) * log_gamma[0][None, :])),
                      rep(jnp.exp(idx[:, None] * log_gamma[1][None, :]))])
    cdec = jnp.exp(TT * log_gamma)
    return dec, qdec, kdec, cdec


def _rotary_tables(n_ctx_tok, n_lat, grid_w):
    rows = jnp.repeat(jnp.arange(n_lat // grid_w, dtype=F32), grid_w)
    cols = jnp.tile(jnp.arange(grid_w, dtype=F32), n_lat // grid_w)
    quarter = HEAD // 4
    inv_freq = ROPE_BASE ** (-jnp.arange(quarter, dtype=F32) / quarter)
    ang = jnp.concatenate([rows[:, None] * inv_freq, cols[:, None] * inv_freq], axis=-1)
    cos, sin = jnp.cos(ang), jnp.sin(ang)
    cos_h = jnp.concatenate([cos, cos], axis=-1)
    sin_h = jnp.concatenate([-sin, sin], axis=-1)
    cos_t = jnp.concatenate([jnp.ones((n_ctx_tok, HEAD), F32), cos_h], axis=0)
    sin_t = jnp.concatenate([jnp.zeros((n_ctx_tok, HEAD), F32), sin_h], axis=0)
    return jnp.tile(cos_t, (1, NH)), jnp.tile(sin_t, (1, NH))


def _swap_halves_cols(w):
    dm = w.shape[0]
    return w.reshape(dm, NH, 2, HEAD // 2)[:, :, ::-1, :].reshape(dm, WM)


def _prep_w_in(w):
    dm = w.shape[0]
    r0 = 3344 - 768
    rq, rk = w[:, r0:r0 + WM], w[:, r0 + WM:r0 + 2 * WM]
    return jnp.concatenate([w[:, :1040], jnp.zeros((dm, W_AB - 16), w.dtype), w[:, 1040:],
                            _swap_halves_cols(rq), _swap_halves_cols(rk)], axis=1).astype(BF16)


def kernel(x, c, ctx, c_ctx, ada_w, ada_b, w_in, w_out, ln_g, ln_b, gdn_conv_w, gdn_a_log, gdn_dt_bias, gdn_norm_w, s5_a_re, s5_a_im, s5_log_step, s5_b_re, s5_b_im, s5_c_re, s5_c_im, s5_d, s5_glu_w, s5_glu_b, hgrn_lower_bounds, hgrn_norm_w, ret_decay, ffn_w1, ffn_w3, ffn_w2, moe_router, moe_w1, moe_w3, moe_w2):
    bsz, n_lat, d = x.shape
    n_ctx_tok = ctx.shape[1]
    depth = ada_w.shape[0]
    grid_w = 64
    assert n_ctx_tok == TT and n_lat % TT == 0 and bsz <= 7
    t = n_ctx_tok + n_lat
    alpha = (2.0 * depth) ** 0.25

    xs = jnp.concatenate([ctx, x], axis=1)
    cs = jnp.concatenate([c, c_ctx[None, :], jnp.zeros((8 - bsz - 1, d), F32)], axis=0)
    mod_all = _ada_mod(cs, ada_w, ada_b)
    lat_mod = mod_all[:, :bsz].reshape(depth, bsz, 6, d)
    ctx_mod = jnp.broadcast_to(mod_all[:, bsz].reshape(depth, 1, 6, d), (depth, bsz, 6, d))
    mod_tab = jnp.stack([ctx_mod, lat_mod], axis=2)

    lb_all = jnp.cumsum(jax.nn.softmax(hgrn_lower_bounds.astype(F32), axis=0), axis=0)
    lb_all = lb_all - lb_all[0]
    ind_sum = jnp.asarray(np.kron(np.eye(NH), np.ones((HEAD, HEAD))), BF16)
    chunks_eye = np.eye(TT // CH)
    tri_bd = jnp.asarray(np.kron(chunks_eye, np.tril(np.ones((CH, CH)))), BF16)
    ones_bd = jnp.asarray(np.kron(chunks_eye, np.ones((CH, CH))), BF16)
    ones_c = jnp.ones((CH, HEAD), BF16)
    mall_np, masks_np = _hgrn_constants()
    mall, masks = jnp.asarray(mall_np, BF16), jnp.asarray(masks_np)
    cos_t, sin_t = _rotary_tables(n_ctx_tok, n_lat, grid_w)
    nch = t // S5C

    for layer in range(depth):
        mod = mod_tab[layer]
        pg, pab, ps, ph, pr = _inproj(xs, mod, _prep_w_in(w_in[layer]))

        pabt = pab[..., :16].transpose(0, 2, 1)
        g_loc = _gdn_local(pg, pab, pabt, gdn_conv_w[layer], ind_sum, -jnp.exp(gdn_a_log[layer]),
                           gdn_dt_bias[layer], tri_bd, ones_bd)
        g_of, g_ob = _gdn_scan(g_loc, ind_sum, n_ctx_tok // CH)

        wz, wy, a1, a2 = _s5_weights(s5_a_re[layer], s5_a_im[layer], s5_log_step[layer], s5_b_re[layer],
                                     s5_b_im[layer], s5_c_re[layer], s5_c_im[layer])
        u4 = ps.reshape(bsz, nch, S5C, S5G, 16).transpose(0, 3, 1, 2, 4).reshape(bsz, S5G, nch, S5C * 16)
        y4 = _s5_scan(u4, wz, wy, a1, a2, n_ctx_tok // S5C)
        ys = y4.reshape(bsz, S5G, nch, S5C, 16).transpose(0, 2, 3, 1, 4).reshape(bsz, t, WM)

        lb = lb_all[layer][None, :]
        lbp = jnp.concatenate([jnp.log(lb), jnp.log1p(-lb), 1.0 - lb, jnp.zeros((5, WM), F32)], axis=0)
        h_oi, h_qgf, h_qgb, h_kvf, h_kvb, h_ecf, h_ecb = _hgrn_local(ph, lbp, mall, masks, ind_sum, ones_c)
        h_of, h_ob = _hgrn_scan(h_qgf, h_qgb, h_kvf, h_kvb, h_ecf, h_ecb, n_ctx_tok // CH)

        dec, qdec, kdec, cdec = _ret_constants(ret_decay[layer])
        r_of, r_ob = _ret_scan(pr, cos_t, sin_t, dec, qdec, kdec, cdec)

        vec = jnp.concatenate([jnp.tile(gdn_norm_w[layer], NH)[None], jnp.tile(hgrn_norm_w[layer], NH)[None],
                               s5_d[layer][None], s5_glu_b[layer][None], jnp.zeros((4, WM), F32)], axis=0)
        xs = _outproj(xs, mod, g_of, g_ob, pg, ys, ps, h_oi, h_of, h_ob, ph, r_of, r_ob, pr, vec, s5_glu_w[layer],
                      ind_sum, w_out[layer].astype(BF16), jnp.stack([ln_g[layer, 0], ln_b[layer, 0]]), alpha)

        j = layer // 2
        ln2 = jnp.stack([ln_g[layer, 1], ln_b[layer, 1]])
        if layer % 2 == 0:
            xs = _ffn(xs, mod, ffn_w1[j].astype(BF16), ffn_w3[j].astype(BF16), ffn_w2[j].astype(BF16), ln2, alpha)
        else:
            router_pad = jnp.concatenate([moe_router[j], jnp.zeros((d, 128 - N_EXP), F32)], axis=1)
            xs = _moe(xs, mod, router_pad, moe_w1[j].astype(BF16), moe_w3[j].astype(BF16),
                      moe_w2[j].astype(BF16), ln2, alpha)
    return xs[:, n_ctx_tok:, :]
```

```python
import functools
import math

import numpy as np
import jax
import jax.numpy as jnp
from jax import lax
from jax.experimental import pallas as pl
from jax.experimental.pallas import tpu as pltpu

F32 = jnp.float32
BF16 = jnp.bfloat16
HI = lax.Precision.HIGHEST

HEAD = 64
NH = 4
WM = NH * HEAD
TT = 256
CH = 64
S5C = 16
S5G = 16
S5P = 64
S5GB = 8
N_EXP = 8
MOE_RB = 512
LN_EPS = 1e-5
RMS_EPS = 1e-6
ROPE_BASE = 10000.0
VMEM_LIMIT = 56 * 1024 * 1024


def _cp(*sem):
    return pltpu.CompilerParams(dimension_semantics=sem, vmem_limit_bytes=VMEM_LIMIT)


def _sigmoid(x):
    return 1.0 / (1.0 + jnp.exp(-x))


def _silu(x):
    return x * _sigmoid(x)


def _softplus(x):
    return jnp.maximum(x, 0.0) + jnp.log1p(jnp.exp(-jnp.abs(x)))


def _dot(a, b, precision=None):
    return jnp.dot(a, b, preferred_element_type=F32, precision=precision)


def _dot_nt(a, b, precision=None):
    return lax.dot_general(a, b, (((1,), (1,)), ((), ())), preferred_element_type=F32, precision=precision)


def _dot_tn(a, b, precision=None):
    return lax.dot_general(a, b, (((0,), (0,)), ((), ())), preferred_element_type=F32, precision=precision)


def _layer_norm(y, g, b):
    mu = jnp.mean(y, axis=-1, keepdims=True)
    yc = y - mu
    var = jnp.mean(yc * yc, axis=-1, keepdims=True)
    return yc * lax.rsqrt(var + LN_EPS) * g + b


def _rev_index(n, n_ctx, n_all):
    return jnp.where(n < n_ctx, n_ctx - 1 - n, n_all + n_ctx - 1 - n)


def _ada_kernel(c_ref, w_ref, b_ref, o_ref):
    o_ref[0] = _dot(_silu(c_ref[...]), w_ref[0], HI) + b_ref[0]


def _ada_mod(cs, ada_w, ada_b):
    depth, d, d6 = ada_w.shape
    tn = 1024
    return pl.pallas_call(
        _ada_kernel,
        grid=(depth, d6 // tn),
        in_specs=[pl.BlockSpec((8, d), lambda l, j: (0, 0)),
                  pl.BlockSpec((1, d, tn), lambda l, j: (l, 0, j)),
                  pl.BlockSpec((1, 1, tn), lambda l, j: (l, 0, j))],
        out_specs=pl.BlockSpec((1, 8, tn), lambda l, j: (l, 0, j)),
        out_shape=jax.ShapeDtypeStruct((depth, 8, d6), F32),
        compiler_params=_cp("parallel", "parallel"),
        name="ada_mod",
    )(cs, ada_w, ada_b.reshape(depth, 1, d6))


W_PG, W_AB, W_S5, W_PH, W_PR = 1024, 128, 256, 1280, 1536
P_OFF = np.cumsum([0, W_PG, W_AB, W_S5, W_PH, W_PR])


def _inproj_kernel(x_ref, mod_ref, w_ref, pg_ref, pab_ref, ps_ref, ph_ref, pr_ref):
    m = mod_ref[0, 0]
    h = (x_ref[0] * (1.0 + m[1:2]) + m[0:1]).astype(BF16)
    for k, o_ref in enumerate((pg_ref, pab_ref, ps_ref, ph_ref, pr_ref)):
        o_ref[0] = _dot(h, w_ref[:, P_OFF[k]:P_OFF[k + 1]])


def _inproj(xs, mod, w):
    b, t, d = xs.shape
    widths = (W_PG, W_AB, W_S5, W_PH, W_PR)
    return pl.pallas_call(
        _inproj_kernel,
        grid=(b, t // TT),
        in_specs=[pl.BlockSpec((1, TT, d), lambda bi, i: (bi, i, 0)),
                  pl.BlockSpec((1, 1, 6, d), lambda bi, i: (bi, jnp.minimum(i, 1), 0, 0)),
                  pl.BlockSpec(w.shape, lambda bi, i: (0, 0))],
        out_specs=[pl.BlockSpec((1, TT, wd), lambda bi, i: (bi, i, 0)) for wd in widths],
        out_shape=[jax.ShapeDtypeStruct((b, t, wd), F32) for wd in widths],
        compiler_params=_cp("parallel", "parallel"),
        name="inproj",
    )(xs, mod, w)


def _pieces(x, n):
    out, r = [], x
    for i in range(n):
        p = r.astype(BF16)
        out.append(p)
        if i + 1 < n:
            r = r - p.astype(F32)
    return out


def _dot_pieces(a_parts, b_parts, dot=_dot):
    n = max(len(a_parts), len(b_parts))
    acc = None
    for i, ap in enumerate(a_parts):
        for j, bp in enumerate(b_parts):
            if i + j < n:
                t = dot(ap, bp)
                acc = t if acc is None else acc + t
    return acc


def _sum01_l(m01, x, n):
    return _dot_pieces([m01], _pieces(x, n))


def _sum01_r(x, m01, n, dot=_dot):
    return _dot_pieces(_pieces(x, n), [m01], dot)


GDN_PIECES = 1
GDN_BASE = 16


def _unit_tri_inverse(a_list, eye, masks):
    n = GDN_PIECES
    ident = jnp.where(eye, 1.0, 0.0)
    ds = [jnp.where(masks[0], a, 0.0) for a in a_list]
    ts = [ident - d for d in ds]
    ps = [_pieces(d, n) for d in ds]
    size = 2
    while size < GDN_BASE:
        ps = [_pieces(_dot_pieces(p, p), n) for p in ps]
        ts = [t + _dot_pieces(_pieces(t, n), p) for t, p in zip(ts, ps)]
        size *= 2
    for off_mask in masks[1:]:
        tps = [_pieces(t, n) for t in ts]
        mids = [_pieces(_dot_pieces(_pieces(jnp.where(off_mask, a, 0.0), n), tp), n) for a, tp in zip(a_list, tps)]
        ts = [t - _dot_pieces(tp, mid) for t, tp, mid in zip(ts, tps, mids)]
    return ts


def _gdn_local_kernel(na_ref, dtb_ref, p_ref, pv_ref, nx_ref, cw_ref, ind_ref, ab_ref, abt_ref, tri_ref, trit_ref,
                      ones_ref, uf_ref, ub_ref, wf_ref, wb_ref, qkf_ref, qkb_ref, qgf_ref, qgb_ref, kdf_ref, kdb_ref,
                      eg_ref):
    i = pl.program_id(1)
    nt = pl.num_programs(1)
    x = p_ref[0]
    prev = jnp.where(i >= 2, pv_ref[0][7:8], 0.0)
    nxt = jnp.where((i >= 1) & (i < nt - 1), nx_ref[0][0:1], 0.0)
    row1 = lax.broadcasted_iota(jnp.int32, (TT, 1), 0)
    xm = jnp.where(row1 == 0, prev, pltpu.roll(x, 1, 0))
    xp = jnp.where(row1 == TT - 1, nxt, pltpu.roll(x, TT - 1, 0))
    cw = cw_ref[...]
    y = _silu(cw[0:1] * xm + cw[1:2] * x + cw[2:3] * xp)
    q, k, v = y[:, :WM], y[:, WM:2 * WM], y[:, 2 * WM:]
    ind = ind_ref[...]
    q = q * lax.rsqrt(_sum01_r(q * q, ind, 2) + RMS_EPS) * HEAD ** -0.5
    k = k * lax.rsqrt(_sum01_r(k * k, ind, 2) + RMS_EPS)

    ab = ab_ref[0]
    abt = abt_ref[0]
    ones_bd = ones_ref[...]
    row = lax.broadcasted_iota(jnp.int32, (TT, TT), 0)
    col = lax.broadcasted_iota(jnp.int32, (TT, TT), 1)
    same = (row // CH) == (col // CH)
    eye = row == col
    in_block = lambda n: (row // n) == (col // n)
    inv_masks, n = [in_block(GDN_BASE)], GDN_BASE
    while n < CH:
        inv_masks.append(in_block(2 * n) & jnp.logical_not(in_block(n)))
        n *= 2
    out_refs =((uf_ref, wf_ref, qkf_ref, qgf_ref, kdf_ref), (ub_ref, wb_ref, qkb_ref, qgb_ref, kdb_ref))
    e_last = []
    for d in range(2):
        incl = same & ((row >= col) if d == 0 else (row <= col))
        strict = same & ((row > col) if d == 0 else (row < col))
        tri_c = tri_ref[...] if d == 0 else trit_ref[...]
        a_col, b_col = ab[:, 4 * d:4 * d + 4], ab[:, 8 + 4 * d:12 + 4 * d]
        a_row = abt[4 * d:4 * d + 4, :]
        g_col = jnp.concatenate([na_ref[d, h] * _softplus(a_col[:, h:h + 1] + dtb_ref[d, h]) for h in range(NH)], axis=1)
        g_row = jnp.concatenate([na_ref[d, h] * _softplus(a_row[h:h + 1, :] + dtb_ref[d, h]) for h in range(NH)], axis=0)
        gc_col = _sum01_l(tri_c, g_col, 3)
        gc_row = _sum01_r(g_row, tri_c, 3, _dot_nt)
        gl_col = _sum01_l(ones_bd, g_col, 3)
        e_last.append(jnp.exp(gl_col))
        a_list, rhs, qks, qgs, kds = [], [], [], [], []
        for h in range(NH):
            sl = slice(h * HEAD, (h + 1) * HEAD)
            qh, kh, vh = q[:, sl], k[:, sl], v[:, sl]
            beta = _sigmoid(b_col[:, h:h + 1])
            gcc = gc_col[:, h:h + 1]
            decay = jnp.exp(jnp.where(incl, gcc - gc_row[h:h + 1, :], -jnp.inf))
            kb = kh * beta
            a_list.append(_dot_nt(kb, kh) * jnp.where(strict, decay, 0.0))
            rhs.append(_pieces(jnp.concatenate([vh * beta, kb * jnp.exp(gcc)], axis=1), GDN_PIECES))
            qk = _dot_nt(qh, kh) * decay
            qks.append(qk[:, 0:CH] + qk[:, CH:2 * CH] + qk[:, 2 * CH:3 * CH] + qk[:, 3 * CH:4 * CH])
            qgs.append(qh * jnp.exp(gcc))
            kds.append(kh * jnp.exp(gl_col[:, h:h + 1] - gcc))
        xs = [_dot_pieces(_pieces(t, GDN_PIECES), r) for t, r in zip(_unit_tri_inverse(a_list, eye, inv_masks), rhs)]
        us = [xx[:, :HEAD] for xx in xs]
        ws = [xx[:, HEAD:] for xx in xs]
        u_ref, w_ref, qk_ref, qg_ref, kd_ref = out_refs[d]
        u_ref[0] = jnp.concatenate(us, axis=1)
        w_ref[0] = jnp.concatenate(ws, axis=1).astype(BF16)
        qk_ref[0] = jnp.concatenate(qks, axis=1).astype(BF16)
        qg_ref[0] = jnp.concatenate(qgs, axis=1).astype(BF16)
        kdt = jnp.concatenate(kds, axis=1).T
        for c in range(TT // CH):
            kd_ref[0, c] = kdt[:, c * CH:(c + 1) * CH].astype(BF16)
    eg_ref[0] =jnp.concatenate(e_last + [jnp.zeros((TT, 128 - 2 * NH), F32)], axis=1)


def _gdn_local(pg, pab, pabt, conv_w, ind_sum, neg_a, dt_bias, tri_bd, ones_bd):
    b, t, _ = pg.shape
    w3 = 3 * WM
    n8 = t // 8
    smem = pl.BlockSpec(memory_space=pltpu.SMEM)
    full = lambda a: pl.BlockSpec(a.shape, lambda bi, i: (0,) * a.ndim)
    tile = pl.BlockSpec((1, TT, WM), lambda bi, i: (bi, i, 0))
    f32o = jax.ShapeDtypeStruct((b, t, WM), F32)
    b16o = jax.ShapeDtypeStruct((b, t, WM), BF16)
    return pl.pallas_call(
        _gdn_local_kernel,
        grid=(b, t // TT),
        in_specs=[smem, smem,
                  pl.BlockSpec((1, TT, w3), lambda bi, i: (bi, i, 0)),
                  pl.BlockSpec((1, 8, w3), lambda bi, i: (bi, jnp.maximum(i * (TT // 8) - 1, 0), 0)),
                  pl.BlockSpec((1, 8, w3), lambda bi, i: (bi, jnp.minimum((i + 1) * (TT // 8), n8 - 1), 0)),
                  pl.BlockSpec((3, w3), lambda bi, i: (0, 0)),
                  full(ind_sum),
                  pl.BlockSpec((1, TT, W_AB), lambda bi, i: (bi, i, 0)),
                  pl.BlockSpec((1, 16, TT), lambda bi, i: (bi, 0, i)),
                  full(tri_bd), full(tri_bd), full(ones_bd)],
        out_specs=[tile] * 8 + [pl.BlockSpec((1, TT // CH, WM, CH), lambda bi, i: (bi, i, 0, 0))] * 2
                  + [pl.BlockSpec((1, TT, 128), lambda bi, i: (bi, i, 0))],
        out_shape=[f32o, f32o] + [b16o] * 6 + [jax.ShapeDtypeStruct((b, t // CH, WM, CH), BF16)] * 2
                  + [jax.ShapeDtypeStruct((b, t, 128), F32)],
        compiler_params=_cp("parallel", "parallel"),
        name="gdn_local",
    )(neg_a, dt_bias, pg, pg, pg, conv_w, ind_sum, pab, pabt, tri_bd, tri_bd.T, ones_bd)


def _gdn_scan_kernel(uf_ref, ub_ref, wf_ref, wb_ref, qkf_ref, qkb_ref, qgf_ref, qgb_ref, kdf_ref, kdb_ref,
                     egf_ref, egb_ref, mask_ref, of_ref, ob_ref, s_ref):
    @pl.when(pl.program_id(0) == 0)
    def _():
        s_ref[...] = jnp.zeros_like(s_ref)

    dirs = ((uf_ref, wf_ref, qkf_ref, qgf_ref, kdf_ref, egf_ref, of_ref),
            (ub_ref, wb_ref, qkb_ref, qgb_ref, kdb_ref, egb_ref, ob_ref))
    mask_b = mask_ref[...]
    mask_f = mask_b.astype(F32)
    for b in range(uf_ref.shape[0]):
        for d, (u_ref, w_ref, qk_ref, qg_ref, kdt_ref, eg_ref, o_ref) in enumerate(dirs):
            s = s_ref[b, d]
            sb = s.astype(BF16)
            ws = _dot(jnp.concatenate([w_ref[b], qg_ref[b]], axis=0), sb)
            vb = (u_ref[b] - ws[:CH]).astype(BF16)
            v_bd = jnp.concatenate([vb] * NH, axis=0) * mask_b
            o_ref[b] = ws[CH:] + _dot(qk_ref[b], v_bd)
            eg = eg_ref[b][0:1, :]
            e_row = jnp.concatenate(
                [jnp.broadcast_to(eg[:, 4 * d + h:4 * d + h + 1], (1, HEAD)) for h in range(NH)], axis=1)
            s_ref[b, d] = s * e_row + _dot(kdt_ref[b, 0], vb) * mask_f


def _gdn_scan(loc, head_mask, n_ctx):
    uf, ub, wf, wb, qkf, qkb, qgf, qgb, kdf, kdb, eg = loc
    b, t, _ = uf.shape
    nc = t // CH
    fwd = lambda n: (0, n, 0)
    bwd = lambda n: (0, _rev_index(n, n_ctx, nc), 0)
    fwd4 = lambda n: (0, n, 0, 0)
    bwd4 = lambda n: (0, _rev_index(n, n_ctx, nc), 0, 0)
    blk = lambda im, w=WM: pl.BlockSpec((b, CH, w), im)
    kdt = lambda im: pl.BlockSpec((b, 1, WM, CH), im)
    return pl.pallas_call(
        _gdn_scan_kernel,
        grid=(nc,),
        in_specs=[blk(fwd), blk(bwd)] * 4 + [kdt(fwd4), kdt(bwd4), blk(fwd, 128), blk(bwd, 128),
                                              pl.BlockSpec(head_mask.shape, lambda n: (0, 0))],
        out_specs=[blk(fwd), blk(bwd)],
        out_shape=[jax.ShapeDtypeStruct((b, t, WM), F32)] * 2,
        scratch_shapes=[pltpu.VMEM((b, 2, WM, WM), F32)],
        compiler_params=_cp("arbitrary"),
        name="gdn_scan",
    )(uf, ub, wf, wb, qkf, qkb, qgf, qgb, kdf, kdb, eg, eg, head_mask)


HG_LEVELS = (32, 16, 8, 4, 2, 1)


def _hgrn_constants():
    idx = np.arange(CH)
    i, t = idx[:, None], idx[None, :]
    blocks = [(t <= i), (t > i)]
    masks = []
    for s in HG_LEVELS:
        m = (idx // (2 * s)) * 2 * s + s
        sec = (idx % (2 * s)) >= s
        mi = m[:, None]
        blocks.append(sec[:, None] & (t >= mi) & (t <= i))
        blocks.append((~sec)[:, None] & (t > i) & (t <= mi - 1))
        same = (idx[:, None] // (2 * s)) == (idx[None, :] // (2 * s))
        masks.append(same & sec[:, None] & (~sec)[None, :])
    mall_f = np.concatenate(blocks, axis=0).astype(np.float32)
    masks_f = np.stack(masks).astype(np.float32)
    nb = len(blocks)
    mall_b = mall_f.reshape(nb, CH, CH)[:, ::-1, ::-1].reshape(nb * CH, CH)
    masks_b = masks_f[:, ::-1, ::-1]
    eye = np.eye(TT // CH, dtype=np.float32)
    bd = lambda m: np.stack([np.kron(eye, m[lv]) for lv in range(len(HG_LEVELS))])
    return np.stack([mall_f, mall_b]), np.stack([bd(masks_f), bd(masks_b)])


def _hgrn_local_kernel(ph_ref, lbp_ref, mall_ref, mask_ref, ind_ref, ones_ref,
                       oi_ref, qgf_ref, qgb_ref, kvf_ref, kvb_ref, ecf_ref, ecb_ref):
    ph = ph_ref[0]
    lbp = lbp_ref[...]
    ind = ind_ref[...]
    ones = ones_ref[...]
    log_lb, log_1m_lb, one_m_lb = lbp[0:1], lbp[1:2], lbp[2:3]
    q = _silu(ph[:, :WM])
    v = ph[:, 3 * WM:4 * WM]
    ncl = TT // CH
    o_sum = None
    for d, (qg_ref, kv_ref, ec_ref) in enumerate(((qgf_ref, kvf_ref, ecf_ref), (qgb_ref, kvb_ref, ecb_ref))):
        fz = ph[:, WM * (1 + d):WM * (2 + d)]
        lsig = jnp.minimum(fz, 0.0) - jnp.log1p(jnp.exp(-jnp.abs(fz)))
        bb = log_1m_lb + lsig
        logf = jnp.maximum(log_lb, bb) + jnp.log1p(jnp.exp(-jnp.abs(log_lb - bb)))
        k = one_m_lb / (1.0 + jnp.exp(fz))
        e_c = [jnp.exp(_sum01_l(mall_ref[d], logf[c * CH:(c + 1) * CH], 2)) for c in range(ncl)]
        blk = lambda r: jnp.concatenate([e_c[c][r * CH:(r + 1) * CH] for c in range(ncl)], axis=0)
        qg_ref[0] = (q * blk(0)).astype(BF16)
        kd = k * blk(1)
        o_d = _sum01_r(q * k, ind, 2) * v
        q_lv = [q * blk(2 + 2 * lv) for lv in range(len(HG_LEVELS))]
        k_lv = [k * blk(3 + 2 * lv) for lv in range(len(HG_LEVELS))]
        heads = [slice(h * HEAD, (h + 1) * HEAD) for h in range(NH)]
        atts = [None] * NH
        for lv in range(len(HG_LEVELS)):
            for h, sl in enumerate(heads):
                term = _dot_nt(q_lv[lv][:, sl], k_lv[lv][:, sl]) * mask_ref[d, lv]
                atts[h] = term if atts[h] is None else atts[h] + term
        o_d = o_d + jnp.concatenate([_dot(atts[h], v[:, sl]) for h, sl in enumerate(heads)], axis=1)
        o_sum = o_d if o_sum is None else o_sum + o_d
        for c in range(ncl):
            rows = slice(c * CH, (c + 1) * CH)
            kv = _dot_tn(kd[rows], v[rows])
            tot = _sum01_r(logf[rows], ones, 3, _dot_tn)
            kv_ref[0, c] = jnp.concatenate([kv[h * HEAD:(h + 1) * HEAD, h * HEAD:(h + 1) * HEAD] for h in range(NH)], axis=1)
            ec_ref[0, c] = jnp.concatenate([jnp.exp(tot[h * HEAD:(h + 1) * HEAD]) for h in range(NH)], axis=1)
    oi_ref[0] = o_sum


def _hgrn_local(ph, lbp, mall, masks, ind_sum, ones_c):
    b, t, _ = ph.shape
    ncl = TT // CH
    full = lambda a: pl.BlockSpec(a.shape, lambda bi, i: (0,) * a.ndim)
    tile = pl.BlockSpec((1, TT, WM), lambda bi, i: (bi, i, 0))
    st_spec = pl.BlockSpec((1, ncl, HEAD, WM), lambda bi, i: (bi, i, 0, 0))
    st_shape = jax.ShapeDtypeStruct((b, t // CH, HEAD, WM), F32)
    return pl.pallas_call(
        _hgrn_local_kernel,
        grid=(b, t // TT),
        in_specs=[pl.BlockSpec((1, TT, W_PH), lambda bi, i: (bi, i, 0)),
                  full(lbp), full(mall), full(masks), full(ind_sum), full(ones_c)],
        out_specs=[tile, tile, tile, st_spec, st_spec, st_spec, st_spec],
        out_shape=[jax.ShapeDtypeStruct((b, t, WM), F32), jax.ShapeDtypeStruct((b, t, WM), BF16),
                   jax.ShapeDtypeStruct((b, t, WM), BF16), st_shape, st_shape, st_shape, st_shape],
        compiler_params=_cp("parallel", "parallel"),
        name="hgrn_local",
    )(ph, lbp, mall, masks, ind_sum, ones_c)


def _hgrn_scan_kernel(qgf_ref, qgb_ref, kvf_ref, kvb_ref, ecf_ref, ecb_ref, of_ref, ob_ref, s_ref):
    @pl.when(pl.program_id(0) == 0)
    def _():
        s_ref[...] = jnp.zeros_like(s_ref)

    dirs = ((qgf_ref, kvf_ref, ecf_ref, of_ref), (qgb_ref, kvb_ref, ecb_ref, ob_ref))
    for b in range(qgf_ref.shape[0]):
        for d, (qg_ref, kv_ref, ec_ref, o_ref) in enumerate(dirs):
            s = s_ref[b, d]
            sb = s.astype(BF16)
            qg = qg_ref[b]
            o_ref[b] = jnp.concatenate(
                [_dot(qg[:, h * HEAD:(h + 1) * HEAD], sb[:, h * HEAD:(h + 1) * HEAD]) for h in range(NH)], axis=1)
            s_ref[b, d] = s * ec_ref[b, 0] + kv_ref[b, 0]


def _hgrn_scan(qgf, qgb, kvf, kvb, ecf, ecb, n_ctx):
    b, t, _ = qgf.shape
    nc = t // CH
    fwd3 = lambda n: (0, n, 0)
    bwd3 = lambda n: (0, _rev_index(n, n_ctx, nc), 0)
    fwd4 = lambda n: (0, n, 0, 0)
    bwd4 = lambda n: (0, _rev_index(n, n_ctx, nc), 0, 0)
    tok = lambda im: pl.BlockSpec((b, CH, WM), im)
    st = lambda im: pl.BlockSpec((b, 1, HEAD, WM), im)
    return pl.pallas_call(
        _hgrn_scan_kernel,
        grid=(nc,),
        in_specs=[tok(fwd3), tok(bwd3), st(fwd4), st(bwd4), st(fwd4), st(bwd4)],
        out_specs=[tok(fwd3), tok(bwd3)],
        out_shape=[jax.ShapeDtypeStruct((b, t, WM), F32)] * 2,
        scratch_shapes=[pltpu.VMEM((b, 2, HEAD, WM), F32)],
        compiler_params=_cp("arbitrary"),
        name="hgrn_scan",
    )(qgf, qgb, kvf, kvb, ecf, ecb)


def _ret_direction(d, pr, cos, sin, dec_ref, qd, kd, cdec_ref, s_ref):
    q = (pr[:, :WM] * cos + pr[:, 4 * WM:5 * WM] * sin)
    k = (pr[:, WM:2 * WM] * cos + pr[:, 5 * WM:6 * WM] * sin) * HEAD ** -0.5
    v = pr[:, 2 * WM:3 * WM]
    q_in = q * qd
    k_in = k * kd
    outs = []
    for h in range(NH):
        sl = slice(h * HEAD, (h + 1) * HEAD)
        s = s_ref[d, h]
        att = _dot_nt(q[:, sl], k[:, sl]) * dec_ref[d, h]
        outs.append(_dot(att, v[:, sl]) + _dot(q_in[:, sl], s))
        s_ref[d, h] = s * cdec_ref[d, h] + _dot_tn(k_in[:, sl], v[:, sl])
    return jnp.concatenate(outs, axis=1)


def _ret_scan_kernel(cdec_ref, pf_ref, pb_ref, cf_ref, sf_ref, cb_ref, sb_ref, dec_ref, qd_ref, kd_ref,
                     of_ref, ob_ref, s_ref):
    @pl.when(pl.program_id(1) == 0)
    def _():
        s_ref[...] = jnp.zeros_like(s_ref)

    of_ref[0] = _ret_direction(0, pf_ref[0], cf_ref[...], sf_ref[...], dec_ref, qd_ref[0], kd_ref[0], cdec_ref, s_ref)
    ob_ref[0] = _ret_direction(1, pb_ref[0], cb_ref[...], sb_ref[...], dec_ref, qd_ref[1], kd_ref[1], cdec_ref, s_ref)


def _ret_scan(pr, cos_t, sin_t, dec, qdec, kdec, cdec):
    b, t, _ = pr.shape
    nt = t // TT
    fwd3 = lambda bi, n: (bi, n, 0)
    bwd3 = lambda bi, n: (bi, _rev_index(n, 1, nt), 0)
    fwd2 = lambda bi, n: (n, 0)
    bwd2 = lambda bi, n: (_rev_index(n, 1, nt), 0)
    return pl.pallas_call(
        _ret_scan_kernel,
        grid=(b, nt),
        in_specs=[pl.BlockSpec(memory_space=pltpu.SMEM),
                  pl.BlockSpec((1, TT, W_PR), fwd3), pl.BlockSpec((1, TT, W_PR), bwd3),
                  pl.BlockSpec((TT, WM), fwd2), pl.BlockSpec((TT, WM), fwd2),
                  pl.BlockSpec((TT, WM), bwd2), pl.BlockSpec((TT, WM), bwd2),
                  pl.BlockSpec(dec.shape, lambda bi, n: (0, 0, 0, 0)),
                  pl.BlockSpec(qdec.shape, lambda bi, n: (0, 0, 0)),
                  pl.BlockSpec(kdec.shape, lambda bi, n: (0, 0, 0))],
        out_specs=[pl.BlockSpec((1, TT, WM), fwd3), pl.BlockSpec((1, TT, WM), bwd3)],
        out_shape=[jax.ShapeDtypeStruct((b, t, WM), F32)] * 2,
        scratch_shapes=[pltpu.VMEM((2, NH, HEAD, HEAD), F32)],
        compiler_params=_cp("parallel", "arbitrary"),
        name="ret_scan",
    )(cdec, pr, pr, cos_t, sin_t, cos_t, sin_t, dec, qdec, kdec)


def _s5_kernel(u_ref, wz_ref, wy_ref, a1_ref, a2_ref, y_ref, z_ref, hp_ref, *, n_ctx):
    nch = u_ref.shape[2]
    for g in range(S5GB):
        z_ref[g] = _dot_pieces(_pieces(u_ref[0, g], 2), [wz_ref[0, g], wz_ref[1, g]])
    a1 = a1_ref[...]
    a2 = a2_ref[...]

    def step(s, hs):
        tiles = (s, _rev_index(s, n_ctx // 8, nch // 8))
        new = []
        for g in range(S5GB):
            for d in range(2):
                h, hx = hs[2 * g + d]
                r0 = pl.multiple_of(tiles[d] * 8, 8)
                lanes = slice(128 * d, 128 * (d + 1))
                z = z_ref[g, pl.ds(r0, 8), lanes]
                zx = pltpu.roll(z, S5P, 1)
                c1, c2 = a1[g, d:d + 1], a2[g, d:d + 1]
                entering = [None] * 8
                for j in (range(8) if d == 0 else range(7, -1, -1)):
                    entering[j] = h
                    h, hx = c1 * h + c2 * hx + z[j:j + 1], c1 * hx - c2 * h + zx[j:j + 1]
                hp_ref[g, pl.ds(r0, 8), lanes] = jnp.concatenate(entering, axis=0)
                new.append((h, hx))
        return tuple(new)

    zero = jnp.zeros((1, 128), F32)
    lax.fori_loop(0, nch // 8, step, tuple((zero, zero) for _ in range(2 * S5GB)))
    for g in range(S5GB):
        lhs = jnp.concatenate([u_ref[0, g], hp_ref[g]], axis=1)
        y_ref[0, g] = _dot_pieces(_pieces(lhs, 2), [wy_ref[0, g], wy_ref[1, g]])


def _s5_scan(u4, wz, wy, a1, a2, n_ctx):
    b, g, nch, w = u4.shape
    return pl.pallas_call(
        functools.partial(_s5_kernel, n_ctx=n_ctx),
        grid=(b, g // S5GB),
        in_specs=[pl.BlockSpec((1, S5GB, nch, w), lambda bi, gi: (bi, gi, 0, 0)),
                  pl.BlockSpec((2, S5GB, w, w), lambda bi, gi: (0, gi, 0, 0)),
                  pl.BlockSpec((2, S5GB, 2 * w, w), lambda bi, gi: (0, gi, 0, 0)),
                  pl.BlockSpec((S5GB, 2, 128), lambda bi, gi: (gi, 0, 0)),
                  pl.BlockSpec((S5GB, 2, 128), lambda bi, gi: (gi, 0, 0))],
        out_specs=pl.BlockSpec((1, S5GB, nch, w), lambda bi, gi: (bi, gi, 0, 0)),
        out_shape=jax.ShapeDtypeStruct(u4.shape, F32),
        scratch_shapes=[pltpu.VMEM((S5GB, nch, w), F32), pltpu.VMEM((S5GB, nch, w), F32)],
        compiler_params=_cp("parallel", "parallel"),
        name="s5_scan",
    )(u4, wz, wy, a1, a2)


def _s5_weights(a_re, a_im, log_step, b_re, b_im, c_re, c_im):
    step = jnp.exp(log_step)[..., None]
    e_re, e_im = a_re * step, a_im * step
    def lam_pow(n):
        n = n[..., None, None, None] if n.ndim else n
        mag = jnp.exp(e_re * n)
        return mag * jnp.cos(e_im * n), mag * jnp.sin(e_im * n)
    l1r, l1i = lam_pow(jnp.asarray(1.0, F32))
    den = a_re * a_re + a_im * a_im
    fr = ((l1r - 1.0) * a_re + l1i * a_im) / den
    fi = (l1i * a_re - (l1r - 1.0) * a_im) / den
    bbr = fr[..., None] * b_re - fi[..., None] * b_im
    bbi = fr[..., None] * b_im + fi[..., None] * b_re
    j = jnp.arange(S5C, dtype=F32)
    es = functools.partial(jnp.einsum, precision=HI)

    def build(d):
        cr, ci = c_re[d], c_im[d]
        br, bi = bbr[d], bbi[d]
        sel = lambda x: x[:, d] if x.ndim == 4 else x
        pr, pi = lam_pow(jnp.arange(S5C + 1, dtype=F32))
        pr, pi = pr[:, d], pi[:, d]
        cl_r = cr[None] * pr[:, :, None, :] - ci[None] * pi[:, :, None, :]
        cl_i = cr[None] * pi[:, :, None, :] + ci[None] * pr[:, :, None, :]
        kk = es('ngop,gpi->ngoi', cl_r[:S5C], br) - es('ngop,gpi->ngoi', cl_i[:S5C], bi)
        ji, jo = jnp.arange(S5C)[:, None], jnp.arange(S5C)[None, :]
        lag = (jo - ji) if d == 0 else (ji - jo)
        kt = kk[jnp.clip(lag, 0, S5C - 1)]
        kt = jnp.where((lag >= 0)[:, :, None, None, None], kt, 0.0)
        toep = kt.transpose(2, 0, 4, 1, 3).reshape(S5G, S5C * 16, S5C * 16)
        m_idx = (jnp.arange(S5C) + 1) if d == 0 else (S5C - jnp.arange(S5C))
        wo_r = cl_r[m_idx].transpose(1, 3, 0, 2).reshape(S5G, S5P, S5C * 16)
        wo_i = -cl_i[m_idx].transpose(1, 3, 0, 2).reshape(S5G, S5P, S5C * 16)
        wout = jnp.concatenate([wo_r, wo_i], axis=1)
        e_idx = (S5C - 1 - jnp.arange(S5C)) if d == 0 else jnp.arange(S5C)
        lr, li = pr[e_idx], pi[e_idx]
        wi_r = (lr[..., None] * br[None] - li[..., None] * bi[None])
        wi_i = (lr[..., None] * bi[None] + li[..., None] * br[None])
        win = jnp.concatenate([wi_r.transpose(1, 0, 3, 2).reshape(S5G, S5C * 16, S5P),
                               wi_i.transpose(1, 0, 3, 2).reshape(S5G, S5C * 16, S5P)], axis=2)
        ar, ai = pr[S5C], pi[S5C]
        a1 = jnp.concatenate([ar, ar], axis=1)
        a2 = jnp.concatenate([-ai, ai], axis=1)
        return toep, wout, win, a1, a2

    tf, of, wf, a1f, a2f = build(0)
    tb, ob, wb, a1b, a2b = build(1)
    wz = jnp.concatenate([wf, wb], axis=2)
    wy = jnp.concatenate([tf + tb, of, ob], axis=1)
    split = lambda w: jnp.stack([w.astype(BF16), (w - w.astype(BF16).astype(F32)).astype(BF16)])
    return split(wz), split(wy), jnp.stack([a1f, a1b], axis=1), jnp.stack([a2f, a2b], axis=1)


def _gelu_tanh(x):
    return 0.5 * x * (1.0 + jnp.tanh(math.sqrt(2.0 / math.pi) * (x + 0.044715 * x * x * x)))


def _outproj_kernel(x_ref, mod_ref, gf_ref, gb_ref, gz_ref, ys_ref, us_ref, hi_ref, hf_ref, hb_ref, hg_ref,
                    rf_ref, rb_ref, rg_ref, vec_ref, glu_ref, ind_ref, w_ref, ln_ref, o_ref, *, alpha):
    ind = ind_ref[...]
    vec = vec_ref[...]

    def head_rms(o):
        return o * lax.rsqrt(_sum01_r(o * o, ind, 2) * (1.0 / HEAD) + RMS_EPS)

    m_gdn = head_rms(gf_ref[0] + gb_ref[0]) * vec[0:1] * _silu(gz_ref[0])
    u = us_ref[0]
    ys = _gelu_tanh(ys_ref[0] + vec[2:3] * u)
    m_s5 = ys * _sigmoid(_dot(ys, glu_ref[...]) + vec[3:4])
    m_hg = head_rms(hi_ref[0] + hf_ref[0] + hb_ref[0]) * vec[1:2] * _silu(hg_ref[0])
    m_rt = head_rms(rf_ref[0] + rb_ref[0]) * _silu(rg_ref[0])
    acc = None
    for k, mk in enumerate((m_gdn, m_s5, m_hg, m_rt)):
        part = _dot(mk.astype(BF16), w_ref[k * WM:(k + 1) * WM, :])
        acc = part if acc is None else acc + part
    m = mod_ref[0, 0]
    ln = ln_ref[...]
    o_ref[0] = _layer_norm(alpha * x_ref[0] + m[2:3] * acc, ln[0:1], ln[1:2])


def _outproj(xs, mod, g_of, g_ob, pg, ys, ps, h_oi, h_of, h_ob, ph, r_of, r_ob, pr, vec, glu_w, ind_sum, w_out, ln,
             alpha):
    b, t, d = xs.shape
    tile = lambda c: pl.BlockSpec((1, TT, WM), lambda bi, i, c=c: (bi, i, c))
    full = lambda a: pl.BlockSpec(a.shape, lambda bi, i: (0,) * a.ndim)
    return pl.pallas_call(
        functools.partial(_outproj_kernel, alpha=alpha),
        grid=(b, t // TT),
        in_specs=[pl.BlockSpec((1, TT, d), lambda bi, i: (bi, i, 0)),
                  pl.BlockSpec((1, 1, 6, d), lambda bi, i: (bi, jnp.minimum(i, 1), 0, 0)),
                  tile(0), tile(0), tile(3), tile(0), tile(0), tile(0), tile(0), tile(0), tile(4),
                  tile(0), tile(0), tile(3),
                  full(vec), full(glu_w), full(ind_sum), full(w_out), full(ln)],
        out_specs=pl.BlockSpec((1, TT, d), lambda bi, i: (bi, i, 0)),
        out_shape=jax.ShapeDtypeStruct(xs.shape, F32),
        compiler_params=_cp("parallel", "parallel"),
        name="outproj",
    )(xs, mod, g_of, g_ob, pg, ys, ps, h_oi, h_of, h_ob, ph, r_of, r_ob, pr, vec, glu_w, ind_sum, w_out, ln)


def _ffn_kernel(x_ref, mod_ref, w1_ref, w3_ref, w2_ref, ln_ref, o_ref, *, alpha):
    m = mod_ref[0, 0]
    x = x_ref[0]
    h = (x * (1.0 + m[4:5]) + m[3:4]).astype(BF16)
    act = (_silu(_dot(h, w1_ref[...])) * _dot(h, w3_ref[...])).astype(BF16)
    y = _dot(act, w2_ref[...])
    ln = ln_ref[...]
    o_ref[0] = _layer_norm(alpha * x + m[5:6] * y, ln[0:1], ln[1:2])


def _ffn(xs, mod, w1, w3, w2, ln, alpha):
    b, t, d = xs.shape
    full = lambda a: pl.BlockSpec(a.shape, lambda bi, i: (0,) * a.ndim)
    return pl.pallas_call(
        functools.partial(_ffn_kernel, alpha=alpha),
        grid=(b, t // TT),
        in_specs=[pl.BlockSpec((1, TT, d), lambda bi, i: (bi, i, 0)),
                  pl.BlockSpec((1, 1, 6, d), lambda bi, i: (bi, jnp.minimum(i, 1), 0, 0)),
                  full(w1), full(w3), full(w2), full(ln)],
        out_specs=pl.BlockSpec((1, TT, d), lambda bi, i: (bi, i, 0)),
        out_shape=jax.ShapeDtypeStruct(xs.shape, F32),
        compiler_params=_cp("parallel", "parallel"),
        name="ffn_dense",
    )(xs, mod, w1, w3, w2, ln)


def _router_kernel(x_ref, mod_ref, r_ref, h_ref, rt_ref):
    m = mod_ref[0, 0]
    h = x_ref[0] * (1.0 + m[4:5]) + m[3:4]
    h_ref[0] = h
    lane = lax.broadcasted_iota(jnp.int32, (TT, 128), 1)
    logits = jnp.where(lane < N_EXP, _dot(h, r_ref[...], HI), -jnp.inf)
    m1 = jnp.max(logits, axis=-1, keepdims=True)
    i1 = jnp.min(jnp.where(logits == m1, lane, 128), axis=-1, keepdims=True)
    rest = jnp.where(lane == i1, -jnp.inf, logits)
    m2 = jnp.max(rest, axis=-1, keepdims=True)
    i2 = jnp.min(jnp.where(rest == m2, lane, 128), axis=-1, keepdims=True)
    e = jnp.exp(m2 - m1)
    g1 = 1.0 / (1.0 + e)
    g2 = e / (1.0 + e)
    rt_ref[0] = jnp.where(lane == 0, i1.astype(F32),
                          jnp.where(lane == 1, i2.astype(F32),
                                    jnp.where(lane == 2, g1, jnp.where(lane == 3, g2, 0.0))))


def _router(xs, mod, router_pad):
    b, t, d = xs.shape
    return pl.pallas_call(
        _router_kernel,
        grid=(b, t // TT),
        in_specs=[pl.BlockSpec((1, TT, d), lambda bi, i: (bi, i, 0)),
                  pl.BlockSpec((1, 1, 6, d), lambda bi, i: (bi, jnp.minimum(i, 1), 0, 0)),
                  pl.BlockSpec(router_pad.shape, lambda bi, i: (0, 0))],
        out_specs=[pl.BlockSpec((1, TT, d), lambda bi, i: (bi, i, 0)),
                   pl.BlockSpec((1, TT, 128), lambda bi, i: (bi, i, 0))],
        out_shape=[jax.ShapeDtypeStruct(xs.shape, F32), jax.ShapeDtypeStruct((b, t, 128), F32)],
        compiler_params=_cp("parallel", "parallel"),
        name="moe_router",
    )(xs, mod, router_pad)


def _row_copy(src_hbm, row, dst_ref, slot, sem):
    return pltpu.make_async_copy(src_hbm.at[pl.ds(row, 1), :], dst_ref.at[pl.ds(slot, 1), :], sem)


def _dispatch_kernel(dest_ref, h_ref, init_hbm, o_hbm, sem, *, tiles_per_batch):
    del init_hbm
    tile = pl.program_id(0) * tiles_per_batch + pl.program_id(1)
    base = tile * (2 * TT)

    def start(r, c):
        t = lax.rem(r, TT)
        pltpu.make_async_copy(h_ref.at[0, pl.ds(t, 1), :], o_hbm.at[pl.ds(dest_ref[base + r], 1), :], sem).start()
        return c

    lax.fori_loop(0, 2 * TT, start, 0, unroll=8)
    for _ in range(2):
        pltpu.make_async_copy(h_ref.at[0], o_hbm.at[pl.ds(0, TT), :], sem).wait()


def _dispatch_rows(h, dest_tiles, n_rows):
    b, t, d = h.shape
    return pl.pallas_call(
        functools.partial(_dispatch_kernel, tiles_per_batch=t // TT),
        grid_spec=pltpu.PrefetchScalarGridSpec(
            num_scalar_prefetch=1,
            grid=(b, t // TT),
            in_specs=[pl.BlockSpec((1, TT, d), lambda bi, i, dr: (bi, i, 0)),
                      pl.BlockSpec(memory_space=pl.ANY)],
            out_specs=pl.BlockSpec(memory_space=pl.ANY),
            scratch_shapes=[pltpu.SemaphoreType.DMA(())]),
        out_shape=jax.ShapeDtypeStruct((n_rows, d), h.dtype),
        input_output_aliases={2: 0},
        compiler_params=_cp("arbitrary", "arbitrary"),
        name="moe_dispatch",
    )(dest_tiles, h, jnp.zeros((n_rows, d), h.dtype))


def _experts_kernel(be_ref, x_ref, w1_ref, w3_ref, w2_ref, y_ref):
    f = pl.program_id(1)
    x = x_ref[...].astype(BF16)
    act = (_silu(_dot(x, w1_ref[0])) * _dot(x, w3_ref[0])).astype(BF16)
    y = _dot(act, w2_ref[0])

    @pl.when(f == 0)
    def _():
        y_ref[...] = y

    @pl.when(f != 0)
    def _():
        y_ref[...] += y


def _experts(xs_sorted, block_expert, w1, w3, w2):
    n_rows, d = xs_sorted.shape
    ff = w1.shape[2]
    nf = 2
    tf = ff // nf
    return pl.pallas_call(
        _experts_kernel,
        grid_spec=pltpu.PrefetchScalarGridSpec(
            num_scalar_prefetch=1,
            grid=(n_rows // MOE_RB, nf),
            in_specs=[pl.BlockSpec((MOE_RB, d), lambda j, f, be: (j, 0)),
                      pl.BlockSpec((1, d, tf), lambda j, f, be: (be[j], 0, f)),
                      pl.BlockSpec((1, d, tf), lambda j, f, be: (be[j], 0, f)),
                      pl.BlockSpec((1, tf, d), lambda j, f, be: (be[j], f, 0))],
            out_specs=pl.BlockSpec((MOE_RB, d), lambda j, f, be: (j, 0))),
        out_shape=jax.ShapeDtypeStruct((n_rows, d), F32),
        compiler_params=_cp("parallel", "arbitrary"),
        name="moe_experts",
    )(block_expert, xs_sorted, w1, w3, w2)


def _combine_kernel(dest_ref, x_ref, mod_ref, rt_ref, ln_ref, y_hbm, o_ref, buf, sem, *, alpha, tiles_per_batch):
    tile = pl.program_id(0) * tiles_per_batch + pl.program_id(1)
    base = tile * (2 * TT)

    def start(r, c):
        _row_copy(y_hbm, dest_ref[base + r], buf, r, sem).start()
        return c

    lax.fori_loop(0, 2 * TT, start, 0, unroll=8)
    pltpu.make_async_copy(y_hbm.at[pl.ds(0, 2 * TT), :], buf, sem).wait()
    rt = rt_ref[0]
    y = rt[:, 2:3] * buf[0:TT, :] + rt[:, 3:4] * buf[TT:2 * TT, :]
    m = mod_ref[0, 0]
    ln = ln_ref[...]
    o_ref[0] = _layer_norm(alpha * x_ref[0] + m[5:6] * y, ln[0:1], ln[1:2])


def _combine(xs, mod, rt, ln, ys_sorted, dest_tiles, alpha):
    b, t, d = xs.shape
    return pl.pallas_call(
        functools.partial(_combine_kernel, alpha=alpha, tiles_per_batch=t // TT),
        grid_spec=pltpu.PrefetchScalarGridSpec(
            num_scalar_prefetch=1,
            grid=(b, t // TT),
            in_specs=[pl.BlockSpec((1, TT, d), lambda bi, i, dr: (bi, i, 0)),
                      pl.BlockSpec((1, 1, 6, d), lambda bi, i, dr: (bi, jnp.minimum(i, 1), 0, 0)),
                      pl.BlockSpec((1, TT, 128), lambda bi, i, dr: (bi, i, 0)),
                      pl.BlockSpec(ln.shape, lambda bi, i, dr: (0, 0)),
                      pl.BlockSpec(memory_space=pl.ANY)],
            out_specs=pl.BlockSpec((1, TT, d), lambda bi, i, dr: (bi, i, 0)),
            scratch_shapes=[pltpu.VMEM((2 * TT, d), F32), pltpu.SemaphoreType.DMA(())]),
        out_shape=jax.ShapeDtypeStruct(xs.shape, F32),
        compiler_params=_cp("arbitrary", "arbitrary"),
        name="moe_combine",
    )(dest_tiles, xs, mod, rt, ln, ys_sorted)


def _moe(xs, mod, router_pad, w1, w3, w2, ln, alpha):
    b, t, d = xs.shape
    n_tok = b * t
    h, rt = _router(xs, mod, router_pad)
    e_idx = rt[..., 0:2].astype(jnp.int32).reshape(n_tok * 2)
    onehot = (e_idx[:, None] == jnp.arange(N_EXP, dtype=jnp.int32)[None, :]).astype(jnp.int32)
    csum = jnp.cumsum(onehot, axis=0)
    counts = csum[-1]
    padded = (counts + MOE_RB - 1) // MOE_RB * MOE_RB
    pad_end = jnp.cumsum(padded)
    pad_start = pad_end - padded
    dest = jnp.sum(onehot * (csum - 1 + pad_start[None, :]), axis=1)
    n_blocks = -(-(n_tok * 2) // MOE_RB) + N_EXP
    n_rows = n_blocks * MOE_RB
    block_expert = jnp.minimum(
        jnp.sum(jnp.arange(n_blocks, dtype=jnp.int32)[:, None] * MOE_RB >= pad_end[None, :], axis=1),
        N_EXP - 1).astype(jnp.int32)
    dest_tiles = dest.reshape(n_tok // TT, TT, 2).transpose(0, 2, 1).reshape(n_tok * 2)
    xs_sorted = _dispatch_rows(h, dest_tiles, n_rows)
    ys_sorted = _experts(xs_sorted, block_expert, w1, w3, w2)
    return _combine(xs, mod, rt, ln, ys_sorted, dest_tiles, alpha)


def _ret_constants(decay_param):
    log_gamma = -jnp.exp(decay_param)
    idx = jnp.arange(TT, dtype=F32)
    diff = idx[:, None] - idx[None, :]
    lg = log_gamma[:, :, None, None]
    dec_f = jnp.exp(jnp.where(diff >= 0, diff * lg[0], -jnp.inf))
    dec_b = jnp.exp(jnp.where(diff <= 0, -diff * lg[1], -jnp.inf))
    dec = jnp.stack([dec_f, dec_b])
    rep = lambda a: jnp.repeat(a, HEAD, axis=-1)
    qdec = jnp.stack([rep(jnp.exp((idx[:, None] + 1.0) * log_gamma[0][None, :])),
                      rep(jnp.exp((TT - idx[:, None]) * log_gamma[1][None, :]))])
    kdec = jnp.stack([rep(jnp.exp((TT - 1.0 - idx[:, None]) * log_gamma[0][None, :])),
                      rep(jnp.exp(idx[:, None] * log_gamma[1][None, :]))])
    cdec = jnp.exp(TT * log_gamma)
    return dec, qdec, kdec, cdec


def _rotary_tables(n_ctx_tok, n_lat, grid_w):
    rows = jnp.repeat(jnp.arange(n_lat // grid_w, dtype=F32), grid_w)
    cols = jnp.tile(jnp.arange(grid_w, dtype=F32), n_lat // grid_w)
    quarter = HEAD // 4
    inv_freq = ROPE_BASE ** (-jnp.arange(quarter, dtype=F32) / quarter)
    ang = jnp.concatenate([rows[:, None] * inv_freq, cols[:, None] * inv_freq], axis=-1)
    cos, sin = jnp.cos(ang), jnp.sin(ang)
    cos_h = jnp.concatenate([cos, cos], axis=-1)
    sin_h = jnp.concatenate([-sin, sin], axis=-1)
    cos_t = jnp.concatenate([jnp.ones((n_ctx_tok, HEAD), F32), cos_h], axis=0)
    sin_t = jnp.concatenate([jnp.zeros((n_ctx_tok, HEAD), F32), sin_h], axis=0)
    return jnp.tile(cos_t, (1, NH)), jnp.tile(sin_t, (1, NH))


def _swap_halves_cols(w):
    dm = w.shape[0]
    return w.reshape(dm, NH, 2, HEAD // 2)[:, :, ::-1, :].reshape(dm, WM)


def _prep_w_in(w):
    dm = w.shape[0]
    r0 = 3344 - 768
    rq, rk = w[:, r0:r0 + WM], w[:, r0 + WM:r0 + 2 * WM]
    return jnp.concatenate([w[:, :1040], jnp.zeros((dm, W_AB - 16), w.dtype), w[:, 1040:],
                            _swap_halves_cols(rq), _swap_halves_cols(rk)], axis=1).astype(BF16)


def kernel(x, c, ctx, c_ctx, ada_w, ada_b, w_in, w_out, ln_g, ln_b, gdn_conv_w, gdn_a_log, gdn_dt_bias, gdn_norm_w, s5_a_re, s5_a_im, s5_log_step, s5_b_re, s5_b_im, s5_c_re, s5_c_im, s5_d, s5_glu_w, s5_glu_b, hgrn_lower_bounds, hgrn_norm_w, ret_decay, ffn_w1, ffn_w3, ffn_w2, moe_router, moe_w1, moe_w3, moe_w2):
    bsz, n_lat, d = x.shape
    n_ctx_tok = ctx.shape[1]
    depth = ada_w.shape[0]
    grid_w = 64
    assert n_ctx_tok == TT and n_lat % TT == 0 and bsz <= 7
    t = n_ctx_tok + n_lat
    alpha = (2.0 * depth) ** 0.25

    xs = jnp.concatenate([ctx, x], axis=1)
    cs = jnp.concatenate([c, c_ctx[None, :], jnp.zeros((8 - bsz - 1, d), F32)], axis=0)
    mod_all = _ada_mod(cs, ada_w, ada_b)
    lat_mod = mod_all[:, :bsz].reshape(depth, bsz, 6, d)
    ctx_mod = jnp.broadcast_to(mod_all[:, bsz].reshape(depth, 1, 6, d), (depth, bsz, 6, d))
    mod_tab = jnp.stack([ctx_mod, lat_mod], axis=2)

    lb_all = jnp.cumsum(jax.nn.softmax(hgrn_lower_bounds.astype(F32), axis=0), axis=0)
    lb_all = lb_all - lb_all[0]
    ind_sum = jnp.asarray(np.kron(np.eye(NH), np.ones((HEAD, HEAD))), BF16)
    chunks_eye = np.eye(TT // CH)
    tri_bd = jnp.asarray(np.kron(chunks_eye, np.tril(np.ones((CH, CH)))), BF16)
    ones_bd = jnp.asarray(np.kron(chunks_eye, np.ones((CH, CH))), BF16)
    ones_c = jnp.ones((CH, HEAD), BF16)
    mall_np, masks_np = _hgrn_constants()
    mall, masks = jnp.asarray(mall_np, BF16), jnp.asarray(masks_np)
    cos_t, sin_t = _rotary_tables(n_ctx_tok, n_lat, grid_w)
    nch = t // S5C

    for layer in range(depth):
        mod = mod_tab[layer]
        pg, pab, ps, ph, pr = _inproj(xs, mod, _prep_w_in(w_in[layer]))

        pabt = pab[..., :16].transpose(0, 2, 1)
        g_loc = _gdn_local(pg, pab, pabt, gdn_conv_w[layer], ind_sum, -jnp.exp(gdn_a_log[layer]),
                           gdn_dt_bias[layer], tri_bd, ones_bd)
        g_of, g_ob = _gdn_scan(g_loc, ind_sum, n_ctx_tok // CH)

        wz, wy, a1, a2 = _s5_weights(s5_a_re[layer], s5_a_im[layer], s5_log_step[layer], s5_b_re[layer],
                                     s5_b_im[layer], s5_c_re[layer], s5_c_im[layer])
        u4 = ps.reshape(bsz, nch, S5C, S5G, 16).transpose(0, 3, 1, 2, 4).reshape(bsz, S5G, nch, S5C * 16)
        y4 = _s5_scan(u4, wz, wy, a1, a2, n_ctx_tok // S5C)
        ys = y4.reshape(bsz, S5G, nch, S5C, 16).transpose(0, 2, 3, 1, 4).reshape(bsz, t, WM)

        lb = lb_all[layer][None, :]
        lbp = jnp.concatenate([jnp.log(lb), jnp.log1p(-lb), 1.0 - lb, jnp.zeros((5, WM), F32)], axis=0)
        h_oi, h_qgf, h_qgb, h_kvf, h_kvb, h_ecf, h_ecb = _hgrn_local(ph, lbp, mall, masks, ind_sum, ones_c)
        h_of, h_ob = _hgrn_scan(h_qgf, h_qgb, h_kvf, h_kvb, h_ecf, h_ecb, n_ctx_tok // CH)

        dec, qdec, kdec, cdec = _ret_constants(ret_decay[layer])
        r_of, r_ob = _ret_scan(pr, cos_t, sin_t, dec, qdec, kdec, cdec)

        vec = jnp.concatenate([jnp.tile(gdn_norm_w[layer], NH)[None], jnp.tile(hgrn_norm_w[layer], NH)[None],
                               s5_d[layer][None], s5_glu_b[layer][None], jnp.zeros((4, WM), F32)], axis=0)
        xs = _outproj(xs, mod, g_of, g_ob, pg, ys, ps, h_oi, h_of, h_ob, ph, r_of, r_ob, pr, vec, s5_glu_w[layer],
                      ind_sum, w_out[layer].astype(BF16), jnp.stack([ln_g[layer, 0], ln_b[layer, 0]]), alpha)

        j = layer // 2
        ln2 = jnp.stack([ln_g[layer, 1], ln_b[layer, 1]])
        if layer % 2 == 0:
            xs = _ffn(xs, mod, ffn_w1[j].astype(BF16), ffn_w3[j].astype(BF16), ffn_w2[j].astype(BF16), ln2, alpha)
        else:
            router_pad = jnp.concatenate([moe_router[j], jnp.zeros((d, 128 - N_EXP), F32)], axis=1)
            xs = _moe(xs, mod, router_pad, moe_w1[j].astype(BF16), moe_w3[j].astype(BF16),
                      moe_w2[j].astype(BF16), ln2, alpha)
    return xs[:, n_ctx_tok:, :]
```

```python
import functools
import math

import numpy as np
import jax
import jax.numpy as jnp
from jax import lax
from jax.experimental import pallas as pl
from jax.experimental.pallas import tpu as pltpu

F32 = jnp.float32
BF16 = jnp.bfloat16
HI = lax.Precision.HIGHEST

HEAD = 64
NH = 4
WM = NH * HEAD
TT = 256
CH = 64
S5C = 16
S5G = 16
S5P = 64
S5GB = 8
N_EXP = 8
MOE_RB = 512
LN_EPS = 1e-5
RMS_EPS = 1e-6
ROPE_BASE = 10000.0
VMEM_LIMIT = 56 * 1024 * 1024


def _cp(*sem):
    return pltpu.CompilerParams(dimension_semantics=sem, vmem_limit_bytes=VMEM_LIMIT)


def _sigmoid(x):
    return 1.0 / (1.0 + jnp.exp(-x))


def _silu(x):
    return x * _sigmoid(x)


def _softplus(x):
    return jnp.maximum(x, 0.0) + jnp.log1p(jnp.exp(-jnp.abs(x)))


def _dot(a, b, precision=None):
    return jnp.dot(a, b, preferred_element_type=F32, precision=precision)


def _dot_nt(a, b, precision=None):
    return lax.dot_general(a, b, (((1,), (1,)), ((), ())), preferred_element_type=F32, precision=precision)


def _dot_tn(a, b, precision=None):
    return lax.dot_general(a, b, (((0,), (0,)), ((), ())), preferred_element_type=F32, precision=precision)


def _layer_norm(y, g, b):
    mu = jnp.mean(y, axis=-1, keepdims=True)
    yc = y - mu
    var = jnp.mean(yc * yc, axis=-1, keepdims=True)
    return yc * lax.rsqrt(var + LN_EPS) * g + b


def _rev_index(n, n_ctx, n_all):
    return jnp.where(n < n_ctx, n_ctx - 1 - n, n_all + n_ctx - 1 - n)


def _ada_kernel(c_ref, w_ref, b_ref, o_ref):
    o_ref[0] = _dot(_silu(c_ref[...]), w_ref[0], HI) + b_ref[0]


def _ada_mod(cs, ada_w, ada_b):
    depth, d, d6 = ada_w.shape
    tn = 1024
    return pl.pallas_call(
        _ada_kernel,
        grid=(depth, d6 // tn),
        in_specs=[pl.BlockSpec((8, d), lambda l, j: (0, 0)),
                  pl.BlockSpec((1, d, tn), lambda l, j: (l, 0, j)),
                  pl.BlockSpec((1, 1, tn), lambda l, j: (l, 0, j))],
        out_specs=pl.BlockSpec((1, 8, tn), lambda l, j: (l, 0, j)),
        out_shape=jax.ShapeDtypeStruct((depth, 8, d6), F32),
        compiler_params=_cp("parallel", "parallel"),
        name="ada_mod",
    )(cs, ada_w, ada_b.reshape(depth, 1, d6))


W_PG, W_AB, W_S5, W_PH, W_PR = 1024, 128, 256, 1280, 1536
P_OFF = np.cumsum([0, W_PG, W_AB, W_S5, W_PH, W_PR])


def _inproj_kernel(x_ref, mod_ref, w_ref, pg_ref, pab_ref, ps_ref, ph_ref, pr_ref):
    m = mod_ref[0, 0]
    h = (x_ref[0] * (1.0 + m[1:2]) + m[0:1]).astype(BF16)
    for k, o_ref in enumerate((pg_ref, pab_ref, ps_ref, ph_ref, pr_ref)):
        o_ref[0] = _dot(h, w_ref[:, P_OFF[k]:P_OFF[k + 1]])


def _inproj(xs, mod, w):
    b, t, d = xs.shape
    widths = (W_PG, W_AB, W_S5, W_PH, W_PR)
    return pl.pallas_call(
        _inproj_kernel,
        grid=(b, t // TT),
        in_specs=[pl.BlockSpec((1, TT, d), lambda bi, i: (bi, i, 0)),
                  pl.BlockSpec((1, 1, 6, d), lambda bi, i: (bi, jnp.minimum(i, 1), 0, 0)),
                  pl.BlockSpec(w.shape, lambda bi, i: (0, 0))],
        out_specs=[pl.BlockSpec((1, TT, wd), lambda bi, i: (bi, i, 0)) for wd in widths],
        out_shape=[jax.ShapeDtypeStruct((b, t, wd), F32) for wd in widths],
        compiler_params=_cp("parallel", "parallel"),
        name="inproj",
    )(xs, mod, w)


def _pieces(x, n):
    out, r = [], x
    for i in range(n):
        p = r.astype(BF16)
        out.append(p)
        if i + 1 < n:
            r = r - p.astype(F32)
    return out


def _dot_pieces(a_parts, b_parts, dot=_dot):
    n = max(len(a_parts), len(b_parts))
    acc = None
    for i, ap in enumerate(a_parts):
        for j, bp in enumerate(b_parts):
            if i + j < n:
                t = dot(ap, bp)
                acc = t if acc is None else acc + t
    return acc


def _sum01_l(m01, x, n):
    return _dot_pieces([m01], _pieces(x, n))


def _sum01_r(x, m01, n, dot=_dot):
    return _dot_pieces(_pieces(x, n), [m01], dot)


GDN_PIECES = 1
GDN_BASE = 16


def _unit_tri_inverse(a_list, eye, masks):
    n = GDN_PIECES
    ident = jnp.where(eye, 1.0, 0.0)
    ds = [jnp.where(masks[0], a, 0.0) for a in a_list]
    ts = [ident - d for d in ds]
    ps = [_pieces(d, n) for d in ds]
    size = 2
    while size < GDN_BASE:
        ps = [_pieces(_dot_pieces(p, p), n) for p in ps]
        ts = [t + _dot_pieces(_pieces(t, n), p) for t, p in zip(ts, ps)]
        size *= 2
    for off_mask in masks[1:]:
        tps = [_pieces(t, n) for t in ts]
        mids = [_pieces(_dot_pieces(_pieces(jnp.where(off_mask, a, 0.0), n), tp), n) for a, tp in zip(a_list, tps)]
        ts = [t - _dot_pieces(tp, mid) for t, tp, mid in zip(ts, tps, mids)]
    return ts


def _gdn_local_kernel(na_ref, dtb_ref, p_ref, pv_ref, nx_ref, cw_ref, ind_ref, ab_ref, abt_ref, tri_ref, trit_ref,
                      ones_ref, uf_ref, ub_ref, wf_ref, wb_ref, qkf_ref, qkb_ref, qgf_ref, qgb_ref, kdf_ref, kdb_ref,
                      eg_ref):
    i = pl.program_id(1)
    nt = pl.num_programs(1)
    x = p_ref[0]
    prev = jnp.where(i >= 2, pv_ref[0][7:8], 0.0)
    nxt = jnp.where((i >= 1) & (i < nt - 1), nx_ref[0][0:1], 0.0)
    row1 = lax.broadcasted_iota(jnp.int32, (TT, 1), 0)
    xm = jnp.where(row1 == 0, prev, pltpu.roll(x, 1, 0))
    xp = jnp.where(row1 == TT - 1, nxt, pltpu.roll(x, TT - 1, 0))
    cw = cw_ref[...]
    y = _silu(cw[0:1] * xm + cw[1:2] * x + cw[2:3] * xp)
    q, k, v = y[:, :WM], y[:, WM:2 * WM], y[:, 2 * WM:]
    ind = ind_ref[...]
    q = q * lax.rsqrt(_sum01_r(q * q, ind, 2) + RMS_EPS) * HEAD ** -0.5
    k = k * lax.rsqrt(_sum01_r(k * k, ind, 2) + RMS_EPS)

    ab = ab_ref[0]
    abt = abt_ref[0]
    ones_bd = ones_ref[...]
    row = lax.broadcasted_iota(jnp.int32, (TT, TT), 0)
    col = lax.broadcasted_iota(jnp.int32, (TT, TT), 1)
    same = (row // CH) == (col // CH)
    eye = row == col
    in_block = lambda n: (row // n) == (col // n)
    inv_masks, n = [in_block(GDN_BASE)], GDN_BASE
    while n < CH:
        inv_masks.append(in_block(2 * n) & jnp.logical_not(in_block(n)))
        n *= 2
    out_refs =((uf_ref, wf_ref, qkf_ref, qgf_ref, kdf_ref), (ub_ref, wb_ref, qkb_ref, qgb_ref, kdb_ref))
    e_last = []
    for d in range(2):
        incl = same & ((row >= col) if d == 0 else (row <= col))
        strict = same & ((row > col) if d == 0 else (row < col))
        tri_c = tri_ref[...] if d == 0 else trit_ref[...]
        a_col, b_col = ab[:, 4 * d:4 * d + 4], ab[:, 8 + 4 * d:12 + 4 * d]
        a_row = abt[4 * d:4 * d + 4, :]
        g_col = jnp.concatenate([na_ref[d, h] * _softplus(a_col[:, h:h + 1] + dtb_ref[d, h]) for h in range(NH)], axis=1)
        g_row = jnp.concatenate([na_ref[d, h] * _softplus(a_row[h:h + 1, :] + dtb_ref[d, h]) for h in range(NH)], axis=0)
        gc_col = _sum01_l(tri_c, g_col, 3)
        gc_row = _sum01_r(g_row, tri_c, 3, _dot_nt)
        gl_col = _sum01_l(ones_bd, g_col, 3)
        e_last.append(jnp.exp(gl_col))
        a_list, rhs, qks, qgs, kds = [], [], [], [], []
        for h in range(NH):
            sl = slice(h * HEAD, (h + 1) * HEAD)
            qh, kh, vh = q[:, sl], k[:, sl], v[:, sl]
            beta = _sigmoid(b_col[:, h:h + 1])
            gcc = gc_col[:, h:h + 1]
            decay = jnp.exp(jnp.where(incl, gcc - gc_row[h:h + 1, :], -jnp.inf))
            kb = kh * beta
            a_list.append(_dot_nt(kb, kh) * jnp.where(strict, decay, 0.0))
            rhs.append(_pieces(jnp.concatenate([vh * beta, kb * jnp.exp(gcc)], axis=1), GDN_PIECES))
            qk = _dot_nt(qh, kh) * decay
            qks.append(qk[:, 0:CH] + qk[:, CH:2 * CH] + qk[:, 2 * CH:3 * CH] + qk[:, 3 * CH:4 * CH])
            qgs.append(qh * jnp.exp(gcc))
            kds.append(kh * jnp.exp(gl_col[:, h:h + 1] - gcc))
        xs = [_dot_pieces(_pieces(t, GDN_PIECES), r) for t, r in zip(_unit_tri_inverse(a_list, eye, inv_masks), rhs)]
        us = [xx[:, :HEAD] for xx in xs]
        ws = [xx[:, HEAD:] for xx in xs]
        u_ref, w_ref, qk_ref, qg_ref, kd_ref = out_refs[d]
        u_ref[0] = jnp.concatenate(us, axis=1)
        w_ref[0] = jnp.concatenate(ws, axis=1).astype(BF16)
        qk_ref[0] = jnp.concatenate(qks, axis=1).astype(BF16)
        qg_ref[0] = jnp.concatenate(qgs, axis=1).astype(BF16)
        kdt = jnp.concatenate(kds, axis=1).T
        for c in range(TT // CH):
            kd_ref[0, c] = kdt[:, c * CH:(c + 1) * CH].astype(BF16)
    eg_ref[0] =jnp.concatenate(e_last + [jnp.zeros((TT, 128 - 2 * NH), F32)], axis=1)


def _gdn_local(pg, pab, pabt, conv_w, ind_sum, neg_a, dt_bias, tri_bd, ones_bd):
    b, t, _ = pg.shape
    w3 = 3 * WM
    n8 = t // 8
    smem = pl.BlockSpec(memory_space=pltpu.SMEM)
    full = lambda a: pl.BlockSpec(a.shape, lambda bi, i: (0,) * a.ndim)
    tile = pl.BlockSpec((1, TT, WM), lambda bi, i: (bi, i, 0))
    f32o = jax.ShapeDtypeStruct((b, t, WM), F32)
    b16o = jax.ShapeDtypeStruct((b, t, WM), BF16)
    return pl.pallas_call(
        _gdn_local_kernel,
        grid=(b, t // TT),
        in_specs=[smem, smem,
                  pl.BlockSpec((1, TT, w3), lambda bi, i: (bi, i, 0)),
                  pl.BlockSpec((1, 8, w3), lambda bi, i: (bi, jnp.maximum(i * (TT // 8) - 1, 0), 0)),
                  pl.BlockSpec((1, 8, w3), lambda bi, i: (bi, jnp.minimum((i + 1) * (TT // 8), n8 - 1), 0)),
                  pl.BlockSpec((3, w3), lambda bi, i: (0, 0)),
                  full(ind_sum),
                  pl.BlockSpec((1, TT, W_AB), lambda bi, i: (bi, i, 0)),
                  pl.BlockSpec((1, 16, TT), lambda bi, i: (bi, 0, i)),
                  full(tri_bd), full(tri_bd), full(ones_bd)],
        out_specs=[tile] * 8 + [pl.BlockSpec((1, TT // CH, WM, CH), lambda bi, i: (bi, i, 0, 0))] * 2
                  + [pl.BlockSpec((1, TT, 128), lambda bi, i: (bi, i, 0))],
        out_shape=[f32o, f32o] + [b16o] * 6 + [jax.ShapeDtypeStruct((b, t // CH, WM, CH), BF16)] * 2
                  + [jax.ShapeDtypeStruct((b, t, 128), F32)],
        compiler_params=_cp("parallel", "parallel"),
        name="gdn_local",
    )(neg_a, dt_bias, pg, pg, pg, conv_w, ind_sum, pab, pabt, tri_bd, tri_bd.T, ones_bd)


def _gdn_scan_kernel(uf_ref, ub_ref, wf_ref, wb_ref, qkf_ref, qkb_ref, qgf_ref, qgb_ref, kdf_ref, kdb_ref,
                     egf_ref, egb_ref, mask_ref, of_ref, ob_ref, s_ref):
    @pl.when(pl.program_id(0) == 0)
    def _():
        s_ref[...] = jnp.zeros_like(s_ref)

    dirs = ((uf_ref, wf_ref, qkf_ref, qgf_ref, kdf_ref, egf_ref, of_ref),
            (ub_ref, wb_ref, qkb_ref, qgb_ref, kdb_ref, egb_ref, ob_ref))
    mask_b = mask_ref[...]
    mask_f = mask_b.astype(F32)
    chains = [(b, d) + refs for b in range(uf_ref.shape[0]) for d, refs in enumerate(dirs)]
    olds = [s_ref[b, d] for b, d, *_ in chains]
    wss = [_dot(jnp.concatenate([w_ref[b], qg_ref[b]], axis=0), s.astype(BF16))
           for (b, d, u_ref, w_ref, qk_ref, qg_ref, kdt_ref, eg_ref, o_ref), s in zip(chains, olds)]
    vbs = [(c[2][c[0]] - ws[:CH]).astype(BF16) for c, ws in zip(chains, wss)]
    for (b, d, u_ref, w_ref, qk_ref, qg_ref, kdt_ref, eg_ref, o_ref), ws, vb in zip(chains, wss, vbs):
        v_bd = jnp.concatenate([vb] * NH, axis=0) * mask_b
        o_ref[b] = ws[CH:] + _dot(qk_ref[b], v_bd)
    for (b, d, u_ref, w_ref, qk_ref, qg_ref, kdt_ref, eg_ref, o_ref), s, vb in zip(chains, olds, vbs):
        eg = eg_ref[b][0:1, :]
        e_row = jnp.concatenate(
            [jnp.broadcast_to(eg[:, 4 * d + h:4 * d + h + 1], (1, HEAD)) for h in range(NH)], axis=1)
        s_ref[b, d] = s * e_row + _dot(kdt_ref[b, 0], vb) * mask_f


def _gdn_scan(loc, head_mask, n_ctx):
    uf, ub, wf, wb, qkf, qkb, qgf, qgb, kdf, kdb, eg = loc
    b, t, _ = uf.shape
    nc = t // CH
    fwd = lambda n: (0, n, 0)
    bwd = lambda n: (0, _rev_index(n, n_ctx, nc), 0)
    fwd4 = lambda n: (0, n, 0, 0)
    bwd4 = lambda n: (0, _rev_index(n, n_ctx, nc), 0, 0)
    blk = lambda im, w=WM: pl.BlockSpec((b, CH, w), im)
    kdt = lambda im: pl.BlockSpec((b, 1, WM, CH), im)
    return pl.pallas_call(
        _gdn_scan_kernel,
        grid=(nc,),
        in_specs=[blk(fwd), blk(bwd)] * 4 + [kdt(fwd4), kdt(bwd4), blk(fwd, 128), blk(bwd, 128),
                                              pl.BlockSpec(head_mask.shape, lambda n: (0, 0))],
        out_specs=[blk(fwd), blk(bwd)],
        out_shape=[jax.ShapeDtypeStruct((b, t, WM), F32)] * 2,
        scratch_shapes=[pltpu.VMEM((b, 2, WM, WM), F32)],
        compiler_params=_cp("arbitrary"),
        name="gdn_scan",
    )(uf, ub, wf, wb, qkf, qkb, qgf, qgb, kdf, kdb, eg, eg, head_mask)


HG_LEVELS = (32, 16, 8, 4, 2, 1)


def _hgrn_constants():
    idx = np.arange(CH)
    i, t = idx[:, None], idx[None, :]
    blocks = [(t <= i), (t > i)]
    masks = []
    for s in HG_LEVELS:
        m = (idx // (2 * s)) * 2 * s + s
        sec = (idx % (2 * s)) >= s
        mi = m[:, None]
        blocks.append(sec[:, None] & (t >= mi) & (t <= i))
        blocks.append((~sec)[:, None] & (t > i) & (t <= mi - 1))
        same = (idx[:, None] // (2 * s)) == (idx[None, :] // (2 * s))
        masks.append(same & sec[:, None] & (~sec)[None, :])
    mall_f = np.concatenate(blocks, axis=0).astype(np.float32)
    masks_f = np.stack(masks).astype(np.float32)
    nb = len(blocks)
    mall_b = mall_f.reshape(nb, CH, CH)[:, ::-1, ::-1].reshape(nb * CH, CH)
    masks_b = masks_f[:, ::-1, ::-1]
    eye = np.eye(TT // CH, dtype=np.float32)
    bd = lambda m: np.stack([np.kron(eye, m[lv]) for lv in range(len(HG_LEVELS))])
    return np.stack([mall_f, mall_b]), np.stack([bd(masks_f), bd(masks_b)])


def _hgrn_local_kernel(ph_ref, lbp_ref, mall_ref, mask_ref, ind_ref, ones_ref,
                       oi_ref, qgf_ref, qgb_ref, kvf_ref, kvb_ref, ecf_ref, ecb_ref):
    ph = ph_ref[0]
    lbp = lbp_ref[...]
    ind = ind_ref[...]
    ones = ones_ref[...]
    log_lb, log_1m_lb, one_m_lb = lbp[0:1], lbp[1:2], lbp[2:3]
    q = _silu(ph[:, :WM])
    v = ph[:, 3 * WM:4 * WM]
    ncl = TT // CH
    o_sum = None
    for d, (qg_ref, kv_ref, ec_ref) in enumerate(((qgf_ref, kvf_ref, ecf_ref), (qgb_ref, kvb_ref, ecb_ref))):
        fz = ph[:, WM * (1 + d):WM * (2 + d)]
        lsig = jnp.minimum(fz, 0.0) - jnp.log1p(jnp.exp(-jnp.abs(fz)))
        bb = log_1m_lb + lsig
        logf = jnp.maximum(log_lb, bb) + jnp.log1p(jnp.exp(-jnp.abs(log_lb - bb)))
        k = one_m_lb / (1.0 + jnp.exp(fz))
        e_c = [jnp.exp(_sum01_l(mall_ref[d], logf[c * CH:(c + 1) * CH], 2)) for c in range(ncl)]
        blk = lambda r: jnp.concatenate([e_c[c][r * CH:(r + 1) * CH] for c in range(ncl)], axis=0)
        qg_ref[0] = (q * blk(0)).astype(BF16)
        kd = k * blk(1)
        o_d = _sum01_r(q * k, ind, 2) * v
        q_lv = [q * blk(2 + 2 * lv) for lv in range(len(HG_LEVELS))]
        k_lv = [k * blk(3 + 2 * lv) for lv in range(len(HG_LEVELS))]
        heads = [slice(h * HEAD, (h + 1) * HEAD) for h in range(NH)]
        atts = [None] * NH
        for lv in range(len(HG_LEVELS)):
            for h, sl in enumerate(heads):
                term = _dot_nt(q_lv[lv][:, sl], k_lv[lv][:, sl]) * mask_ref[d, lv]
                atts[h] = term if atts[h] is None else atts[h] + term
        o_d = o_d + jnp.concatenate([_dot(atts[h], v[:, sl]) for h, sl in enumerate(heads)], axis=1)
        o_sum = o_d if o_sum is None else o_sum + o_d
        for c in range(ncl):
            rows = slice(c * CH, (c + 1) * CH)
            kv = _dot_tn(kd[rows], v[rows])
            tot = _sum01_r(logf[rows], ones, 3, _dot_tn)
            kv_ref[0, c] = jnp.concatenate([kv[h * HEAD:(h + 1) * HEAD, h * HEAD:(h + 1) * HEAD] for h in range(NH)], axis=1)
            ec_ref[0, c] = jnp.concatenate([jnp.exp(tot[h * HEAD:(h + 1) * HEAD]) for h in range(NH)], axis=1)
    oi_ref[0] = o_sum


def _hgrn_local(ph, lbp, mall, masks, ind_sum, ones_c):
    b, t, _ = ph.shape
    ncl = TT // CH
    full = lambda a: pl.BlockSpec(a.shape, lambda bi, i: (0,) * a.ndim)
    tile = pl.BlockSpec((1, TT, WM), lambda bi, i: (bi, i, 0))
    st_spec = pl.BlockSpec((1, ncl, HEAD, WM), lambda bi, i: (bi, i, 0, 0))
    st_shape = jax.ShapeDtypeStruct((b, t // CH, HEAD, WM), F32)
    return pl.pallas_call(
        _hgrn_local_kernel,
        grid=(b, t // TT),
        in_specs=[pl.BlockSpec((1, TT, W_PH), lambda bi, i: (bi, i, 0)),
                  full(lbp), full(mall), full(masks), full(ind_sum), full(ones_c)],
        out_specs=[tile, tile, tile, st_spec, st_spec, st_spec, st_spec],
        out_shape=[jax.ShapeDtypeStruct((b, t, WM), F32), jax.ShapeDtypeStruct((b, t, WM), BF16),
                   jax.ShapeDtypeStruct((b, t, WM), BF16), st_shape, st_shape, st_shape, st_shape],
        compiler_params=_cp("parallel", "parallel"),
        name="hgrn_local",
    )(ph, lbp, mall, masks, ind_sum, ones_c)


def _hgrn_scan_kernel(qgf_ref, qgb_ref, kvf_ref, kvb_ref, ecf_ref, ecb_ref, of_ref, ob_ref, s_ref):
    @pl.when(pl.program_id(0) == 0)
    def _():
        s_ref[...] = jnp.zeros_like(s_ref)

    dirs = ((qgf_ref, kvf_ref, ecf_ref, of_ref), (qgb_ref, kvb_ref, ecb_ref, ob_ref))
    for b in range(qgf_ref.shape[0]):
        for d, (qg_ref, kv_ref, ec_ref, o_ref) in enumerate(dirs):
            s = s_ref[b, d]
            sb = s.astype(BF16)
            qg = qg_ref[b]
            o_ref[b] = jnp.concatenate(
                [_dot(qg[:, h * HEAD:(h + 1) * HEAD], sb[:, h * HEAD:(h + 1) * HEAD]) for h in range(NH)], axis=1)
            s_ref[b, d] = s * ec_ref[b, 0] + kv_ref[b, 0]


def _hgrn_scan(qgf, qgb, kvf, kvb, ecf, ecb, n_ctx):
    b, t, _ = qgf.shape
    nc = t // CH
    fwd3 = lambda n: (0, n, 0)
    bwd3 = lambda n: (0, _rev_index(n, n_ctx, nc), 0)
    fwd4 = lambda n: (0, n, 0, 0)
    bwd4 = lambda n: (0, _rev_index(n, n_ctx, nc), 0, 0)
    tok = lambda im: pl.BlockSpec((b, CH, WM), im)
    st = lambda im: pl.BlockSpec((b, 1, HEAD, WM), im)
    return pl.pallas_call(
        _hgrn_scan_kernel,
        grid=(nc,),
        in_specs=[tok(fwd3), tok(bwd3), st(fwd4), st(bwd4), st(fwd4), st(bwd4)],
        out_specs=[tok(fwd3), tok(bwd3)],
        out_shape=[jax.ShapeDtypeStruct((b, t, WM), F32)] * 2,
        scratch_shapes=[pltpu.VMEM((b, 2, HEAD, WM), F32)],
        compiler_params=_cp("arbitrary"),
        name="hgrn_scan",
    )(qgf, qgb, kvf, kvb, ecf, ecb)


def _ret_direction(d, pr, cos, sin, dec_ref, qd, kd, cdec_ref, s_ref):
    q = (pr[:, :WM] * cos + pr[:, 4 * WM:5 * WM] * sin)
    k = (pr[:, WM:2 * WM] * cos + pr[:, 5 * WM:6 * WM] * sin) * HEAD ** -0.5
    v = pr[:, 2 * WM:3 * WM]
    q_in = q * qd
    k_in = k * kd
    outs = []
    for h in range(NH):
        sl = slice(h * HEAD, (h + 1) * HEAD)
        s = s_ref[d, h]
        att = _dot_nt(q[:, sl], k[:, sl]) * dec_ref[d, h]
        outs.append(_dot(att, v[:, sl]) + _dot(q_in[:, sl], s))
        s_ref[d, h] = s * cdec_ref[d, h] + _dot_tn(k_in[:, sl], v[:, sl])
    return jnp.concatenate(outs, axis=1)


def _ret_scan_kernel(cdec_ref, pf_ref, pb_ref, cf_ref, sf_ref, cb_ref, sb_ref, dec_ref, qd_ref, kd_ref,
                     of_ref, ob_ref, s_ref):
    @pl.when(pl.program_id(1) == 0)
    def _():
        s_ref[...] = jnp.zeros_like(s_ref)

    of_ref[0] = _ret_direction(0, pf_ref[0], cf_ref[...], sf_ref[...], dec_ref, qd_ref[0], kd_ref[0], cdec_ref, s_ref)
    ob_ref[0] = _ret_direction(1, pb_ref[0], cb_ref[...], sb_ref[...], dec_ref, qd_ref[1], kd_ref[1], cdec_ref, s_ref)


def _ret_scan(pr, cos_t, sin_t, dec, qdec, kdec, cdec):
    b, t, _ = pr.shape
    nt = t // TT
    fwd3 = lambda bi, n: (bi, n, 0)
    bwd3 = lambda bi, n: (bi, _rev_index(n, 1, nt), 0)
    fwd2 = lambda bi, n: (n, 0)
    bwd2 = lambda bi, n: (_rev_index(n, 1, nt), 0)
    return pl.pallas_call(
        _ret_scan_kernel,
        grid=(b, nt),
        in_specs=[pl.BlockSpec(memory_space=pltpu.SMEM),
                  pl.BlockSpec((1, TT, W_PR), fwd3), pl.BlockSpec((1, TT, W_PR), bwd3),
                  pl.BlockSpec((TT, WM), fwd2), pl.BlockSpec((TT, WM), fwd2),
                  pl.BlockSpec((TT, WM), bwd2), pl.BlockSpec((TT, WM), bwd2),
                  pl.BlockSpec(dec.shape, lambda bi, n: (0, 0, 0, 0)),
                  pl.BlockSpec(qdec.shape, lambda bi, n: (0, 0, 0)),
                  pl.BlockSpec(kdec.shape, lambda bi, n: (0, 0, 0))],
        out_specs=[pl.BlockSpec((1, TT, WM), fwd3), pl.BlockSpec((1, TT, WM), bwd3)],
        out_shape=[jax.ShapeDtypeStruct((b, t, WM), F32)] * 2,
        scratch_shapes=[pltpu.VMEM((2, NH, HEAD, HEAD), F32)],
        compiler_params=_cp("parallel", "arbitrary"),
        name="ret_scan",
    )(cdec, pr, pr, cos_t, sin_t, cos_t, sin_t, dec, qdec, kdec)


def _s5_kernel(u_ref, wz_ref, wy_ref, a1_ref, a2_ref, y_ref, z_ref, hp_ref, *, n_ctx):
    nch = u_ref.shape[2]
    for g in range(S5GB):
        z_ref[g] = _dot_pieces(_pieces(u_ref[0, g], 2), [wz_ref[0, g], wz_ref[1, g]])
    a1 = a1_ref[...]
    a2 = a2_ref[...]

    def step(s, hs):
        tiles = (s, _rev_index(s, n_ctx // 8, nch // 8))
        new = []
        for g in range(S5GB):
            for d in range(2):
                h, hx = hs[2 * g + d]
                r0 = pl.multiple_of(tiles[d] * 8, 8)
                lanes = slice(128 * d, 128 * (d + 1))
                z = z_ref[g, pl.ds(r0, 8), lanes]
                zx = pltpu.roll(z, S5P, 1)
                c1, c2 = a1[g, d:d + 1], a2[g, d:d + 1]
                entering = [None] * 8
                for j in (range(8) if d == 0 else range(7, -1, -1)):
                    entering[j] = h
                    h, hx = c1 * h + c2 * hx + z[j:j + 1], c1 * hx - c2 * h + zx[j:j + 1]
                hp_ref[g, pl.ds(r0, 8), lanes] = jnp.concatenate(entering, axis=0)
                new.append((h, hx))
        return tuple(new)

    zero = jnp.zeros((1, 128), F32)
    lax.fori_loop(0, nch // 8, step, tuple((zero, zero) for _ in range(2 * S5GB)))
    for g in range(S5GB):
        lhs = jnp.concatenate([u_ref[0, g], hp_ref[g]], axis=1)
        y_ref[0, g] = _dot_pieces(_pieces(lhs, 2), [wy_ref[0, g], wy_ref[1, g]])


def _s5_scan(u4, wz, wy, a1, a2, n_ctx):
    b, g, nch, w = u4.shape
    return pl.pallas_call(
        functools.partial(_s5_kernel, n_ctx=n_ctx),
        grid=(b, g // S5GB),
        in_specs=[pl.BlockSpec((1, S5GB, nch, w), lambda bi, gi: (bi, gi, 0, 0)),
                  pl.BlockSpec((2, S5GB, w, w), lambda bi, gi: (0, gi, 0, 0)),
                  pl.BlockSpec((2, S5GB, 2 * w, w), lambda bi, gi: (0, gi, 0, 0)),
                  pl.BlockSpec((S5GB, 2, 128), lambda bi, gi: (gi, 0, 0)),
                  pl.BlockSpec((S5GB, 2, 128), lambda bi, gi: (gi, 0, 0))],
        out_specs=pl.BlockSpec((1, S5GB, nch, w), lambda bi, gi: (bi, gi, 0, 0)),
        out_shape=jax.ShapeDtypeStruct(u4.shape, F32),
        scratch_shapes=[pltpu.VMEM((S5GB, nch, w), F32), pltpu.VMEM((S5GB, nch, w), F32)],
        compiler_params=_cp("parallel", "parallel"),
        name="s5_scan",
    )(u4, wz, wy, a1, a2)


def _s5_weights(a_re, a_im, log_step, b_re, b_im, c_re, c_im):
    step = jnp.exp(log_step)[..., None]
    e_re, e_im = a_re * step, a_im * step
    def lam_pow(n):
        n = n[..., None, None, None] if n.ndim else n
        mag = jnp.exp(e_re * n)
        return mag * jnp.cos(e_im * n), mag * jnp.sin(e_im * n)
    l1r, l1i = lam_pow(jnp.asarray(1.0, F32))
    den = a_re * a_re + a_im * a_im
    fr = ((l1r - 1.0) * a_re + l1i * a_im) / den
    fi = (l1i * a_re - (l1r - 1.0) * a_im) / den
    bbr = fr[..., None] * b_re - fi[..., None] * b_im
    bbi = fr[..., None] * b_im + fi[..., None] * b_re
    j = jnp.arange(S5C, dtype=F32)
    es = functools.partial(jnp.einsum, precision=HI)

    def build(d):
        cr, ci = c_re[d], c_im[d]
        br, bi = bbr[d], bbi[d]
        sel = lambda x: x[:, d] if x.ndim == 4 else x
        pr, pi = lam_pow(jnp.arange(S5C + 1, dtype=F32))
        pr, pi = pr[:, d], pi[:, d]
        cl_r = cr[None] * pr[:, :, None, :] - ci[None] * pi[:, :, None, :]
        cl_i = cr[None] * pi[:, :, None, :] + ci[None] * pr[:, :, None, :]
        kk = es('ngop,gpi->ngoi', cl_r[:S5C], br) - es('ngop,gpi->ngoi', cl_i[:S5C], bi)
        ji, jo = np.arange(S5C)[:, None], np.arange(S5C)[None, :]
        lag = (jo - ji) if d == 0 else (ji - jo)
        place = jnp.asarray(lag[None] == np.arange(S5C)[:, None, None], F32)
        kt = es('lij,lgoc->ijgoc', place, kk)
        toep = kt.transpose(2, 0, 4, 1, 3).reshape(S5G, S5C * 16, S5C * 16)
        pick_m = (lambda x: x[1:S5C + 1]) if d == 0 else (lambda x: x[1:S5C + 1][::-1])
        wo_r = pick_m(cl_r).transpose(1, 3, 0, 2).reshape(S5G, S5P, S5C * 16)
        wo_i = -pick_m(cl_i).transpose(1, 3, 0, 2).reshape(S5G, S5P, S5C * 16)
        wout = jnp.concatenate([wo_r, wo_i], axis=1)
        pick_e = (lambda x: x[:S5C][::-1]) if d == 0 else (lambda x: x[:S5C])
        lr, li = pick_e(pr), pick_e(pi)
        wi_r = (lr[..., None] * br[None] - li[..., None] * bi[None])
        wi_i = (lr[..., None] * bi[None] + li[..., None] * br[None])
        win = jnp.concatenate([wi_r.transpose(1, 0, 3, 2).reshape(S5G, S5C * 16, S5P),
                               wi_i.transpose(1, 0, 3, 2).reshape(S5G, S5C * 16, S5P)], axis=2)
        ar, ai = pr[S5C], pi[S5C]
        a1 = jnp.concatenate([ar, ar], axis=1)
        a2 = jnp.concatenate([-ai, ai], axis=1)
        return toep, wout, win, a1, a2

    tf, of, wf, a1f, a2f = build(0)
    tb, ob, wb, a1b, a2b = build(1)
    wz = jnp.concatenate([wf, wb], axis=2)
    wy = jnp.concatenate([tf + tb, of, ob], axis=1)
    split = lambda w: jnp.stack([w.astype(BF16), (w - w.astype(BF16).astype(F32)).astype(BF16)])
    return split(wz), split(wy), jnp.stack([a1f, a1b], axis=1), jnp.stack([a2f, a2b], axis=1)


def _gelu_tanh(x):
    return 0.5 * x * (1.0 + jnp.tanh(math.sqrt(2.0 / math.pi) * (x + 0.044715 * x * x * x)))


def _outproj_kernel(x_ref, mod_ref, gf_ref, gb_ref, gz_ref, ys_ref, us_ref, hi_ref, hf_ref, hb_ref, hg_ref,
                    rf_ref, rb_ref, rg_ref, vec_ref, glu_ref, ind_ref, w_ref, ln_ref, o_ref, *, alpha):
    ind = ind_ref[...]
    vec = vec_ref[...]

    def head_rms(o):
        return o * lax.rsqrt(_sum01_r(o * o, ind, 2) * (1.0 / HEAD) + RMS_EPS)

    m_gdn = head_rms(gf_ref[0] + gb_ref[0]) * vec[0:1] * _silu(gz_ref[0])
    u = us_ref[0]
    ys = _gelu_tanh(ys_ref[0] + vec[2:3] * u)
    m_s5 = ys * _sigmoid(_dot(ys, glu_ref[...]) + vec[3:4])
    m_hg = head_rms(hi_ref[0] + hf_ref[0] + hb_ref[0]) * vec[1:2] * _silu(hg_ref[0])
    m_rt = head_rms(rf_ref[0] + rb_ref[0]) * _silu(rg_ref[0])
    acc = None
    for k, mk in enumerate((m_gdn, m_s5, m_hg, m_rt)):
        part = _dot(mk.astype(BF16), w_ref[k * WM:(k + 1) * WM, :])
        acc = part if acc is None else acc + part
    m = mod_ref[0, 0]
    ln = ln_ref[...]
    o_ref[0] = _layer_norm(alpha * x_ref[0] + m[2:3] * acc, ln[0:1], ln[1:2])


def _outproj(xs, mod, g_of, g_ob, pg, ys, ps, h_oi, h_of, h_ob, ph, r_of, r_ob, pr, vec, glu_w, ind_sum, w_out, ln,
             alpha):
    b, t, d = xs.shape
    tile = lambda c: pl.BlockSpec((1, TT, WM), lambda bi, i, c=c: (bi, i, c))
    full = lambda a: pl.BlockSpec(a.shape, lambda bi, i: (0,) * a.ndim)
    return pl.pallas_call(
        functools.partial(_outproj_kernel, alpha=alpha),
        grid=(b, t // TT),
        in_specs=[pl.BlockSpec((1, TT, d), lambda bi, i: (bi, i, 0)),
                  pl.BlockSpec((1, 1, 6, d), lambda bi, i: (bi, jnp.minimum(i, 1), 0, 0)),
                  tile(0), tile(0), tile(3), tile(0), tile(0), tile(0), tile(0), tile(0), tile(4),
                  tile(0), tile(0), tile(3),
                  full(vec), full(glu_w), full(ind_sum), full(w_out), full(ln)],
        out_specs=pl.BlockSpec((1, TT, d), lambda bi, i: (bi, i, 0)),
        out_shape=jax.ShapeDtypeStruct(xs.shape, F32),
        compiler_params=_cp("parallel", "parallel"),
        name="outproj",
    )(xs, mod, g_of, g_ob, pg, ys, ps, h_oi, h_of, h_ob, ph, r_of, r_ob, pr, vec, glu_w, ind_sum, w_out, ln)


def _ffn_kernel(x_ref, mod_ref, w1_ref, w3_ref, w2_ref, ln_ref, o_ref, *, alpha):
    m = mod_ref[0, 0]
    x = x_ref[0]
    h = (x * (1.0 + m[4:5]) + m[3:4]).astype(BF16)
    act = (_silu(_dot(h, w1_ref[...])) * _dot(h, w3_ref[...])).astype(BF16)
    y = _dot(act, w2_ref[...])
    ln = ln_ref[...]
    o_ref[0] = _layer_norm(alpha * x + m[5:6] * y, ln[0:1], ln[1:2])


def _ffn(xs, mod, w1, w3, w2, ln, alpha):
    b, t, d = xs.shape
    full = lambda a: pl.BlockSpec(a.shape, lambda bi, i: (0,) * a.ndim)
    return pl.pallas_call(
        functools.partial(_ffn_kernel, alpha=alpha),
        grid=(b, t // TT),
        in_specs=[pl.BlockSpec((1, TT, d), lambda bi, i: (bi, i, 0)),
                  pl.BlockSpec((1, 1, 6, d), lambda bi, i: (bi, jnp.minimum(i, 1), 0, 0)),
                  full(w1), full(w3), full(w2), full(ln)],
        out_specs=pl.BlockSpec((1, TT, d), lambda bi, i: (bi, i, 0)),
        out_shape=jax.ShapeDtypeStruct(xs.shape, F32),
        compiler_params=_cp("parallel", "parallel"),
        name="ffn_dense",
    )(xs, mod, w1, w3, w2, ln)


def _router_kernel(x_ref, mod_ref, r_ref, h_ref, rt_ref):
    m = mod_ref[0, 0]
    h = x_ref[0] * (1.0 + m[4:5]) + m[3:4]
    h_ref[0] = h
    lane = lax.broadcasted_iota(jnp.int32, (TT, 128), 1)
    logits = jnp.where(lane < N_EXP, _dot(h, r_ref[...], HI), -jnp.inf)
    m1 = jnp.max(logits, axis=-1, keepdims=True)
    i1 = jnp.min(jnp.where(logits == m1, lane, 128), axis=-1, keepdims=True)
    rest = jnp.where(lane == i1, -jnp.inf, logits)
    m2 = jnp.max(rest, axis=-1, keepdims=True)
    i2 = jnp.min(jnp.where(rest == m2, lane, 128), axis=-1, keepdims=True)
    e = jnp.exp(m2 - m1)
    g1 = 1.0 / (1.0 + e)
    g2 = e / (1.0 + e)
    rt_ref[0] = jnp.where(lane == 0, i1.astype(F32),
                          jnp.where(lane == 1, i2.astype(F32),
                                    jnp.where(lane == 2, g1, jnp.where(lane == 3, g2, 0.0))))


def _router(xs, mod, router_pad):
    b, t, d = xs.shape
    return pl.pallas_call(
        _router_kernel,
        grid=(b, t // TT),
        in_specs=[pl.BlockSpec((1, TT, d), lambda bi, i: (bi, i, 0)),
                  pl.BlockSpec((1, 1, 6, d), lambda bi, i: (bi, jnp.minimum(i, 1), 0, 0)),
                  pl.BlockSpec(router_pad.shape, lambda bi, i: (0, 0))],
        out_specs=[pl.BlockSpec((1, TT, d), lambda bi, i: (bi, i, 0)),
                   pl.BlockSpec((1, TT, 128), lambda bi, i: (bi, i, 0))],
        out_shape=[jax.ShapeDtypeStruct(xs.shape, F32), jax.ShapeDtypeStruct((b, t, 128), F32)],
        compiler_params=_cp("parallel", "parallel"),
        name="moe_router",
    )(xs, mod, router_pad)


def _row_copy(src_hbm, row, dst_ref, slot, sem):
    return pltpu.make_async_copy(src_hbm.at[pl.ds(row, 1), :], dst_ref.at[pl.ds(slot, 1), :], sem)


def _dispatch_kernel(dest_ref, h_ref, init_hbm, o_hbm, sem, *, tiles_per_batch):
    del init_hbm
    tile = pl.program_id(0) * tiles_per_batch + pl.program_id(1)
    base = tile * (2 * TT)

    def start(r, c):
        t = lax.rem(r, TT)
        pltpu.make_async_copy(h_ref.at[0, pl.ds(t, 1), :], o_hbm.at[pl.ds(dest_ref[base + r], 1), :], sem).start()
        return c

    lax.fori_loop(0, 2 * TT, start, 0, unroll=8)
    for _ in range(2):
        pltpu.make_async_copy(h_ref.at[0], o_hbm.at[pl.ds(0, TT), :], sem).wait()


def _dispatch_rows(h, dest_tiles, n_rows):
    b, t, d = h.shape
    return pl.pallas_call(
        functools.partial(_dispatch_kernel, tiles_per_batch=t // TT),
        grid_spec=pltpu.PrefetchScalarGridSpec(
            num_scalar_prefetch=1,
            grid=(b, t // TT),
            in_specs=[pl.BlockSpec((1, TT, d), lambda bi, i, dr: (bi, i, 0)),
                      pl.BlockSpec(memory_space=pl.ANY)],
            out_specs=pl.BlockSpec(memory_space=pl.ANY),
            scratch_shapes=[pltpu.SemaphoreType.DMA(())]),
        out_shape=jax.ShapeDtypeStruct((n_rows, d), h.dtype),
        input_output_aliases={2: 0},
        compiler_params=_cp("arbitrary", "arbitrary"),
        name="moe_dispatch",
    )(dest_tiles, h, jnp.zeros((n_rows, d), h.dtype))


def _experts_kernel(be_ref, x_ref, w1_ref, w3_ref, w2_ref, y_ref):
    f = pl.program_id(1)
    x = x_ref[...].astype(BF16)
    act = (_silu(_dot(x, w1_ref[0])) * _dot(x, w3_ref[0])).astype(BF16)
    y = _dot(act, w2_ref[0])

    @pl.when(f == 0)
    def _():
        y_ref[...] = y

    @pl.when(f != 0)
    def _():
        y_ref[...] += y


def _experts(xs_sorted, block_expert, w1, w3, w2):
    n_rows, d = xs_sorted.shape
    ff = w1.shape[2]
    nf = 2
    tf = ff // nf
    return pl.pallas_call(
        _experts_kernel,
        grid_spec=pltpu.PrefetchScalarGridSpec(
            num_scalar_prefetch=1,
            grid=(n_rows // MOE_RB, nf),
            in_specs=[pl.BlockSpec((MOE_RB, d), lambda j, f, be: (j, 0)),
                      pl.BlockSpec((1, d, tf), lambda j, f, be: (be[j], 0, f)),
                      pl.BlockSpec((1, d, tf), lambda j, f, be: (be[j], 0, f)),
                      pl.BlockSpec((1, tf, d), lambda j, f, be: (be[j], f, 0))],
            out_specs=pl.BlockSpec((MOE_RB, d), lambda j, f, be: (j, 0))),
        out_shape=jax.ShapeDtypeStruct((n_rows, d), F32),
        compiler_params=_cp("parallel", "arbitrary"),
        name="moe_experts",
    )(block_expert, xs_sorted, w1, w3, w2)


def _combine_kernel(dest_ref, x_ref, mod_ref, rt_ref, ln_ref, y_hbm, o_ref, buf, sem, *, alpha, tiles_per_batch):
    tile = pl.program_id(0) * tiles_per_batch + pl.program_id(1)
    base = tile * (2 * TT)

    def start(r, c):
        _row_copy(y_hbm, dest_ref[base + r], buf, r, sem).start()
        return c

    lax.fori_loop(0, 2 * TT, start, 0, unroll=8)
    pltpu.make_async_copy(y_hbm.at[pl.ds(0, 2 * TT), :], buf, sem).wait()
    rt = rt_ref[0]
    y = rt[:, 2:3] * buf[0:TT, :] + rt[:, 3:4] * buf[TT:2 * TT, :]
    m = mod_ref[0, 0]
    ln = ln_ref[...]
    o_ref[0] = _layer_norm(alpha * x_ref[0] + m[5:6] * y, ln[0:1], ln[1:2])


def _combine(xs, mod, rt, ln, ys_sorted, dest_tiles, alpha):
    b, t, d = xs.shape
    return pl.pallas_call(
        functools.partial(_combine_kernel, alpha=alpha, tiles_per_batch=t // TT),
        grid_spec=pltpu.PrefetchScalarGridSpec(
            num_scalar_prefetch=1,
            grid=(b, t // TT),
            in_specs=[pl.BlockSpec((1, TT, d), lambda bi, i, dr: (bi, i, 0)),
                      pl.BlockSpec((1, 1, 6, d), lambda bi, i, dr: (bi, jnp.minimum(i, 1), 0, 0)),
                      pl.BlockSpec((1, TT, 128), lambda bi, i, dr: (bi, i, 0)),
                      pl.BlockSpec(ln.shape, lambda bi, i, dr: (0, 0)),
                      pl.BlockSpec(memory_space=pl.ANY)],
            out_specs=pl.BlockSpec((1, TT, d), lambda bi, i, dr: (bi, i, 0)),
            scratch_shapes=[pltpu.VMEM((2 * TT, d), F32), pltpu.SemaphoreType.DMA(())]),
        out_shape=jax.ShapeDtypeStruct(xs.shape, F32),
        compiler_params=_cp("arbitrary", "arbitrary"),
        name="moe_combine",
    )(dest_tiles, xs, mod, rt, ln, ys_sorted)


def _moe(xs, mod, router_pad, w1, w3, w2, ln, alpha):
    b, t, d = xs.shape
    n_tok = b * t
    h, rt = _router(xs, mod, router_pad)
    e_idx = rt[..., 0:2].astype(jnp.int32).reshape(n_tok * 2)
    onehot = (e_idx[:, None] == jnp.arange(N_EXP, dtype=jnp.int32)[None, :]).astype(jnp.int32)
    seg = 2 * TT
    local = jnp.einsum('ij,tjk->tik', jnp.asarray(np.tril(np.ones((seg, seg))), F32),
                       onehot.astype(F32).reshape(-1, seg, N_EXP), precision=HI)
    seg_tot = local[:, -1, :]
    seg_off = jnp.cumsum(seg_tot, axis=0) - seg_tot
    csum = (local + seg_off[:, None, :]).astype(jnp.int32).reshape(n_tok * 2, N_EXP)
    counts = csum[-1]
    padded = (counts + MOE_RB - 1) // MOE_RB * MOE_RB
    pad_end = jnp.cumsum(padded)
    pad_start = pad_end - padded
    dest = jnp.sum(onehot * (csum - 1 + pad_start[None, :]), axis=1)
    n_blocks = -(-(n_tok * 2) // MOE_RB) + N_EXP
    n_rows = n_blocks * MOE_RB
    block_expert = jnp.minimum(
        jnp.sum(jnp.arange(n_blocks, dtype=jnp.int32)[:, None] * MOE_RB >= pad_end[None, :], axis=1),
        N_EXP - 1).astype(jnp.int32)
    dest_tiles = dest.reshape(n_tok // TT, TT, 2).transpose(0, 2, 1).reshape(n_tok * 2)
    xs_sorted = _dispatch_rows(h, dest_tiles, n_rows)
    ys_sorted = _experts(xs_sorted, block_expert, w1, w3, w2)
    return _combine(xs, mod, rt, ln, ys_sorted, dest_tiles, alpha)


def _ret_constants(decay_param):
    log_gamma = -jnp.exp(decay_param)
    idx = jnp.arange(TT, dtype=F32)
    diff = idx[:, None] - idx[None, :]
    lg = log_gamma[:, :, None, None]
    dec_f = jnp.exp(jnp.where(diff >= 0, diff * lg[0], -jnp.inf))
    dec_b = jnp.exp(jnp.where(diff <= 0, -diff * lg[1], -jnp.inf))
    dec = jnp.stack([dec_f, dec_b])
    rep = lambda a: jnp.repeat(a, HEAD, axis=-1)
    qdec = jnp.stack([rep(jnp.exp((idx[:, None] + 1.0) * log_gamma[0][None, :])),
                      rep(jnp.exp((TT - idx[:, None]) * log_gamma[1][None, :]))])
    kdec = jnp.stack([rep(jnp.exp((TT - 1.0 - idx[:, None]) * log_gamma[0][None, :])),
                      rep(jnp.exp(idx[:, None] * log_gamma[1][None, :]))])
    cdec = jnp.exp(TT * log_gamma)
    return dec, qdec, kdec, cdec


def _rotary_tables(n_ctx_tok, n_lat, grid_w):
    rows = jnp.repeat(jnp.arange(n_lat // grid_w, dtype=F32), grid_w)
    cols = jnp.tile(jnp.arange(grid_w, dtype=F32), n_lat // grid_w)
    quarter = HEAD // 4
    inv_freq = ROPE_BASE ** (-jnp.arange(quarter, dtype=F32) / quarter)
    ang = jnp.concatenate([rows[:, None] * inv_freq, cols[:, None] * inv_freq], axis=-1)
    cos, sin = jnp.cos(ang), jnp.sin(ang)
    cos_h = jnp.concatenate([cos, cos], axis=-1)
    sin_h = jnp.concatenate([-sin, sin], axis=-1)
    cos_t = jnp.concatenate([jnp.ones((n_ctx_tok, HEAD), F32), cos_h], axis=0)
    sin_t = jnp.concatenate([jnp.zeros((n_ctx_tok, HEAD), F32), sin_h], axis=0)
    return jnp.tile(cos_t, (1, NH)), jnp.tile(sin_t, (1, NH))


def _swap_halves_cols(w):
    dm = w.shape[0]
    return w.reshape(dm, NH, 2, HEAD // 2)[:, :, ::-1, :].reshape(dm, WM)


def _prep_w_in(w):
    dm = w.shape[0]
    r0 = 3344 - 768
    rq, rk = w[:, r0:r0 + WM], w[:, r0 + WM:r0 + 2 * WM]
    return jnp.concatenate([w[:, :1040], jnp.zeros((dm, W_AB - 16), w.dtype), w[:, 1040:],
                            _swap_halves_cols(rq), _swap_halves_cols(rk)], axis=1).astype(BF16)


def kernel(x, c, ctx, c_ctx, ada_w, ada_b, w_in, w_out, ln_g, ln_b, gdn_conv_w, gdn_a_log, gdn_dt_bias, gdn_norm_w, s5_a_re, s5_a_im, s5_log_step, s5_b_re, s5_b_im, s5_c_re, s5_c_im, s5_d, s5_glu_w, s5_glu_b, hgrn_lower_bounds, hgrn_norm_w, ret_decay, ffn_w1, ffn_w3, ffn_w2, moe_router, moe_w1, moe_w3, moe_w2):
    bsz, n_lat, d = x.shape
    n_ctx_tok = ctx.shape[1]
    depth = ada_w.shape[0]
    grid_w = 64
    assert n_ctx_tok == TT and n_lat % TT == 0 and bsz <= 7
    t = n_ctx_tok + n_lat
    alpha = (2.0 * depth) ** 0.25

    xs = jnp.concatenate([ctx, x], axis=1)
    cs = jnp.concatenate([c, c_ctx[None, :], jnp.zeros((8 - bsz - 1, d), F32)], axis=0)
    mod_all = _ada_mod(cs, ada_w, ada_b)
    lat_mod = mod_all[:, :bsz].reshape(depth, bsz, 6, d)
    ctx_mod = jnp.broadcast_to(mod_all[:, bsz].reshape(depth, 1, 6, d), (depth, bsz, 6, d))
    mod_tab = jnp.stack([ctx_mod, lat_mod], axis=2)

    lb_all = jnp.cumsum(jax.nn.softmax(hgrn_lower_bounds.astype(F32), axis=0), axis=0)
    lb_all = lb_all - lb_all[0]
    ind_sum = jnp.asarray(np.kron(np.eye(NH), np.ones((HEAD, HEAD))), BF16)
    chunks_eye = np.eye(TT // CH)
    tri_bd = jnp.asarray(np.kron(chunks_eye, np.tril(np.ones((CH, CH)))), BF16)
    ones_bd = jnp.asarray(np.kron(chunks_eye, np.ones((CH, CH))), BF16)
    ones_c = jnp.ones((CH, HEAD), BF16)
    mall_np, masks_np = _hgrn_constants()
    mall, masks = jnp.asarray(mall_np, BF16), jnp.asarray(masks_np)
    cos_t, sin_t = _rotary_tables(n_ctx_tok, n_lat, grid_w)
    nch = t // S5C

    for layer in range(depth):
        mod = mod_tab[layer]
        pg, pab, ps, ph, pr = _inproj(xs, mod, _prep_w_in(w_in[layer]))

        pabt = pab[..., :16].transpose(0, 2, 1)
        g_loc = _gdn_local(pg, pab, pabt, gdn_conv_w[layer], ind_sum, -jnp.exp(gdn_a_log[layer]),
                           gdn_dt_bias[layer], tri_bd, ones_bd)
        g_of, g_ob = _gdn_scan(g_loc, ind_sum, n_ctx_tok // CH)

        wz, wy, a1, a2 = _s5_weights(s5_a_re[layer], s5_a_im[layer], s5_log_step[layer], s5_b_re[layer],
                                     s5_b_im[layer], s5_c_re[layer], s5_c_im[layer])
        u4 = ps.reshape(bsz, nch, S5C, S5G, 16).transpose(0, 3, 1, 2, 4).reshape(bsz, S5G, nch, S5C * 16)
        y4 = _s5_scan(u4, wz, wy, a1, a2, n_ctx_tok // S5C)
        ys = y4.reshape(bsz, S5G, nch, S5C, 16).transpose(0, 2, 3, 1, 4).reshape(bsz, t, WM)

        lb = lb_all[layer][None, :]
        lbp = jnp.concatenate([jnp.log(lb), jnp.log1p(-lb), 1.0 - lb, jnp.zeros((5, WM), F32)], axis=0)
        h_oi, h_qgf, h_qgb, h_kvf, h_kvb, h_ecf, h_ecb = _hgrn_local(ph, lbp, mall, masks, ind_sum, ones_c)
        h_of, h_ob = _hgrn_scan(h_qgf, h_qgb, h_kvf, h_kvb, h_ecf, h_ecb, n_ctx_tok // CH)

        dec, qdec, kdec, cdec = _ret_constants(ret_decay[layer])
        r_of, r_ob = _ret_scan(pr, cos_t, sin_t, dec, qdec, kdec, cdec)

        vec = jnp.concatenate([jnp.tile(gdn_norm_w[layer], NH)[None], jnp.tile(hgrn_norm_w[layer], NH)[None],
                               s5_d[layer][None], s5_glu_b[layer][None], jnp.zeros((4, WM), F32)], axis=0)
        xs = _outproj(xs, mod, g_of, g_ob, pg, ys, ps, h_oi, h_of, h_ob, ph, r_of, r_ob, pr, vec, s5_glu_w[layer],
                      ind_sum, w_out[layer].astype(BF16), jnp.stack([ln_g[layer, 0], ln_b[layer, 0]]), alpha)

        j = layer // 2
        ln2 = jnp.stack([ln_g[layer, 1], ln_b[layer, 1]])
        if layer % 2 == 0:
            xs = _ffn(xs, mod, ffn_w1[j].astype(BF16), ffn_w3[j].astype(BF16), ffn_w2[j].astype(BF16), ln2, alpha)
        else:
            router_pad = jnp.concatenate([moe_router[j], jnp.zeros((d, 128 - N_EXP), F32)], axis=1)
            xs = _moe(xs, mod, router_pad, moe_w1[j].astype(BF16), moe_w3[j].astype(BF16),
                      moe_w2[j].astype(BF16), ln2, alpha)
    return xs[:, n_ctx_tok:, :]
```

```python
import functools
import math

import numpy as np
import jax
import jax.numpy as jnp
from jax import lax
from jax.experimental import pallas as pl
from jax.experimental.pallas import tpu as pltpu

F32 = jnp.float32
BF16 = jnp.bfloat16
HI = lax.Precision.HIGHEST

HEAD = 64
NH = 4
WM = NH * HEAD
TT = 256
CH = 64
S5C = 16
S5G = 16
S5P = 64
S5GB = 8
N_EXP = 8
MOE_RB = 512
LN_EPS = 1e-5
RMS_EPS = 1e-6
ROPE_BASE = 10000.0
VMEM_LIMIT = 56 * 1024 * 1024


def _cp(*sem):
    return pltpu.CompilerParams(dimension_semantics=sem, vmem_limit_bytes=VMEM_LIMIT)


def _sigmoid(x):
    return 1.0 / (1.0 + jnp.exp(-x))


def _silu(x):
    return x * _sigmoid(x)


def _softplus(x):
    return jnp.maximum(x, 0.0) + jnp.log1p(jnp.exp(-jnp.abs(x)))


def _dot(a, b, precision=None):
    return jnp.dot(a, b, preferred_element_type=F32, precision=precision)


def _dot_nt(a, b, precision=None):
    return lax.dot_general(a, b, (((1,), (1,)), ((), ())), preferred_element_type=F32, precision=precision)


def _dot_tn(a, b, precision=None):
    return lax.dot_general(a, b, (((0,), (0,)), ((), ())), preferred_element_type=F32, precision=precision)


def _layer_norm(y, g, b):
    mu = jnp.mean(y, axis=-1, keepdims=True)
    yc = y - mu
    var = jnp.mean(yc * yc, axis=-1, keepdims=True)
    return yc * lax.rsqrt(var + LN_EPS) * g + b


def _rev_index(n, n_ctx, n_all):
    return jnp.where(n < n_ctx, n_ctx - 1 - n, n_all + n_ctx - 1 - n)


def _ada_kernel(c_ref, w_ref, b_ref, o_ref):
    o_ref[0] = _dot(_silu(c_ref[...]), w_ref[0], HI) + b_ref[0]


def _ada_mod(cs, ada_w, ada_b):
    depth, d, d6 = ada_w.shape
    tn = 1024
    return pl.pallas_call(
        _ada_kernel,
        grid=(depth, d6 // tn),
        in_specs=[pl.BlockSpec((8, d), lambda l, j: (0, 0)),
                  pl.BlockSpec((1, d, tn), lambda l, j: (l, 0, j)),
                  pl.BlockSpec((1, 1, tn), lambda l, j: (l, 0, j))],
        out_specs=pl.BlockSpec((1, 8, tn), lambda l, j: (l, 0, j)),
        out_shape=jax.ShapeDtypeStruct((depth, 8, d6), F32),
        compiler_params=_cp("parallel", "parallel"),
        name="ada_mod",
    )(cs, ada_w, ada_b.reshape(depth, 1, d6))


W_PG, W_AB, W_S5, W_PH, W_PR = 1024, 128, 256, 1280, 1536
P_OFF = np.cumsum([0, W_PG, W_AB, W_S5, W_PH, W_PR])


def _inproj_kernel(x_ref, mod_ref, w_ref, pg_ref, pab_ref, ps_ref, ph_ref, pr_ref):
    m = mod_ref[0, 0]
    h = (x_ref[0] * (1.0 + m[1:2]) + m[0:1]).astype(BF16)
    for k, o_ref in enumerate((pg_ref, pab_ref, ps_ref, ph_ref, pr_ref)):
        o_ref[0] = _dot(h, w_ref[:, P_OFF[k]:P_OFF[k + 1]]).astype(o_ref.dtype)


def _inproj(xs, mod, w):
    b, t, d = xs.shape
    widths = (W_PG, W_AB, W_S5, W_PH, W_PR)
    return pl.pallas_call(
        _inproj_kernel,
        grid=(b, t // TT),
        in_specs=[pl.BlockSpec((1, TT, d), lambda bi, i: (bi, i, 0)),
                  pl.BlockSpec((1, 1, 6, d), lambda bi, i: (bi, jnp.minimum(i, 1), 0, 0)),
                  pl.BlockSpec(w.shape, lambda bi, i: (0, 0))],
        out_specs=[pl.BlockSpec((1, TT, wd), lambda bi, i: (bi, i, 0)) for wd in widths],
        out_shape=[jax.ShapeDtypeStruct((b, t, wd), BF16 if wd == W_S5 else F32) for wd in widths],
        compiler_params=_cp("parallel", "parallel"),
        name="inproj",
    )(xs, mod, w)


def _pieces(x, n):
    out, r = [], x
    for i in range(n):
        p = r.astype(BF16)
        out.append(p)
        if i + 1 < n:
            r = r - p.astype(F32)
    return out


def _dot_pieces(a_parts, b_parts, dot=_dot):
    n = max(len(a_parts), len(b_parts))
    acc = None
    for i, ap in enumerate(a_parts):
        for j, bp in enumerate(b_parts):
            if i + j < n:
                t = dot(ap, bp)
                acc = t if acc is None else acc + t
    return acc


def _sum01_l(m01, x, n):
    return _dot_pieces([m01], _pieces(x, n))


def _sum01_r(x, m01, n, dot=_dot):
    return _dot_pieces(_pieces(x, n), [m01], dot)


GDN_PIECES = 1
GDN_BASE = 16


def _unit_tri_inverse(a_list, eye, masks):
    n = GDN_PIECES
    ident = jnp.where(eye, 1.0, 0.0)
    ds = [jnp.where(masks[0], a, 0.0) for a in a_list]
    ts = [ident - d for d in ds]
    ps = [_pieces(d, n) for d in ds]
    size = 2
    while size < GDN_BASE:
        ps = [_pieces(_dot_pieces(p, p), n) for p in ps]
        ts = [t + _dot_pieces(_pieces(t, n), p) for t, p in zip(ts, ps)]
        size *= 2
    for off_mask in masks[1:]:
        tps = [_pieces(t, n) for t in ts]
        mids = [_pieces(_dot_pieces(_pieces(jnp.where(off_mask, a, 0.0), n), tp), n) for a, tp in zip(a_list, tps)]
        ts = [t - _dot_pieces(tp, mid) for t, tp, mid in zip(ts, tps, mids)]
    return ts


def _gdn_local_kernel(na_ref, dtb_ref, p_ref, pv_ref, nx_ref, cw_ref, ind_ref, ab_ref, abt_ref, tri_ref, trit_ref,
                      ones_ref, uf_ref, ub_ref, wf_ref, wb_ref, qkf_ref, qkb_ref, qgf_ref, qgb_ref, kdf_ref, kdb_ref,
                      eg_ref):
    i = pl.program_id(1)
    nt = pl.num_programs(1)
    x = p_ref[0]
    prev = jnp.where(i >= 2, pv_ref[0][7:8], 0.0)
    nxt = jnp.where((i >= 1) & (i < nt - 1), nx_ref[0][0:1], 0.0)
    row1 = lax.broadcasted_iota(jnp.int32, (TT, 1), 0)
    xm = jnp.where(row1 == 0, prev, pltpu.roll(x, 1, 0))
    xp = jnp.where(row1 == TT - 1, nxt, pltpu.roll(x, TT - 1, 0))
    cw = cw_ref[...]
    y = _silu(cw[0:1] * xm + cw[1:2] * x + cw[2:3] * xp)
    q, k, v = y[:, :WM], y[:, WM:2 * WM], y[:, 2 * WM:]
    ind = ind_ref[...]
    q = q * lax.rsqrt(_sum01_r(q * q, ind, 2) + RMS_EPS) * HEAD ** -0.5
    k = k * lax.rsqrt(_sum01_r(k * k, ind, 2) + RMS_EPS)

    ab = ab_ref[0]
    abt = abt_ref[0]
    ones_bd = ones_ref[...]
    row = lax.broadcasted_iota(jnp.int32, (TT, TT), 0)
    col = lax.broadcasted_iota(jnp.int32, (TT, TT), 1)
    same = (row // CH) == (col // CH)
    eye = row == col
    in_block = lambda n: (row // n) == (col // n)
    inv_masks, n = [in_block(GDN_BASE)], GDN_BASE
    while n < CH:
        inv_masks.append(in_block(2 * n) & jnp.logical_not(in_block(n)))
        n *= 2
    out_refs =((uf_ref, wf_ref, qkf_ref, qgf_ref, kdf_ref), (ub_ref, wb_ref, qkb_ref, qgb_ref, kdb_ref))
    e_last = []
    for d in range(2):
        incl = same & ((row >= col) if d == 0 else (row <= col))
        strict = same & ((row > col) if d == 0 else (row < col))
        tri_c = tri_ref[...] if d == 0 else trit_ref[...]
        a_col, b_col = ab[:, 4 * d:4 * d + 4], ab[:, 8 + 4 * d:12 + 4 * d]
        a_row = abt[4 * d:4 * d + 4, :]
        g_col = jnp.concatenate([na_ref[d, h] * _softplus(a_col[:, h:h + 1] + dtb_ref[d, h]) for h in range(NH)], axis=1)
        g_row = jnp.concatenate([na_ref[d, h] * _softplus(a_row[h:h + 1, :] + dtb_ref[d, h]) for h in range(NH)], axis=0)
        gc_col = _sum01_l(tri_c, g_col, 3)
        gc_row = _sum01_r(g_row, tri_c, 3, _dot_nt)
        gl_col = _sum01_l(ones_bd, g_col, 3)
        e_last.append(jnp.exp(gl_col))
        a_list, rhs, qks, qgs, kds = [], [], [], [], []
        for h in range(NH):
            sl = slice(h * HEAD, (h + 1) * HEAD)
            qh, kh, vh = q[:, sl], k[:, sl], v[:, sl]
            beta = _sigmoid(b_col[:, h:h + 1])
            gcc = gc_col[:, h:h + 1]
            decay = jnp.exp(jnp.where(incl, gcc - gc_row[h:h + 1, :], -jnp.inf))
            kb = kh * beta
            a_list.append(_dot_nt(kb, kh) * jnp.where(strict, decay, 0.0))
            rhs.append(_pieces(jnp.concatenate([vh * beta, kb * jnp.exp(gcc)], axis=1), GDN_PIECES))
            qk = _dot_nt(qh, kh) * decay
            qks.append(qk[:, 0:CH] + qk[:, CH:2 * CH] + qk[:, 2 * CH:3 * CH] + qk[:, 3 * CH:4 * CH])
            qgs.append(qh * jnp.exp(gcc))
            kds.append(kh * jnp.exp(gl_col[:, h:h + 1] - gcc))
        xs = [_dot_pieces(_pieces(t, GDN_PIECES), r) for t, r in zip(_unit_tri_inverse(a_list, eye, inv_masks), rhs)]
        us = [xx[:, :HEAD] for xx in xs]
        ws = [xx[:, HEAD:] for xx in xs]
        u_ref, w_ref, qk_ref, qg_ref, kd_ref = out_refs[d]
        u_ref[0] = jnp.concatenate(us, axis=1)
        w_ref[0] = jnp.concatenate(ws, axis=1).astype(BF16)
        qk_ref[0] = jnp.concatenate(qks, axis=1).astype(BF16)
        qg_ref[0] = jnp.concatenate(qgs, axis=1).astype(BF16)
        kdt = jnp.concatenate(kds, axis=1).T
        for c in range(TT // CH):
            kd_ref[0, c] = kdt[:, c * CH:(c + 1) * CH].astype(BF16)
    eg_ref[0] =jnp.concatenate(e_last + [jnp.zeros((TT, 128 - 2 * NH), F32)], axis=1)


def _gdn_local(pg, pab, pabt, conv_w, ind_sum, neg_a, dt_bias, tri_bd, ones_bd):
    b, t, _ = pg.shape
    w3 = 3 * WM
    n8 = t // 8
    smem = pl.BlockSpec(memory_space=pltpu.SMEM)
    full = lambda a: pl.BlockSpec(a.shape, lambda bi, i: (0,) * a.ndim)
    tile = pl.BlockSpec((1, TT, WM), lambda bi, i: (bi, i, 0))
    f32o = jax.ShapeDtypeStruct((b, t, WM), F32)
    b16o = jax.ShapeDtypeStruct((b, t, WM), BF16)
    return pl.pallas_call(
        _gdn_local_kernel,
        grid=(b, t // TT),
        in_specs=[smem, smem,
                  pl.BlockSpec((1, TT, w3), lambda bi, i: (bi, i, 0)),
                  pl.BlockSpec((1, 8, w3), lambda bi, i: (bi, jnp.maximum(i * (TT // 8) - 1, 0), 0)),
                  pl.BlockSpec((1, 8, w3), lambda bi, i: (bi, jnp.minimum((i + 1) * (TT // 8), n8 - 1), 0)),
                  pl.BlockSpec((3, w3), lambda bi, i: (0, 0)),
                  full(ind_sum),
                  pl.BlockSpec((1, TT, W_AB), lambda bi, i: (bi, i, 0)),
                  pl.BlockSpec((1, 16, TT), lambda bi, i: (bi, 0, i)),
                  full(tri_bd), full(tri_bd), full(ones_bd)],
        out_specs=[tile] * 8 + [pl.BlockSpec((1, TT // CH, WM, CH), lambda bi, i: (bi, i, 0, 0))] * 2
                  + [pl.BlockSpec((1, TT, 128), lambda bi, i: (bi, i, 0))],
        out_shape=[f32o, f32o] + [b16o] * 6 + [jax.ShapeDtypeStruct((b, t // CH, WM, CH), BF16)] * 2
                  + [jax.ShapeDtypeStruct((b, t, 128), F32)],
        compiler_params=_cp("parallel", "parallel"),
        name="gdn_local",
    )(neg_a, dt_bias, pg, pg, pg, conv_w, ind_sum, pab, pabt, tri_bd, tri_bd.T, ones_bd)


def _gdn_scan_kernel(uf_ref, ub_ref, wf_ref, wb_ref, qkf_ref, qkb_ref, qgf_ref, qgb_ref, kdf_ref, kdb_ref,
                     egf_ref, egb_ref, mask_ref, of_ref, ob_ref, s_ref):
    @pl.when(pl.program_id(0) == 0)
    def _():
        s_ref[...] = jnp.zeros_like(s_ref)

    dirs = ((uf_ref, wf_ref, qkf_ref, qgf_ref, kdf_ref, egf_ref, of_ref),
            (ub_ref, wb_ref, qkb_ref, qgb_ref, kdb_ref, egb_ref, ob_ref))
    mask_b = mask_ref[...]
    mask_f = mask_b.astype(F32)
    chains = [(b, d) + refs for b in range(uf_ref.shape[0]) for d, refs in enumerate(dirs)]
    olds = [s_ref[b, d] for b, d, *_ in chains]
    wss = [_dot(jnp.concatenate([w_ref[b], qg_ref[b]], axis=0), s.astype(BF16))
           for (b, d, u_ref, w_ref, qk_ref, qg_ref, kdt_ref, eg_ref, o_ref), s in zip(chains, olds)]
    vbs = [(c[2][c[0]] - ws[:CH]).astype(BF16) for c, ws in zip(chains, wss)]
    for (b, d, u_ref, w_ref, qk_ref, qg_ref, kdt_ref, eg_ref, o_ref), ws, vb in zip(chains, wss, vbs):
        v_bd = jnp.concatenate([vb] * NH, axis=0) * mask_b
        o_ref[b] = ws[CH:] + _dot(qk_ref[b], v_bd)
    for (b, d, u_ref, w_ref, qk_ref, qg_ref, kdt_ref, eg_ref, o_ref), s, vb in zip(chains, olds, vbs):
        eg = eg_ref[b][0:1, :]
        e_row = jnp.concatenate(
            [jnp.broadcast_to(eg[:, 4 * d + h:4 * d + h + 1], (1, HEAD)) for h in range(NH)], axis=1)
        s_ref[b, d] = s * e_row + _dot(kdt_ref[b, 0], vb) * mask_f


def _gdn_scan(loc, head_mask, n_ctx):
    uf, ub, wf, wb, qkf, qkb, qgf, qgb, kdf, kdb, eg = loc
    b, t, _ = uf.shape
    nc = t // CH
    fwd = lambda n: (0, n, 0)
    bwd = lambda n: (0, _rev_index(n, n_ctx, nc), 0)
    fwd4 = lambda n: (0, n, 0, 0)
    bwd4 = lambda n: (0, _rev_index(n, n_ctx, nc), 0, 0)
    blk = lambda im, w=WM: pl.BlockSpec((b, CH, w), im)
    kdt = lambda im: pl.BlockSpec((b, 1, WM, CH), im)
    return pl.pallas_call(
        _gdn_scan_kernel,
        grid=(nc,),
        in_specs=[blk(fwd), blk(bwd)] * 4 + [kdt(fwd4), kdt(bwd4), blk(fwd, 128), blk(bwd, 128),
                                              pl.BlockSpec(head_mask.shape, lambda n: (0, 0))],
        out_specs=[blk(fwd), blk(bwd)],
        out_shape=[jax.ShapeDtypeStruct((b, t, WM), F32)] * 2,
        scratch_shapes=[pltpu.VMEM((b, 2, WM, WM), F32)],
        compiler_params=_cp("arbitrary"),
        name="gdn_scan",
    )(uf, ub, wf, wb, qkf, qkb, qgf, qgb, kdf, kdb, eg, eg, head_mask)


HG_LEVELS = (32, 16, 8, 4, 2, 1)


def _hgrn_constants():
    idx = np.arange(CH)
    i, t = idx[:, None], idx[None, :]
    blocks = [(t <= i), (t > i)]
    masks = []
    for s in HG_LEVELS:
        m = (idx // (2 * s)) * 2 * s + s
        sec = (idx % (2 * s)) >= s
        mi = m[:, None]
        blocks.append(sec[:, None] & (t >= mi) & (t <= i))
        blocks.append((~sec)[:, None] & (t > i) & (t <= mi - 1))
        same = (idx[:, None] // (2 * s)) == (idx[None, :] // (2 * s))
        masks.append(same & sec[:, None] & (~sec)[None, :])
    mall_f = np.concatenate(blocks, axis=0).astype(np.float32)
    masks_f = np.stack(masks).astype(np.float32)
    nb = len(blocks)
    mall_b = mall_f.reshape(nb, CH, CH)[:, ::-1, ::-1].reshape(nb * CH, CH)
    masks_b = masks_f[:, ::-1, ::-1]
    eye = np.eye(TT // CH, dtype=np.float32)
    bd = lambda m: np.stack([np.kron(eye, m[lv]) for lv in range(len(HG_LEVELS))])
    return np.stack([mall_f, mall_b]), np.stack([bd(masks_f), bd(masks_b)])


def _hgrn_local_kernel(ph_ref, lbp_ref, mall_ref, mask_ref, ind_ref, ones_ref,
                       oi_ref, qgf_ref, qgb_ref, kvf_ref, kvb_ref, ecf_ref, ecb_ref):
    ph = ph_ref[0]
    lbp = lbp_ref[...]
    ind = ind_ref[...]
    ones = ones_ref[...]
    log_lb, log_1m_lb, one_m_lb = lbp[0:1], lbp[1:2], lbp[2:3]
    q = _silu(ph[:, :WM])
    v = ph[:, 3 * WM:4 * WM]
    ncl = TT // CH
    o_sum = None
    for d, (qg_ref, kv_ref, ec_ref) in enumerate(((qgf_ref, kvf_ref, ecf_ref), (qgb_ref, kvb_ref, ecb_ref))):
        fz = ph[:, WM * (1 + d):WM * (2 + d)]
        lsig = jnp.minimum(fz, 0.0) - jnp.log1p(jnp.exp(-jnp.abs(fz)))
        bb = log_1m_lb + lsig
        logf = jnp.maximum(log_lb, bb) + jnp.log1p(jnp.exp(-jnp.abs(log_lb - bb)))
        k = one_m_lb / (1.0 + jnp.exp(fz))
        e_c = [jnp.exp(_sum01_l(mall_ref[d], logf[c * CH:(c + 1) * CH], 2)) for c in range(ncl)]
        blk = lambda r: jnp.concatenate([e_c[c][r * CH:(r + 1) * CH] for c in range(ncl)], axis=0)
        qg_ref[0] = (q * blk(0)).astype(BF16)
        kd = k * blk(1)
        o_d = _sum01_r(q * k, ind, 2) * v
        q_lv = [q * blk(2 + 2 * lv) for lv in range(len(HG_LEVELS))]
        k_lv = [k * blk(3 + 2 * lv) for lv in range(len(HG_LEVELS))]
        heads = [slice(h * HEAD, (h + 1) * HEAD) for h in range(NH)]
        atts = [None] * NH
        for lv in range(len(HG_LEVELS)):
            for h, sl in enumerate(heads):
                term = _dot_nt(q_lv[lv][:, sl], k_lv[lv][:, sl]) * mask_ref[d, lv]
                atts[h] = term if atts[h] is None else atts[h] + term
        o_d = o_d + jnp.concatenate([_dot(atts[h], v[:, sl]) for h, sl in enumerate(heads)], axis=1)
        o_sum = o_d if o_sum is None else o_sum + o_d
        for c in range(ncl):
            rows = slice(c * CH, (c + 1) * CH)
            kv = _dot_tn(kd[rows], v[rows])
            tot = _sum01_r(logf[rows], ones, 3, _dot_tn)
            kv_ref[0, c] = jnp.concatenate([kv[h * HEAD:(h + 1) * HEAD, h * HEAD:(h + 1) * HEAD] for h in range(NH)], axis=1)
            ec_ref[0, c] = jnp.concatenate([jnp.exp(tot[h * HEAD:(h + 1) * HEAD]) for h in range(NH)], axis=1)
    oi_ref[0] = o_sum


def _hgrn_local(ph, lbp, mall, masks, ind_sum, ones_c):
    b, t, _ = ph.shape
    ncl = TT // CH
    full = lambda a: pl.BlockSpec(a.shape, lambda bi, i: (0,) * a.ndim)
    tile = pl.BlockSpec((1, TT, WM), lambda bi, i: (bi, i, 0))
    st_spec = pl.BlockSpec((1, ncl, HEAD, WM), lambda bi, i: (bi, i, 0, 0))
    st_shape = jax.ShapeDtypeStruct((b, t // CH, HEAD, WM), F32)
    return pl.pallas_call(
        _hgrn_local_kernel,
        grid=(b, t // TT),
        in_specs=[pl.BlockSpec((1, TT, W_PH), lambda bi, i: (bi, i, 0)),
                  full(lbp), full(mall), full(masks), full(ind_sum), full(ones_c)],
        out_specs=[tile, tile, tile, st_spec, st_spec, st_spec, st_spec],
        out_shape=[jax.ShapeDtypeStruct((b, t, WM), F32), jax.ShapeDtypeStruct((b, t, WM), BF16),
                   jax.ShapeDtypeStruct((b, t, WM), BF16), st_shape, st_shape, st_shape, st_shape],
        compiler_params=_cp("parallel", "parallel"),
        name="hgrn_local",
    )(ph, lbp, mall, masks, ind_sum, ones_c)


def _hgrn_scan_kernel(qgf_ref, qgb_ref, kvf_ref, kvb_ref, ecf_ref, ecb_ref, of_ref, ob_ref, s_ref):
    @pl.when(pl.program_id(0) == 0)
    def _():
        s_ref[...] = jnp.zeros_like(s_ref)

    dirs = ((qgf_ref, kvf_ref, ecf_ref, of_ref), (qgb_ref, kvb_ref, ecb_ref, ob_ref))
    for b in range(qgf_ref.shape[0]):
        for d, (qg_ref, kv_ref, ec_ref, o_ref) in enumerate(dirs):
            s = s_ref[b, d]
            sb = s.astype(BF16)
            qg = qg_ref[b]
            o_ref[b] = jnp.concatenate(
                [_dot(qg[:, h * HEAD:(h + 1) * HEAD], sb[:, h * HEAD:(h + 1) * HEAD]) for h in range(NH)], axis=1)
            s_ref[b, d] = s * ec_ref[b, 0] + kv_ref[b, 0]


def _hgrn_scan(qgf, qgb, kvf, kvb, ecf, ecb, n_ctx):
    b, t, _ = qgf.shape
    nc = t // CH
    fwd3 = lambda n: (0, n, 0)
    bwd3 = lambda n: (0, _rev_index(n, n_ctx, nc), 0)
    fwd4 = lambda n: (0, n, 0, 0)
    bwd4 = lambda n: (0, _rev_index(n, n_ctx, nc), 0, 0)
    tok = lambda im: pl.BlockSpec((b, CH, WM), im)
    st = lambda im: pl.BlockSpec((b, 1, HEAD, WM), im)
    return pl.pallas_call(
        _hgrn_scan_kernel,
        grid=(nc,),
        in_specs=[tok(fwd3), tok(bwd3), st(fwd4), st(bwd4), st(fwd4), st(bwd4)],
        out_specs=[tok(fwd3), tok(bwd3)],
        out_shape=[jax.ShapeDtypeStruct((b, t, WM), F32)] * 2,
        scratch_shapes=[pltpu.VMEM((b, 2, HEAD, WM), F32)],
        compiler_params=_cp("arbitrary"),
        name="hgrn_scan",
    )(qgf, qgb, kvf, kvb, ecf, ecb)


def _ret_direction(d, pr, cos, sin, dec_ref, qd, kd, cdec_ref, s_ref):
    q = (pr[:, :WM] * cos + pr[:, 4 * WM:5 * WM] * sin)
    k = (pr[:, WM:2 * WM] * cos + pr[:, 5 * WM:6 * WM] * sin) * HEAD ** -0.5
    v = pr[:, 2 * WM:3 * WM]
    q_in = q * qd
    k_in = k * kd
    outs = []
    for h in range(NH):
        sl = slice(h * HEAD, (h + 1) * HEAD)
        s = s_ref[d, h]
        att = _dot_nt(q[:, sl], k[:, sl]) * dec_ref[d, h]
        outs.append(_dot(att, v[:, sl]) + _dot(q_in[:, sl], s))
        s_ref[d, h] = s * cdec_ref[d, h] + _dot_tn(k_in[:, sl], v[:, sl])
    return jnp.concatenate(outs, axis=1)


def _ret_scan_kernel(cdec_ref, pf_ref, pb_ref, cf_ref, sf_ref, cb_ref, sb_ref, dec_ref, qd_ref, kd_ref,
                     of_ref, ob_ref, s_ref):
    @pl.when(pl.program_id(1) == 0)
    def _():
        s_ref[...] = jnp.zeros_like(s_ref)

    of_ref[0] = _ret_direction(0, pf_ref[0], cf_ref[...], sf_ref[...], dec_ref, qd_ref[0], kd_ref[0], cdec_ref, s_ref)
    ob_ref[0] = _ret_direction(1, pb_ref[0], cb_ref[...], sb_ref[...], dec_ref, qd_ref[1], kd_ref[1], cdec_ref, s_ref)


def _ret_scan(pr, cos_t, sin_t, dec, qdec, kdec, cdec):
    b, t, _ = pr.shape
    nt = t // TT
    fwd3 = lambda bi, n: (bi, n, 0)
    bwd3 = lambda bi, n: (bi, _rev_index(n, 1, nt), 0)
    fwd2 = lambda bi, n: (n, 0)
    bwd2 = lambda bi, n: (_rev_index(n, 1, nt), 0)
    return pl.pallas_call(
        _ret_scan_kernel,
        grid=(b, nt),
        in_specs=[pl.BlockSpec(memory_space=pltpu.SMEM),
                  pl.BlockSpec((1, TT, W_PR), fwd3), pl.BlockSpec((1, TT, W_PR), bwd3),
                  pl.BlockSpec((TT, WM), fwd2), pl.BlockSpec((TT, WM), fwd2),
                  pl.BlockSpec((TT, WM), bwd2), pl.BlockSpec((TT, WM), bwd2),
                  pl.BlockSpec(dec.shape, lambda bi, n: (0, 0, 0, 0)),
                  pl.BlockSpec(qdec.shape, lambda bi, n: (0, 0, 0)),
                  pl.BlockSpec(kdec.shape, lambda bi, n: (0, 0, 0))],
        out_specs=[pl.BlockSpec((1, TT, WM), fwd3), pl.BlockSpec((1, TT, WM), bwd3)],
        out_shape=[jax.ShapeDtypeStruct((b, t, WM), F32)] * 2,
        scratch_shapes=[pltpu.VMEM((2, NH, HEAD, HEAD), F32)],
        compiler_params=_cp("parallel", "arbitrary"),
        name="ret_scan",
    )(cdec, pr, pr, cos_t, sin_t, cos_t, sin_t, dec, qdec, kdec)


def _s5_kernel(u_ref, wz_ref, wy_ref, a1_ref, a2_ref, y_ref, z_ref, hp_ref, *, n_ctx):
    nch = u_ref.shape[2]
    w_in = u_ref.shape[3]
    for g in range(S5GB):
        z_ref[g] = _dot_pieces([u_ref[0, g]], [wz_ref[0, g], wz_ref[1, g]])
    a1 = a1_ref[...]
    a2 = a2_ref[...]

    def step(s, hs):
        tiles = (s, _rev_index(s, n_ctx // 8, nch // 8))
        new = []
        for g in range(S5GB):
            for d in range(2):
                h, hx = hs[2 * g + d]
                r0 = pl.multiple_of(tiles[d] * 8, 8)
                lanes = slice(128 * d, 128 * (d + 1))
                z = z_ref[g, pl.ds(r0, 8), lanes]
                zx = pltpu.roll(z, S5P, 1)
                c1, c2 = a1[g, d:d + 1], a2[g, d:d + 1]
                entering = [None] * 8
                for j in (range(8) if d == 0 else range(7, -1, -1)):
                    entering[j] = h
                    h, hx = c1 * h + c2 * hx + z[j:j + 1], c1 * hx - c2 * h + zx[j:j + 1]
                hp_ref[g, pl.ds(r0, 8), lanes] = jnp.concatenate(entering, axis=0)
                new.append((h, hx))
        return tuple(new)

    zero = jnp.zeros((1, 128), F32)
    lax.fori_loop(0, nch // 8, step, tuple((zero, zero) for _ in range(2 * S5GB)))
    for g in range(S5GB):
        y = (_dot_pieces([u_ref[0, g]], [wy_ref[0, g, :w_in], wy_ref[1, g, :w_in]])
             + _dot_pieces(_pieces(hp_ref[g], 2), [wy_ref[0, g, w_in:], wy_ref[1, g, w_in:]]))
        y_ref[0, g] = y.astype(y_ref.dtype)


def _s5_scan(u4, wz, wy, a1, a2, n_ctx):
    b, g, nch, w = u4.shape
    return pl.pallas_call(
        functools.partial(_s5_kernel, n_ctx=n_ctx),
        grid=(b, g // S5GB),
        in_specs=[pl.BlockSpec((1, S5GB, nch, w), lambda bi, gi: (bi, gi, 0, 0)),
                  pl.BlockSpec((2, S5GB, w, w), lambda bi, gi: (0, gi, 0, 0)),
                  pl.BlockSpec((2, S5GB, 2 * w, w), lambda bi, gi: (0, gi, 0, 0)),
                  pl.BlockSpec((S5GB, 2, 128), lambda bi, gi: (gi, 0, 0)),
                  pl.BlockSpec((S5GB, 2, 128), lambda bi, gi: (gi, 0, 0))],
        out_specs=pl.BlockSpec((1, S5GB, nch, w), lambda bi, gi: (bi, gi, 0, 0)),
        out_shape=jax.ShapeDtypeStruct(u4.shape, u4.dtype),
        scratch_shapes=[pltpu.VMEM((S5GB, nch, w), F32), pltpu.VMEM((S5GB, nch, w), F32)],
        compiler_params=_cp("parallel", "parallel"),
        name="s5_scan",
    )(u4, wz, wy, a1, a2)


def _s5_weights(a_re, a_im, log_step, b_re, b_im, c_re, c_im):
    step = jnp.exp(log_step)[..., None]
    e_re, e_im = a_re * step, a_im * step
    def lam_pow(n):
        n = n[..., None, None, None] if n.ndim else n
        mag = jnp.exp(e_re * n)
        return mag * jnp.cos(e_im * n), mag * jnp.sin(e_im * n)
    l1r, l1i = lam_pow(jnp.asarray(1.0, F32))
    den = a_re * a_re + a_im * a_im
    fr = ((l1r - 1.0) * a_re + l1i * a_im) / den
    fi = (l1i * a_re - (l1r - 1.0) * a_im) / den
    bbr = fr[..., None] * b_re - fi[..., None] * b_im
    bbi = fr[..., None] * b_im + fi[..., None] * b_re
    j = jnp.arange(S5C, dtype=F32)
    es = functools.partial(jnp.einsum, precision=HI)

    def build(d):
        cr, ci = c_re[d], c_im[d]
        br, bi = bbr[d], bbi[d]
        sel = lambda x: x[:, d] if x.ndim == 4 else x
        pr, pi = lam_pow(jnp.arange(S5C + 1, dtype=F32))
        pr, pi = pr[:, d], pi[:, d]
        cl_r = cr[None] * pr[:, :, None, :] - ci[None] * pi[:, :, None, :]
        cl_i = cr[None] * pi[:, :, None, :] + ci[None] * pr[:, :, None, :]
        kk = es('ngop,gpi->ngoi', cl_r[:S5C], br) - es('ngop,gpi->ngoi', cl_i[:S5C], bi)
        ji, jo = np.arange(S5C)[:, None], np.arange(S5C)[None, :]
        lag = (jo - ji) if d == 0 else (ji - jo)
        place = jnp.asarray(lag[None] == np.arange(S5C)[:, None, None], F32)
        kt = es('lij,lgoc->ijgoc', place, kk)
        toep = kt.transpose(2, 0, 4, 1, 3).reshape(S5G, S5C * 16, S5C * 16)
        pick_m = (lambda x: x[1:S5C + 1]) if d == 0 else (lambda x: x[1:S5C + 1][::-1])
        wo_r = pick_m(cl_r).transpose(1, 3, 0, 2).reshape(S5G, S5P, S5C * 16)
        wo_i = -pick_m(cl_i).transpose(1, 3, 0, 2).reshape(S5G, S5P, S5C * 16)
        wout = jnp.concatenate([wo_r, wo_i], axis=1)
        pick_e = (lambda x: x[:S5C][::-1]) if d == 0 else (lambda x: x[:S5C])
        lr, li = pick_e(pr), pick_e(pi)
        wi_r = (lr[..., None] * br[None] - li[..., None] * bi[None])
        wi_i = (lr[..., None] * bi[None] + li[..., None] * br[None])
        win = jnp.concatenate([wi_r.transpose(1, 0, 3, 2).reshape(S5G, S5C * 16, S5P),
                               wi_i.transpose(1, 0, 3, 2).reshape(S5G, S5C * 16, S5P)], axis=2)
        ar, ai = pr[S5C], pi[S5C]
        a1 = jnp.concatenate([ar, ar], axis=1)
        a2 = jnp.concatenate([-ai, ai], axis=1)
        return toep, wout, win, a1, a2

    tf, of, wf, a1f, a2f = build(0)
    tb, ob, wb, a1b, a2b = build(1)
    wz = jnp.concatenate([wf, wb], axis=2)
    wy = jnp.concatenate([tf + tb, of, ob], axis=1)
    split = lambda w: jnp.stack([w.astype(BF16), (w - w.astype(BF16).astype(F32)).astype(BF16)])
    return split(wz), split(wy), jnp.stack([a1f, a1b], axis=1), jnp.stack([a2f, a2b], axis=1)


def _gelu_tanh(x):
    return 0.5 * x * (1.0 + jnp.tanh(math.sqrt(2.0 / math.pi) * (x + 0.044715 * x * x * x)))


def _outproj_kernel(x_ref, mod_ref, gf_ref, gb_ref, gz_ref, ys_ref, us_ref, hi_ref, hf_ref, hb_ref, hg_ref,
                    rf_ref, rb_ref, rg_ref, vec_ref, glu_ref, ind_ref, w_ref, ln_ref, o_ref, *, alpha):
    ind = ind_ref[...]
    vec = vec_ref[...]

    def head_rms(o):
        return o * lax.rsqrt(_sum01_r(o * o, ind, 2) * (1.0 / HEAD) + RMS_EPS)

    m_gdn = head_rms(gf_ref[0] + gb_ref[0]) * vec[0:1] * _silu(gz_ref[0])
    u = us_ref[0].astype(F32)
    ys = _gelu_tanh(ys_ref[0].astype(F32) + vec[2:3] * u)
    m_s5 = ys * _sigmoid(_dot(ys, glu_ref[...]) + vec[3:4])
    m_hg = head_rms(hi_ref[0] + hf_ref[0] + hb_ref[0]) * vec[1:2] * _silu(hg_ref[0])
    m_rt = head_rms(rf_ref[0] + rb_ref[0]) * _silu(rg_ref[0])
    acc = None
    for k, mk in enumerate((m_gdn, m_s5, m_hg, m_rt)):
        part = _dot(mk.astype(BF16), w_ref[k * WM:(k + 1) * WM, :])
        acc = part if acc is None else acc + part
    m = mod_ref[0, 0]
    ln = ln_ref[...]
    o_ref[0] = _layer_norm(alpha * x_ref[0] + m[2:3] * acc, ln[0:1], ln[1:2])


def _outproj(xs, mod, g_of, g_ob, pg, ys, ps, h_oi, h_of, h_ob, ph, r_of, r_ob, pr, vec, glu_w, ind_sum, w_out, ln,
             alpha):
    b, t, d = xs.shape
    tile = lambda c: pl.BlockSpec((1, TT, WM), lambda bi, i, c=c: (bi, i, c))
    full = lambda a: pl.BlockSpec(a.shape, lambda bi, i: (0,) * a.ndim)
    return pl.pallas_call(
        functools.partial(_outproj_kernel, alpha=alpha),
        grid=(b, t // TT),
        in_specs=[pl.BlockSpec((1, TT, d), lambda bi, i: (bi, i, 0)),
                  pl.BlockSpec((1, 1, 6, d), lambda bi, i: (bi, jnp.minimum(i, 1), 0, 0)),
                  tile(0), tile(0), tile(3), tile(0), tile(0), tile(0), tile(0), tile(0), tile(4),
                  tile(0), tile(0), tile(3),
                  full(vec), full(glu_w), full(ind_sum), full(w_out), full(ln)],
        out_specs=pl.BlockSpec((1, TT, d), lambda bi, i: (bi, i, 0)),
        out_shape=jax.ShapeDtypeStruct(xs.shape, F32),
        compiler_params=_cp("parallel", "parallel"),
        name="outproj",
    )(xs, mod, g_of, g_ob, pg, ys, ps, h_oi, h_of, h_ob, ph, r_of, r_ob, pr, vec, glu_w, ind_sum, w_out, ln)


def _ffn_kernel(x_ref, mod_ref, w1_ref, w3_ref, w2_ref, ln_ref, o_ref, *, alpha):
    m = mod_ref[0, 0]
    x = x_ref[0]
    h = (x * (1.0 + m[4:5]) + m[3:4]).astype(BF16)
    act = (_silu(_dot(h, w1_ref[...])) * _dot(h, w3_ref[...])).astype(BF16)
    y = _dot(act, w2_ref[...])
    ln = ln_ref[...]
    o_ref[0] = _layer_norm(alpha * x + m[5:6] * y, ln[0:1], ln[1:2])


def _ffn(xs, mod, w1, w3, w2, ln, alpha):
    b, t, d = xs.shape
    full = lambda a: pl.BlockSpec(a.shape, lambda bi, i: (0,) * a.ndim)
    return pl.pallas_call(
        functools.partial(_ffn_kernel, alpha=alpha),
        grid=(b, t // TT),
        in_specs=[pl.BlockSpec((1, TT, d), lambda bi, i: (bi, i, 0)),
                  pl.BlockSpec((1, 1, 6, d), lambda bi, i: (bi, jnp.minimum(i, 1), 0, 0)),
                  full(w1), full(w3), full(w2), full(ln)],
        out_specs=pl.BlockSpec((1, TT, d), lambda bi, i: (bi, i, 0)),
        out_shape=jax.ShapeDtypeStruct(xs.shape, F32),
        compiler_params=_cp("parallel", "parallel"),
        name="ffn_dense",
    )(xs, mod, w1, w3, w2, ln)


def _router_kernel(x_ref, mod_ref, r_ref, h_ref, rt_ref):
    m = mod_ref[0, 0]
    h = x_ref[0] * (1.0 + m[4:5]) + m[3:4]
    h_ref[0] = h
    lane = lax.broadcasted_iota(jnp.int32, (TT, 128), 1)
    logits = jnp.where(lane < N_EXP, _dot(h, r_ref[...], HI), -jnp.inf)
    m1 = jnp.max(logits, axis=-1, keepdims=True)
    i1 = jnp.min(jnp.where(logits == m1, lane, 128), axis=-1, keepdims=True)
    rest = jnp.where(lane == i1, -jnp.inf, logits)
    m2 = jnp.max(rest, axis=-1, keepdims=True)
    i2 = jnp.min(jnp.where(rest == m2, lane, 128), axis=-1, keepdims=True)
    e = jnp.exp(m2 - m1)
    g1 = 1.0 / (1.0 + e)
    g2 = e / (1.0 + e)
    rt_ref[0] = jnp.where(lane == 0, i1.astype(F32),
                          jnp.where(lane == 1, i2.astype(F32),
                                    jnp.where(lane == 2, g1, jnp.where(lane == 3, g2, 0.0))))


def _router(xs, mod, router_pad):
    b, t, d = xs.shape
    return pl.pallas_call(
        _router_kernel,
        grid=(b, t // TT),
        in_specs=[pl.BlockSpec((1, TT, d), lambda bi, i: (bi, i, 0)),
                  pl.BlockSpec((1, 1, 6, d), lambda bi, i: (bi, jnp.minimum(i, 1), 0, 0)),
                  pl.BlockSpec(router_pad.shape, lambda bi, i: (0, 0))],
        out_specs=[pl.BlockSpec((1, TT, d), lambda bi, i: (bi, i, 0)),
                   pl.BlockSpec((1, TT, 128), lambda bi, i: (bi, i, 0))],
        out_shape=[jax.ShapeDtypeStruct(xs.shape, F32), jax.ShapeDtypeStruct((b, t, 128), F32)],
        compiler_params=_cp("parallel", "parallel"),
        name="moe_router",
    )(xs, mod, router_pad)


def _row_copy(src_hbm, row, dst_ref, slot, sem):
    return pltpu.make_async_copy(src_hbm.at[pl.ds(row, 1), :], dst_ref.at[pl.ds(slot, 1), :], sem)


def _dispatch_kernel(dest_ref, h_ref, init_hbm, o_hbm, sem, *, tiles_per_batch):
    del init_hbm
    tile = pl.program_id(0) * tiles_per_batch + pl.program_id(1)
    base = tile * (2 * TT)

    def start(r, c):
        t = lax.rem(r, TT)
        pltpu.make_async_copy(h_ref.at[0, pl.ds(t, 1), :], o_hbm.at[pl.ds(dest_ref[base + r], 1), :], sem).start()
        return c

    lax.fori_loop(0, 2 * TT, start, 0, unroll=8)
    for _ in range(2):
        pltpu.make_async_copy(h_ref.at[0], o_hbm.at[pl.ds(0, TT), :], sem).wait()


def _dispatch_rows(h, dest_tiles, n_rows):
    b, t, d = h.shape
    return pl.pallas_call(
        functools.partial(_dispatch_kernel, tiles_per_batch=t // TT),
        grid_spec=pltpu.PrefetchScalarGridSpec(
            num_scalar_prefetch=1,
            grid=(b, t // TT),
            in_specs=[pl.BlockSpec((1, TT, d), lambda bi, i, dr: (bi, i, 0)),
                      pl.BlockSpec(memory_space=pl.ANY)],
            out_specs=pl.BlockSpec(memory_space=pl.ANY),
            scratch_shapes=[pltpu.SemaphoreType.DMA(())]),
        out_shape=jax.ShapeDtypeStruct((n_rows, d), h.dtype),
        input_output_aliases={2: 0},
        compiler_params=_cp("arbitrary", "arbitrary"),
        name="moe_dispatch",
    )(dest_tiles, h, jnp.zeros((n_rows, d), h.dtype))


def _experts_kernel(be_ref, x_ref, w1_ref, w3_ref, w2_ref, y_ref):
    f = pl.program_id(1)
    x = x_ref[...].astype(BF16)
    act = (_silu(_dot(x, w1_ref[0])) * _dot(x, w3_ref[0])).astype(BF16)
    y = _dot(act, w2_ref[0])

    @pl.when(f == 0)
    def _():
        y_ref[...] = y

    @pl.when(f != 0)
    def _():
        y_ref[...] += y


def _experts(xs_sorted, block_expert, w1, w3, w2):
    n_rows, d = xs_sorted.shape
    ff = w1.shape[2]
    nf = 2
    tf = ff // nf
    return pl.pallas_call(
        _experts_kernel,
        grid_spec=pltpu.PrefetchScalarGridSpec(
            num_scalar_prefetch=1,
            grid=(n_rows // MOE_RB, nf),
            in_specs=[pl.BlockSpec((MOE_RB, d), lambda j, f, be: (j, 0)),
                      pl.BlockSpec((1, d, tf), lambda j, f, be: (be[j], 0, f)),
                      pl.BlockSpec((1, d, tf), lambda j, f, be: (be[j], 0, f)),
                      pl.BlockSpec((1, tf, d), lambda j, f, be: (be[j], f, 0))],
            out_specs=pl.BlockSpec((MOE_RB, d), lambda j, f, be: (j, 0))),
        out_shape=jax.ShapeDtypeStruct((n_rows, d), F32),
        compiler_params=_cp("parallel", "arbitrary"),
        name="moe_experts",
    )(block_expert, xs_sorted, w1, w3, w2)


def _combine_kernel(dest_ref, x_ref, mod_ref, rt_ref, ln_ref, y_hbm, o_ref, buf, sem, *, alpha, tiles_per_batch,
                    first_tile):
    tile = pl.program_id(0) * tiles_per_batch + pl.program_id(1) + first_tile
    base = tile * (2 * TT)

    def start(r, c):
        _row_copy(y_hbm, dest_ref[base + r], buf, r, sem).start()
        return c

    lax.fori_loop(0, 2 * TT, start, 0, unroll=8)
    pltpu.make_async_copy(y_hbm.at[pl.ds(0, 2 * TT), :], buf, sem).wait()
    rt = rt_ref[0]
    y = rt[:, 2:3] * buf[0:TT, :] + rt[:, 3:4] * buf[TT:2 * TT, :]
    m = mod_ref[0, 0]
    ln = ln_ref[...]
    o_ref[0] = _layer_norm(alpha * x_ref[0] + m[5:6] * y, ln[0:1], ln[1:2])


def _combine(xs, mod, rt, ln, ys_sorted, dest_tiles, alpha, first_tile):
    b, t, d = xs.shape
    ft = first_tile
    return pl.pallas_call(
        functools.partial(_combine_kernel, alpha=alpha, tiles_per_batch=t // TT, first_tile=ft),
        grid_spec=pltpu.PrefetchScalarGridSpec(
            num_scalar_prefetch=1,
            grid=(b, t // TT - ft),
            in_specs=[pl.BlockSpec((1, TT, d), lambda bi, i, dr: (bi, i + ft, 0)),
                      pl.BlockSpec((1, 1, 6, d), lambda bi, i, dr: (bi, jnp.minimum(i + ft, 1), 0, 0)),
                      pl.BlockSpec((1, TT, 128), lambda bi, i, dr: (bi, i + ft, 0)),
                      pl.BlockSpec(ln.shape, lambda bi, i, dr: (0, 0)),
                      pl.BlockSpec(memory_space=pl.ANY)],
            out_specs=pl.BlockSpec((1, TT, d), lambda bi, i, dr: (bi, i, 0)),
            scratch_shapes=[pltpu.VMEM((2 * TT, d), F32), pltpu.SemaphoreType.DMA(())]),
        out_shape=jax.ShapeDtypeStruct((b, t - ft * TT, d), F32),
        compiler_params=_cp("arbitrary", "arbitrary"),
        name="moe_combine",
    )(dest_tiles, xs, mod, rt, ln, ys_sorted)


def _moe(xs, mod, router_pad, w1, w3, w2, ln, alpha, first_tile):
    b, t, d = xs.shape
    n_tok = b * t
    h, rt = _router(xs, mod, router_pad)
    e_idx = rt[..., 0:2].astype(jnp.int32).reshape(n_tok * 2)
    onehot = (e_idx[:, None] == jnp.arange(N_EXP, dtype=jnp.int32)[None, :]).astype(jnp.int32)
    seg = 2 * TT
    local = jnp.einsum('ij,tjk->tik', jnp.asarray(np.tril(np.ones((seg, seg))), F32),
                       onehot.astype(F32).reshape(-1, seg, N_EXP), precision=HI)
    seg_tot = local[:, -1, :]
    seg_off = jnp.cumsum(seg_tot, axis=0) - seg_tot
    csum = (local + seg_off[:, None, :]).astype(jnp.int32).reshape(n_tok * 2, N_EXP)
    counts = csum[-1]
    padded = (counts + MOE_RB - 1) // MOE_RB * MOE_RB
    pad_end = jnp.cumsum(padded)
    pad_start = pad_end - padded
    dest = jnp.sum(onehot * (csum - 1 + pad_start[None, :]), axis=1)
    n_blocks = -(-(n_tok * 2) // MOE_RB) + N_EXP
    n_rows = n_blocks * MOE_RB
    block_expert = jnp.minimum(
        jnp.sum(jnp.arange(n_blocks, dtype=jnp.int32)[:, None] * MOE_RB >= pad_end[None, :], axis=1),
        N_EXP - 1).astype(jnp.int32)
    dest_tiles = dest.reshape(n_tok // TT, TT, 2).transpose(0, 2, 1).reshape(n_tok * 2)
    xs_sorted = _dispatch_rows(h, dest_tiles, n_rows)
    ys_sorted = _experts(xs_sorted, block_expert, w1, w3, w2)
    return _combine(xs, mod, rt, ln, ys_sorted, dest_tiles, alpha, first_tile)


def _ret_constants(decay_param):
    log_gamma = -jnp.exp(decay_param)
    idx = jnp.arange(TT, dtype=F32)
    diff = idx[:, None] - idx[None, :]
    lg = log_gamma[:, :, None, None]
    dec_f = jnp.exp(jnp.where(diff >= 0, diff * lg[0], -jnp.inf))
    dec_b = jnp.exp(jnp.where(diff <= 0, -diff * lg[1], -jnp.inf))
    dec = jnp.stack([dec_f, dec_b])
    rep = lambda a: jnp.repeat(a, HEAD, axis=-1)
    qdec = jnp.stack([rep(jnp.exp((idx[:, None] + 1.0) * log_gamma[0][None, :])),
                      rep(jnp.exp((TT - idx[:, None]) * log_gamma[1][None, :]))])
    kdec = jnp.stack([rep(jnp.exp((TT - 1.0 - idx[:, None]) * log_gamma[0][None, :])),
                      rep(jnp.exp(idx[:, None] * log_gamma[1][None, :]))])
    cdec = jnp.exp(TT * log_gamma)
    return dec, qdec, kdec, cdec


def _rotary_tables(n_ctx_tok, n_lat, grid_w):
    rows = jnp.repeat(jnp.arange(n_lat // grid_w, dtype=F32), grid_w)
    cols = jnp.tile(jnp.arange(grid_w, dtype=F32), n_lat // grid_w)
    quarter = HEAD // 4
    inv_freq = ROPE_BASE ** (-jnp.arange(quarter, dtype=F32) / quarter)
    ang = jnp.concatenate([rows[:, None] * inv_freq, cols[:, None] * inv_freq], axis=-1)
    cos, sin = jnp.cos(ang), jnp.sin(ang)
    cos_h = jnp.concatenate([cos, cos], axis=-1)
    sin_h = jnp.concatenate([-sin, sin], axis=-1)
    cos_t = jnp.concatenate([jnp.ones((n_ctx_tok, HEAD), F32), cos_h], axis=0)
    sin_t = jnp.concatenate([jnp.zeros((n_ctx_tok, HEAD), F32), sin_h], axis=0)
    return jnp.tile(cos_t, (1, NH)), jnp.tile(sin_t, (1, NH))


def _swap_halves_cols(w):
    dm = w.shape[0]
    return w.reshape(dm, NH, 2, HEAD // 2)[:, :, ::-1, :].reshape(dm, WM)


def _prep_w_in(w):
    dm = w.shape[0]
    r0 = 3344 - 768
    rq, rk = w[:, r0:r0 + WM], w[:, r0 + WM:r0 + 2 * WM]
    return jnp.concatenate([w[:, :1040], jnp.zeros((dm, W_AB - 16), w.dtype), w[:, 1040:],
                            _swap_halves_cols(rq), _swap_halves_cols(rk)], axis=1).astype(BF16)


def kernel(x, c, ctx, c_ctx, ada_w, ada_b, w_in, w_out, ln_g, ln_b, gdn_conv_w, gdn_a_log, gdn_dt_bias, gdn_norm_w, s5_a_re, s5_a_im, s5_log_step, s5_b_re, s5_b_im, s5_c_re, s5_c_im, s5_d, s5_glu_w, s5_glu_b, hgrn_lower_bounds, hgrn_norm_w, ret_decay, ffn_w1, ffn_w3, ffn_w2, moe_router, moe_w1, moe_w3, moe_w2):
    bsz, n_lat, d = x.shape
    n_ctx_tok = ctx.shape[1]
    depth = ada_w.shape[0]
    grid_w = 64
    assert n_ctx_tok == TT and n_lat % TT == 0 and bsz <= 7
    t = n_ctx_tok + n_lat
    alpha = (2.0 * depth) ** 0.25

    xs = jnp.concatenate([ctx, x], axis=1)
    cs = jnp.concatenate([c, c_ctx[None, :], jnp.zeros((8 - bsz - 1, d), F32)], axis=0)
    mod_all = _ada_mod(cs, ada_w, ada_b)
    lat_mod = mod_all[:, :bsz].reshape(depth, bsz, 6, d)
    ctx_mod = jnp.broadcast_to(mod_all[:, bsz].reshape(depth, 1, 6, d), (depth, bsz, 6, d))
    mod_tab = jnp.stack([ctx_mod, lat_mod], axis=2)

    lb_all = jnp.cumsum(jax.nn.softmax(hgrn_lower_bounds.astype(F32), axis=0), axis=0)
    lb_all = lb_all - lb_all[0]
    ind_sum = jnp.asarray(np.kron(np.eye(NH), np.ones((HEAD, HEAD))), BF16)
    chunks_eye = np.eye(TT // CH)
    tri_bd = jnp.asarray(np.kron(chunks_eye, np.tril(np.ones((CH, CH)))), BF16)
    ones_bd = jnp.asarray(np.kron(chunks_eye, np.ones((CH, CH))), BF16)
    ones_c = jnp.ones((CH, HEAD), BF16)
    mall_np, masks_np = _hgrn_constants()
    mall, masks = jnp.asarray(mall_np, BF16), jnp.asarray(masks_np)
    cos_t, sin_t = _rotary_tables(n_ctx_tok, n_lat, grid_w)
    nch = t // S5C

    for layer in range(depth):
        mod = mod_tab[layer]
        pg, pab, ps, ph, pr = _inproj(xs, mod, _prep_w_in(w_in[layer]))

        pabt = pab[..., :16].transpose(0, 2, 1)
        g_loc = _gdn_local(pg, pab, pabt, gdn_conv_w[layer], ind_sum, -jnp.exp(gdn_a_log[layer]),
                           gdn_dt_bias[layer], tri_bd, ones_bd)
        g_of, g_ob = _gdn_scan(g_loc, ind_sum, n_ctx_tok // CH)

        wz, wy, a1, a2 = _s5_weights(s5_a_re[layer], s5_a_im[layer], s5_log_step[layer], s5_b_re[layer],
                                     s5_b_im[layer], s5_c_re[layer], s5_c_im[layer])
        u4 = ps.reshape(bsz, nch, S5C, S5G, 16).transpose(0, 3, 1, 2, 4).reshape(bsz, S5G, nch, S5C * 16)
        y4 = _s5_scan(u4, wz, wy, a1, a2, n_ctx_tok // S5C)
        ys = y4.reshape(bsz, S5G, nch, S5C, 16).transpose(0, 2, 3, 1, 4).reshape(bsz, t, WM)

        lb = lb_all[layer][None, :]
        lbp = jnp.concatenate([jnp.log(lb), jnp.log1p(-lb), 1.0 - lb, jnp.zeros((5, WM), F32)], axis=0)
        h_oi, h_qgf, h_qgb, h_kvf, h_kvb, h_ecf, h_ecb = _hgrn_local(ph, lbp, mall, masks, ind_sum, ones_c)
        h_of, h_ob = _hgrn_scan(h_qgf, h_qgb, h_kvf, h_kvb, h_ecf, h_ecb, n_ctx_tok // CH)

        dec, qdec, kdec, cdec = _ret_constants(ret_decay[layer])
        r_of, r_ob = _ret_scan(pr, cos_t, sin_t, dec, qdec, kdec, cdec)

        vec = jnp.concatenate([jnp.tile(gdn_norm_w[layer], NH)[None], jnp.tile(hgrn_norm_w[layer], NH)[None],
                               s5_d[layer][None], s5_glu_b[layer][None], jnp.zeros((4, WM), F32)], axis=0)
        xs = _outproj(xs, mod, g_of, g_ob, pg, ys, ps, h_oi, h_of, h_ob, ph, r_of, r_ob, pr, vec, s5_glu_w[layer],
                      ind_sum, w_out[layer].astype(BF16), jnp.stack([ln_g[layer, 0], ln_b[layer, 0]]), alpha)

        j = layer // 2
        ln2 = jnp.stack([ln_g[layer, 1], ln_b[layer, 1]])
        if layer % 2 == 0:
            xs = _ffn(xs, mod, ffn_w1[j].astype(BF16), ffn_w3[j].astype(BF16), ffn_w2[j].astype(BF16), ln2, alpha)
        else:
            router_pad = jnp.concatenate([moe_router[j], jnp.zeros((d, 128 - N_EXP), F32)], axis=1)
            xs = _moe(xs, mod, router_pad, moe_w1[j].astype(BF16), moe_w3[j].astype(BF16),
                      moe_w2[j].astype(BF16), ln2, alpha, first_tile=int(layer == depth - 1))
    return xs if depth % 2 == 0 else xs[:, n_ctx_tok:, :]
```

```python
import functools
import math

import numpy as np
import jax
import jax.numpy as jnp
from jax import lax
from jax.experimental import pallas as pl
from jax.experimental.pallas import tpu as pltpu

F32 = jnp.float32
BF16 = jnp.bfloat16
HI = lax.Precision.HIGHEST

HEAD = 64
NH = 4
WM = NH * HEAD
TT = 256
CH = 64
S5C = 16
S5G = 16
S5P = 64
S5GB = 8
N_EXP = 8
MOE_RB = 512
LN_EPS = 1e-5
RMS_EPS = 1e-6
ROPE_BASE = 10000.0
VMEM_LIMIT = 56 * 1024 * 1024


def _cp(*sem):
    return pltpu.CompilerParams(dimension_semantics=sem, vmem_limit_bytes=VMEM_LIMIT)


def _sigmoid(x):
    return 1.0 / (1.0 + jnp.exp(-x))


def _silu(x):
    return x * _sigmoid(x)


def _softplus(x):
    return jnp.maximum(x, 0.0) + jnp.log1p(jnp.exp(-jnp.abs(x)))


def _dot(a, b, precision=None):
    return jnp.dot(a, b, preferred_element_type=F32, precision=precision)


def _dot_nt(a, b, precision=None):
    return lax.dot_general(a, b, (((1,), (1,)), ((), ())), preferred_element_type=F32, precision=precision)


def _dot_tn(a, b, precision=None):
    return lax.dot_general(a, b, (((0,), (0,)), ((), ())), preferred_element_type=F32, precision=precision)


def _layer_norm(y, g, b):
    mu = jnp.mean(y, axis=-1, keepdims=True)
    yc = y - mu
    var = jnp.mean(yc * yc, axis=-1, keepdims=True)
    return yc * lax.rsqrt(var + LN_EPS) * g + b


def _rev_index(n, n_ctx, n_all):
    return jnp.where(n < n_ctx, n_ctx - 1 - n, n_all + n_ctx - 1 - n)


def _ada_kernel(c_ref, w_ref, b_ref, o_ref):
    o_ref[0] = _dot(_silu(c_ref[...]), w_ref[0], HI) + b_ref[0]


def _ada_mod(cs, ada_w, ada_b):
    depth, d, d6 = ada_w.shape
    tn = 1024
    return pl.pallas_call(
        _ada_kernel,
        grid=(depth, d6 // tn),
        in_specs=[pl.BlockSpec((8, d), lambda l, j: (0, 0)),
                  pl.BlockSpec((1, d, tn), lambda l, j: (l, 0, j)),
                  pl.BlockSpec((1, 1, tn), lambda l, j: (l, 0, j))],
        out_specs=pl.BlockSpec((1, 8, tn), lambda l, j: (l, 0, j)),
        out_shape=jax.ShapeDtypeStruct((depth, 8, d6), F32),
        compiler_params=_cp("parallel", "parallel"),
        name="ada_mod",
    )(cs, ada_w, ada_b.reshape(depth, 1, d6))


W_PG, W_AB, W_S5, W_PH, W_PR = 1024, 128, 256, 1280, 1536
P_OFF = np.cumsum([0, W_PG, W_AB, W_S5, W_PH, W_PR])


def _inproj_kernel(x_ref, mod_ref, w_ref, pg_ref, pab_ref, ps_ref, ph_ref, pr_ref):
    m = mod_ref[0, 0]
    h = (x_ref[0] * (1.0 + m[1:2]) + m[0:1]).astype(BF16)
    for k, o_ref in enumerate((pg_ref, pab_ref, ps_ref, ph_ref, pr_ref)):
        o_ref[0] = _dot(h, w_ref[:, P_OFF[k]:P_OFF[k + 1]]).astype(o_ref.dtype)


def _inproj(xs, mod, w):
    b, t, d = xs.shape
    widths = (W_PG, W_AB, W_S5, W_PH, W_PR)
    return pl.pallas_call(
        _inproj_kernel,
        grid=(b, t // TT),
        in_specs=[pl.BlockSpec((1, TT, d), lambda bi, i: (bi, i, 0)),
                  pl.BlockSpec((1, 1, 6, d), lambda bi, i: (bi, jnp.minimum(i, 1), 0, 0)),
                  pl.BlockSpec(w.shape, lambda bi, i: (0, 0))],
        out_specs=[pl.BlockSpec((1, TT, wd), lambda bi, i: (bi, i, 0)) for wd in widths],
        out_shape=[jax.ShapeDtypeStruct((b, t, wd), BF16 if wd == W_S5 else F32) for wd in widths],
        compiler_params=_cp("parallel", "parallel"),
        name="inproj",
    )(xs, mod, w)


def _pieces(x, n):
    out, r = [], x
    for i in range(n):
        p = r.astype(BF16)
        out.append(p)
        if i + 1 < n:
            r = r - p.astype(F32)
    return out


def _dot_pieces(a_parts, b_parts, dot=_dot):
    n = max(len(a_parts), len(b_parts))
    acc = None
    for i, ap in enumerate(a_parts):
        for j, bp in enumerate(b_parts):
            if i + j < n:
                t = dot(ap, bp)
                acc = t if acc is None else acc + t
    return acc


def _sum01_l(m01, x, n):
    return _dot_pieces([m01], _pieces(x, n))


def _sum01_r(x, m01, n, dot=_dot):
    return _dot_pieces(_pieces(x, n), [m01], dot)


GDN_PIECES = 1
GDN_BASE = 16


def _unit_tri_inverse(a_list, eye, masks):
    n = GDN_PIECES
    ident = jnp.where(eye, 1.0, 0.0)
    ds = [jnp.where(masks[0], a, 0.0) for a in a_list]
    ts = [ident - d for d in ds]
    ps = [_pieces(d, n) for d in ds]
    size = 2
    while size < GDN_BASE:
        ps = [_pieces(_dot_pieces(p, p), n) for p in ps]
        ts = [t + _dot_pieces(_pieces(t, n), p) for t, p in zip(ts, ps)]
        size *= 2
    for off_mask in masks[1:]:
        tps = [_pieces(t, n) for t in ts]
        mids = [_pieces(_dot_pieces(_pieces(jnp.where(off_mask, a, 0.0), n), tp), n) for a, tp in zip(a_list, tps)]
        ts = [t - _dot_pieces(tp, mid) for t, tp, mid in zip(ts, tps, mids)]
    return ts


def _gdn_local_kernel(na_ref, dtb_ref, p_ref, pv_ref, nx_ref, cw_ref, ind_ref, ab_ref, abt_ref, tri_ref, trit_ref,
                      ones_ref, uf_ref, ub_ref, wf_ref, wb_ref, qkf_ref, qkb_ref, qgf_ref, qgb_ref, kdf_ref, kdb_ref,
                      eg_ref):
    i = pl.program_id(1)
    nt = pl.num_programs(1)
    x = p_ref[0]
    prev = jnp.where(i >= 2, pv_ref[0][7:8], 0.0)
    nxt = jnp.where((i >= 1) & (i < nt - 1), nx_ref[0][0:1], 0.0)
    row1 = lax.broadcasted_iota(jnp.int32, (TT, 1), 0)
    xm = jnp.where(row1 == 0, prev, pltpu.roll(x, 1, 0))
    xp = jnp.where(row1 == TT - 1, nxt, pltpu.roll(x, TT - 1, 0))
    cw = cw_ref[...]
    y = _silu(cw[0:1] * xm + cw[1:2] * x + cw[2:3] * xp)
    q, k, v = y[:, :WM], y[:, WM:2 * WM], y[:, 2 * WM:]
    ind = ind_ref[...]
    q = q * lax.rsqrt(_sum01_r(q * q, ind, 2) + RMS_EPS) * HEAD ** -0.5
    k = k * lax.rsqrt(_sum01_r(k * k, ind, 2) + RMS_EPS)

    ab = ab_ref[0]
    abt = abt_ref[0]
    ones_bd = ones_ref[...]
    row = lax.broadcasted_iota(jnp.int32, (TT, TT), 0)
    col = lax.broadcasted_iota(jnp.int32, (TT, TT), 1)
    same = (row // CH) == (col // CH)
    eye = row == col
    in_block = lambda n: (row // n) == (col // n)
    inv_masks, n = [in_block(GDN_BASE)], GDN_BASE
    while n < CH:
        inv_masks.append(in_block(2 * n) & jnp.logical_not(in_block(n)))
        n *= 2
    out_refs = ((uf_ref, wf_ref, qkf_ref, qgf_ref, kdf_ref), (ub_ref, wb_ref, qkb_ref, qgb_ref, kdb_ref))
    e_last, a_list, rhs = [], [], []
    for d in range(2):
        incl = same & ((row >= col) if d == 0 else (row <= col))
        strict = same & ((row > col) if d == 0 else (row < col))
        tri_c = tri_ref[...] if d == 0 else trit_ref[...]
        a_col, b_col = ab[:, 4 * d:4 * d + 4], ab[:, 8 + 4 * d:12 + 4 * d]
        a_row = abt[4 * d:4 * d + 4, :]
        g_col = jnp.concatenate([na_ref[d, h] * _softplus(a_col[:, h:h + 1] + dtb_ref[d, h]) for h in range(NH)], axis=1)
        g_row = jnp.concatenate([na_ref[d, h] * _softplus(a_row[h:h + 1, :] + dtb_ref[d, h]) for h in range(NH)], axis=0)
        gc_col = _sum01_l(tri_c, g_col, 3)
        gc_row = _sum01_r(g_row, tri_c, 3, _dot_nt)
        gl_col = _sum01_l(ones_bd, g_col, 3)
        e_last.append(jnp.exp(gl_col))
        qks, qgs, kds = [], [], []
        for h in range(NH):
            sl = slice(h * HEAD, (h + 1) * HEAD)
            qh, kh, vh = q[:, sl], k[:, sl], v[:, sl]
            beta = _sigmoid(b_col[:, h:h + 1])
            gcc = gc_col[:, h:h + 1]
            decay = jnp.exp(jnp.where(incl, gcc - gc_row[h:h + 1, :], -jnp.inf))
            kb = kh * beta
            a_list.append(_dot_nt(kb, kh) * jnp.where(strict, decay, 0.0))
            rhs.append(_pieces(jnp.concatenate([vh * beta, kb * jnp.exp(gcc)], axis=1), GDN_PIECES))
            qk = _dot_nt(qh, kh) * decay
            qks.append(qk[:, 0:CH] + qk[:, CH:2 * CH] + qk[:, 2 * CH:3 * CH] + qk[:, 3 * CH:4 * CH])
            qgs.append(qh * jnp.exp(gcc))
            kds.append(kh * jnp.exp(gl_col[:, h:h + 1] - gcc))
        u_ref, w_ref, qk_ref, qg_ref, kd_ref = out_refs[d]
        qk_ref[0] = jnp.concatenate(qks, axis=1).astype(BF16)
        qg_ref[0] = jnp.concatenate(qgs, axis=1).astype(BF16)
        kdt = jnp.concatenate(kds, axis=1).T
        for c in range(TT // CH):
            kd_ref[0, c] = kdt[:, c * CH:(c + 1) * CH].astype(BF16)
    xs = [_dot_pieces(_pieces(t, GDN_PIECES), r) for t, r in zip(_unit_tri_inverse(a_list, eye, inv_masks), rhs)]
    for d in range(2):
        u_ref, w_ref = out_refs[d][:2]
        u_ref[0] = jnp.concatenate([xx[:, :HEAD] for xx in xs[d * NH:(d + 1) * NH]], axis=1)
        w_ref[0] = jnp.concatenate([xx[:, HEAD:] for xx in xs[d * NH:(d + 1) * NH]], axis=1).astype(BF16)
    eg_ref[0] = jnp.concatenate(e_last + [jnp.zeros((TT, 128 - 2 * NH), F32)], axis=1)


def _gdn_local(pg, pab, pabt, conv_w, ind_sum, neg_a, dt_bias, tri_bd, ones_bd):
    b, t, _ = pg.shape
    w3 = 3 * WM
    n8 = t // 8
    smem = pl.BlockSpec(memory_space=pltpu.SMEM)
    full = lambda a: pl.BlockSpec(a.shape, lambda bi, i: (0,) * a.ndim)
    tile = pl.BlockSpec((1, TT, WM), lambda bi, i: (bi, i, 0))
    f32o = jax.ShapeDtypeStruct((b, t, WM), F32)
    b16o = jax.ShapeDtypeStruct((b, t, WM), BF16)
    return pl.pallas_call(
        _gdn_local_kernel,
        grid=(b, t // TT),
        in_specs=[smem, smem,
                  pl.BlockSpec((1, TT, w3), lambda bi, i: (bi, i, 0)),
                  pl.BlockSpec((1, 8, w3), lambda bi, i: (bi, jnp.maximum(i * (TT // 8) - 1, 0), 0)),
                  pl.BlockSpec((1, 8, w3), lambda bi, i: (bi, jnp.minimum((i + 1) * (TT // 8), n8 - 1), 0)),
                  pl.BlockSpec((3, w3), lambda bi, i: (0, 0)),
                  full(ind_sum),
                  pl.BlockSpec((1, TT, W_AB), lambda bi, i: (bi, i, 0)),
                  pl.BlockSpec((1, 16, TT), lambda bi, i: (bi, 0, i)),
                  full(tri_bd), full(tri_bd), full(ones_bd)],
        out_specs=[tile] * 8 + [pl.BlockSpec((1, TT // CH, WM, CH), lambda bi, i: (bi, i, 0, 0))] * 2
                  + [pl.BlockSpec((1, TT, 128), lambda bi, i: (bi, i, 0))],
        out_shape=[f32o, f32o] + [b16o] * 6 + [jax.ShapeDtypeStruct((b, t // CH, WM, CH), BF16)] * 2
                  + [jax.ShapeDtypeStruct((b, t, 128), F32)],
        compiler_params=_cp("parallel", "parallel"),
        name="gdn_local",
    )(neg_a, dt_bias, pg, pg, pg, conv_w, ind_sum, pab, pabt, tri_bd, tri_bd.T, ones_bd)


def _gdn_scan_kernel(uf_ref, ub_ref, wf_ref, wb_ref, qkf_ref, qkb_ref, qgf_ref, qgb_ref, kdf_ref, kdb_ref,
                     egf_ref, egb_ref, mask_ref, of_ref, ob_ref, s_ref):
    @pl.when(pl.program_id(0) == 0)
    def _():
        s_ref[...] = jnp.zeros_like(s_ref)

    dirs = ((uf_ref, wf_ref, qkf_ref, qgf_ref, kdf_ref, egf_ref, of_ref),
            (ub_ref, wb_ref, qkb_ref, qgb_ref, kdb_ref, egb_ref, ob_ref))
    mask_b = mask_ref[...]
    mask_f = mask_b.astype(F32)
    chains = [(b, d) + refs for b in range(uf_ref.shape[0]) for d, refs in enumerate(dirs)]
    olds = [s_ref[b, d] for b, d, *_ in chains]
    wss = [_dot(jnp.concatenate([w_ref[b], qg_ref[b]], axis=0), s.astype(BF16))
           for (b, d, u_ref, w_ref, qk_ref, qg_ref, kdt_ref, eg_ref, o_ref), s in zip(chains, olds)]
    vbs = [(c[2][c[0]] - ws[:CH]).astype(BF16) for c, ws in zip(chains, wss)]
    for (b, d, u_ref, w_ref, qk_ref, qg_ref, kdt_ref, eg_ref, o_ref), ws, vb in zip(chains, wss, vbs):
        v_bd = jnp.concatenate([vb] * NH, axis=0) * mask_b
        o_ref[b] = ws[CH:] + _dot(qk_ref[b], v_bd)
    for (b, d, u_ref, w_ref, qk_ref, qg_ref, kdt_ref, eg_ref, o_ref), s, vb in zip(chains, olds, vbs):
        eg = eg_ref[b][0:1, :]
        e_row = jnp.concatenate(
            [jnp.broadcast_to(eg[:, 4 * d + h:4 * d + h + 1], (1, HEAD)) for h in range(NH)], axis=1)
        s_ref[b, d] = s * e_row + _dot(kdt_ref[b, 0], vb) * mask_f


def _gdn_scan(loc, head_mask, n_ctx):
    uf, ub, wf, wb, qkf, qkb, qgf, qgb, kdf, kdb, eg = loc
    b, t, _ = uf.shape
    nc = t // CH
    fwd = lambda n: (0, n, 0)
    bwd = lambda n: (0, _rev_index(n, n_ctx, nc), 0)
    fwd4 = lambda n: (0, n, 0, 0)
    bwd4 = lambda n: (0, _rev_index(n, n_ctx, nc), 0, 0)
    blk = lambda im, w=WM: pl.BlockSpec((b, CH, w), im)
    kdt = lambda im: pl.BlockSpec((b, 1, WM, CH), im)
    return pl.pallas_call(
        _gdn_scan_kernel,
        grid=(nc,),
        in_specs=[blk(fwd), blk(bwd)] * 4 + [kdt(fwd4), kdt(bwd4), blk(fwd, 128), blk(bwd, 128),
                                              pl.BlockSpec(head_mask.shape, lambda n: (0, 0))],
        out_specs=[blk(fwd), blk(bwd)],
        out_shape=[jax.ShapeDtypeStruct((b, t, WM), F32)] * 2,
        scratch_shapes=[pltpu.VMEM((b, 2, WM, WM), F32)],
        compiler_params=_cp("arbitrary"),
        name="gdn_scan",
    )(uf, ub, wf, wb, qkf, qkb, qgf, qgb, kdf, kdb, eg, eg, head_mask)


HG_LEVELS = (32, 16, 8, 4, 2, 1)


def _hgrn_constants():
    idx = np.arange(CH)
    i, t = idx[:, None], idx[None, :]
    blocks = [(t <= i), (t > i)]
    masks = []
    for s in HG_LEVELS:
        m = (idx // (2 * s)) * 2 * s + s
        sec = (idx % (2 * s)) >= s
        mi = m[:, None]
        blocks.append(sec[:, None] & (t >= mi) & (t <= i))
        blocks.append((~sec)[:, None] & (t > i) & (t <= mi - 1))
        same = (idx[:, None] // (2 * s)) == (idx[None, :] // (2 * s))
        masks.append(same & sec[:, None] & (~sec)[None, :])
    mall_f = np.concatenate(blocks, axis=0).astype(np.float32)
    masks_f = np.stack(masks).astype(np.float32)
    nb = len(blocks)
    mall_b = mall_f.reshape(nb, CH, CH)[:, ::-1, ::-1].reshape(nb * CH, CH)
    masks_b = masks_f[:, ::-1, ::-1]
    eye = np.eye(TT // CH, dtype=np.float32)
    bd = lambda m: np.stack([np.kron(eye, m[lv]) for lv in range(len(HG_LEVELS))])
    return np.stack([mall_f, mall_b]), np.stack([bd(masks_f), bd(masks_b)])


def _hgrn_local_kernel(ph_ref, lbp_ref, mall_ref, mask_ref, ind_ref, ones_ref,
                       oi_ref, qgf_ref, qgb_ref, kvf_ref, kvb_ref, ecf_ref, ecb_ref):
    ph = ph_ref[0]
    lbp = lbp_ref[...]
    ind = ind_ref[...]
    ones = ones_ref[...]
    log_lb, log_1m_lb, one_m_lb = lbp[0:1], lbp[1:2], lbp[2:3]
    q = _silu(ph[:, :WM])
    v = ph[:, 3 * WM:4 * WM]
    ncl = TT // CH
    o_sum = None
    for d, (qg_ref, kv_ref, ec_ref) in enumerate(((qgf_ref, kvf_ref, ecf_ref), (qgb_ref, kvb_ref, ecb_ref))):
        fz = ph[:, WM * (1 + d):WM * (2 + d)]
        lsig = jnp.minimum(fz, 0.0) - jnp.log1p(jnp.exp(-jnp.abs(fz)))
        bb = log_1m_lb + lsig
        logf = jnp.maximum(log_lb, bb) + jnp.log1p(jnp.exp(-jnp.abs(log_lb - bb)))
        k = one_m_lb / (1.0 + jnp.exp(fz))
        e_c = [jnp.exp(_sum01_l(mall_ref[d], logf[c * CH:(c + 1) * CH], 2)) for c in range(ncl)]
        blk = lambda r: jnp.concatenate([e_c[c][r * CH:(r + 1) * CH] for c in range(ncl)], axis=0)
        qg_ref[0] = (q * blk(0)).astype(BF16)
        kd = k * blk(1)
        o_d = _sum01_r(q * k, ind, 2) * v
        q_lv = [q * blk(2 + 2 * lv) for lv in range(len(HG_LEVELS))]
        k_lv = [k * blk(3 + 2 * lv) for lv in range(len(HG_LEVELS))]
        heads = [slice(h * HEAD, (h + 1) * HEAD) for h in range(NH)]
        atts = [None] * NH
        for lv in range(len(HG_LEVELS)):
            for h, sl in enumerate(heads):
                term = _dot_nt(q_lv[lv][:, sl], k_lv[lv][:, sl]) * mask_ref[d, lv]
                atts[h] = term if atts[h] is None else atts[h] + term
        o_d = o_d + jnp.concatenate([_dot(atts[h], v[:, sl]) for h, sl in enumerate(heads)], axis=1)
        o_sum = o_d if o_sum is None else o_sum + o_d
        for c in range(ncl):
            rows = slice(c * CH, (c + 1) * CH)
            kv = _dot_tn(kd[rows], v[rows])
            tot = _sum01_r(logf[rows], ones, 3, _dot_tn)
            kv_ref[0, c] = jnp.concatenate([kv[h * HEAD:(h + 1) * HEAD, h * HEAD:(h + 1) * HEAD] for h in range(NH)], axis=1)
            ec_ref[0, c] = jnp.concatenate([jnp.exp(tot[h * HEAD:(h + 1) * HEAD]) for h in range(NH)], axis=1)
    oi_ref[0] = o_sum


def _hgrn_local(ph, lbp, mall, masks, ind_sum, ones_c):
    b, t, _ = ph.shape
    ncl = TT // CH
    full = lambda a: pl.BlockSpec(a.shape, lambda bi, i: (0,) * a.ndim)
    tile = pl.BlockSpec((1, TT, WM), lambda bi, i: (bi, i, 0))
    st_spec = pl.BlockSpec((1, ncl, HEAD, WM), lambda bi, i: (bi, i, 0, 0))
    st_shape = jax.ShapeDtypeStruct((b, t // CH, HEAD, WM), F32)
    return pl.pallas_call(
        _hgrn_local_kernel,
        grid=(b, t // TT),
        in_specs=[pl.BlockSpec((1, TT, W_PH), lambda bi, i: (bi, i, 0)),
                  full(lbp), full(mall), full(masks), full(ind_sum), full(ones_c)],
        out_specs=[tile, tile, tile, st_spec, st_spec, st_spec, st_spec],
        out_shape=[jax.ShapeDtypeStruct((b, t, WM), F32), jax.ShapeDtypeStruct((b, t, WM), BF16),
                   jax.ShapeDtypeStruct((b, t, WM), BF16), st_shape, st_shape, st_shape, st_shape],
        compiler_params=_cp("parallel", "parallel"),
        name="hgrn_local",
    )(ph, lbp, mall, masks, ind_sum, ones_c)


def _hgrn_scan_kernel(qgf_ref, qgb_ref, kvf_ref, kvb_ref, ecf_ref, ecb_ref, of_ref, ob_ref, s_ref):
    @pl.when(pl.program_id(0) == 0)
    def _():
        s_ref[...] = jnp.zeros_like(s_ref)

    dirs = ((qgf_ref, kvf_ref, ecf_ref, of_ref), (qgb_ref, kvb_ref, ecb_ref, ob_ref))
    for b in range(qgf_ref.shape[0]):
        for d, (qg_ref, kv_ref, ec_ref, o_ref) in enumerate(dirs):
            s = s_ref[b, d]
            sb = s.astype(BF16)
            qg = qg_ref[b]
            o_ref[b] = jnp.concatenate(
                [_dot(qg[:, h * HEAD:(h + 1) * HEAD], sb[:, h * HEAD:(h + 1) * HEAD]) for h in range(NH)], axis=1)
            s_ref[b, d] = s * ec_ref[b, 0] + kv_ref[b, 0]


def _hgrn_scan(qgf, qgb, kvf, kvb, ecf, ecb, n_ctx):
    b, t, _ = qgf.shape
    nc = t // CH
    fwd3 = lambda n: (0, n, 0)
    bwd3 = lambda n: (0, _rev_index(n, n_ctx, nc), 0)
    fwd4 = lambda n: (0, n, 0, 0)
    bwd4 = lambda n: (0, _rev_index(n, n_ctx, nc), 0, 0)
    tok = lambda im: pl.BlockSpec((b, CH, WM), im)
    st = lambda im: pl.BlockSpec((b, 1, HEAD, WM), im)
    return pl.pallas_call(
        _hgrn_scan_kernel,
        grid=(nc,),
        in_specs=[tok(fwd3), tok(bwd3), st(fwd4), st(bwd4), st(fwd4), st(bwd4)],
        out_specs=[tok(fwd3), tok(bwd3)],
        out_shape=[jax.ShapeDtypeStruct((b, t, WM), F32)] * 2,
        scratch_shapes=[pltpu.VMEM((b, 2, HEAD, WM), F32)],
        compiler_params=_cp("arbitrary"),
        name="hgrn_scan",
    )(qgf, qgb, kvf, kvb, ecf, ecb)


def _ret_direction(d, pr, cos, sin, dec_ref, qd, kd, cdec_ref, s_ref):
    q = (pr[:, :WM] * cos + pr[:, 4 * WM:5 * WM] * sin)
    k = (pr[:, WM:2 * WM] * cos + pr[:, 5 * WM:6 * WM] * sin) * HEAD ** -0.5
    v = pr[:, 2 * WM:3 * WM]
    q_in = q * qd
    k_in = k * kd
    outs = []
    for h in range(NH):
        sl = slice(h * HEAD, (h + 1) * HEAD)
        s = s_ref[d, h]
        att = _dot_nt(q[:, sl], k[:, sl]) * dec_ref[d, h]
        outs.append(_dot(att, v[:, sl]) + _dot(q_in[:, sl], s))
        s_ref[d, h] = s * cdec_ref[d, h] + _dot_tn(k_in[:, sl], v[:, sl])
    return jnp.concatenate(outs, axis=1)


def _ret_scan_kernel(cdec_ref, pf_ref, pb_ref, cf_ref, sf_ref, cb_ref, sb_ref, dec_ref, qd_ref, kd_ref,
                     of_ref, ob_ref, s_ref):
    @pl.when(pl.program_id(1) == 0)
    def _():
        s_ref[...] = jnp.zeros_like(s_ref)

    of_ref[0] = _ret_direction(0, pf_ref[0], cf_ref[...], sf_ref[...], dec_ref, qd_ref[0], kd_ref[0], cdec_ref, s_ref)
    ob_ref[0] = _ret_direction(1, pb_ref[0], cb_ref[...], sb_ref[...], dec_ref, qd_ref[1], kd_ref[1], cdec_ref, s_ref)


def _ret_scan(pr, cos_t, sin_t, dec, qdec, kdec, cdec):
    b, t, _ = pr.shape
    nt = t // TT
    fwd3 = lambda bi, n: (bi, n, 0)
    bwd3 = lambda bi, n: (bi, _rev_index(n, 1, nt), 0)
    fwd2 = lambda bi, n: (n, 0)
    bwd2 = lambda bi, n: (_rev_index(n, 1, nt), 0)
    return pl.pallas_call(
        _ret_scan_kernel,
        grid=(b, nt),
        in_specs=[pl.BlockSpec(memory_space=pltpu.SMEM),
                  pl.BlockSpec((1, TT, W_PR), fwd3), pl.BlockSpec((1, TT, W_PR), bwd3),
                  pl.BlockSpec((TT, WM), fwd2), pl.BlockSpec((TT, WM), fwd2),
                  pl.BlockSpec((TT, WM), bwd2), pl.BlockSpec((TT, WM), bwd2),
                  pl.BlockSpec(dec.shape, lambda bi, n: (0, 0, 0, 0)),
                  pl.BlockSpec(qdec.shape, lambda bi, n: (0, 0, 0)),
                  pl.BlockSpec(kdec.shape, lambda bi, n: (0, 0, 0))],
        out_specs=[pl.BlockSpec((1, TT, WM), fwd3), pl.BlockSpec((1, TT, WM), bwd3)],
        out_shape=[jax.ShapeDtypeStruct((b, t, WM), F32)] * 2,
        scratch_shapes=[pltpu.VMEM((2, NH, HEAD, HEAD), F32)],
        compiler_params=_cp("parallel", "arbitrary"),
        name="ret_scan",
    )(cdec, pr, pr, cos_t, sin_t, cos_t, sin_t, dec, qdec, kdec)


def _s5_kernel(u_ref, wz_ref, wy_ref, a1_ref, a2_ref, y_ref, z_ref, hp_ref, *, n_ctx):
    nch = u_ref.shape[2]
    w_in = u_ref.shape[3]
    for g in range(S5GB):
        z_ref[g] = _dot_pieces([u_ref[0, g]], [wz_ref[0, g], wz_ref[1, g]])
    a1 = a1_ref[...]
    a2 = a2_ref[...]

    def step(s, hs):
        tiles = (s, _rev_index(s, n_ctx // 8, nch // 8))
        new = []
        for g in range(S5GB):
            for d in range(2):
                h, hx = hs[2 * g + d]
                r0 = pl.multiple_of(tiles[d] * 8, 8)
                lanes = slice(128 * d, 128 * (d + 1))
                z = z_ref[g, pl.ds(r0, 8), lanes]
                zx = pltpu.roll(z, S5P, 1)
                c1, c2 = a1[g, d:d + 1], a2[g, d:d + 1]
                entering = [None] * 8
                for j in (range(8) if d == 0 else range(7, -1, -1)):
                    entering[j] = h
                    h, hx = c1 * h + c2 * hx + z[j:j + 1], c1 * hx - c2 * h + zx[j:j + 1]
                hp_ref[g, pl.ds(r0, 8), lanes] = jnp.concatenate(entering, axis=0)
                new.append((h, hx))
        return tuple(new)

    zero = jnp.zeros((1, 128), F32)
    lax.fori_loop(0, nch // 8, step, tuple((zero, zero) for _ in range(2 * S5GB)))
    for g in range(S5GB):
        y = (_dot_pieces([u_ref[0, g]], [wy_ref[0, g, :w_in], wy_ref[1, g, :w_in]])
             + _dot_pieces(_pieces(hp_ref[g], 2), [wy_ref[0, g, w_in:], wy_ref[1, g, w_in:]]))
        y_ref[0, g] = y.astype(y_ref.dtype)


def _s5_scan(u4, wz, wy, a1, a2, n_ctx):
    b, g, nch, w = u4.shape
    return pl.pallas_call(
        functools.partial(_s5_kernel, n_ctx=n_ctx),
        grid=(b, g // S5GB),
        in_specs=[pl.BlockSpec((1, S5GB, nch, w), lambda bi, gi: (bi, gi, 0, 0)),
                  pl.BlockSpec((2, S5GB, w, w), lambda bi, gi: (0, gi, 0, 0)),
                  pl.BlockSpec((2, S5GB, 2 * w, w), lambda bi, gi: (0, gi, 0, 0)),
                  pl.BlockSpec((S5GB, 2, 128), lambda bi, gi: (gi, 0, 0)),
                  pl.BlockSpec((S5GB, 2, 128), lambda bi, gi: (gi, 0, 0))],
        out_specs=pl.BlockSpec((1, S5GB, nch, w), lambda bi, gi: (bi, gi, 0, 0)),
        out_shape=jax.ShapeDtypeStruct(u4.shape, u4.dtype),
        scratch_shapes=[pltpu.VMEM((S5GB, nch, w), F32), pltpu.VMEM((S5GB, nch, w), F32)],
        compiler_params=_cp("parallel", "parallel"),
        name="s5_scan",
    )(u4, wz, wy, a1, a2)


def _s5_weights(a_re, a_im, log_step, b_re, b_im, c_re, c_im):
    step = jnp.exp(log_step)[..., None]
    e_re, e_im = a_re * step, a_im * step
    def lam_pow(n):
        n = n[..., None, None, None] if n.ndim else n
        mag = jnp.exp(e_re * n)
        return mag * jnp.cos(e_im * n), mag * jnp.sin(e_im * n)
    l1r, l1i = lam_pow(jnp.asarray(1.0, F32))
    den = a_re * a_re + a_im * a_im
    fr = ((l1r - 1.0) * a_re + l1i * a_im) / den
    fi = (l1i * a_re - (l1r - 1.0) * a_im) / den
    bbr = fr[..., None] * b_re - fi[..., None] * b_im
    bbi = fr[..., None] * b_im + fi[..., None] * b_re
    j = jnp.arange(S5C, dtype=F32)
    es = functools.partial(jnp.einsum, precision=HI)

    def build(d):
        cr, ci = c_re[d], c_im[d]
        br, bi = bbr[d], bbi[d]
        sel = lambda x: x[:, d] if x.ndim == 4 else x
        pr, pi = lam_pow(jnp.arange(S5C + 1, dtype=F32))
        pr, pi = pr[:, d], pi[:, d]
        cl_r = cr[None] * pr[:, :, None, :] - ci[None] * pi[:, :, None, :]
        cl_i = cr[None] * pi[:, :, None, :] + ci[None] * pr[:, :, None, :]
        kk = es('ngop,gpi->ngoi', cl_r[:S5C], br) - es('ngop,gpi->ngoi', cl_i[:S5C], bi)
        ji, jo = np.arange(S5C)[:, None], np.arange(S5C)[None, :]
        lag = (jo - ji) if d == 0 else (ji - jo)
        place = jnp.asarray(lag[None] == np.arange(S5C)[:, None, None], F32)
        kt = es('lij,lgoc->ijgoc', place, kk)
        toep = kt.transpose(2, 0, 4, 1, 3).reshape(S5G, S5C * 16, S5C * 16)
        pick_m = (lambda x: x[1:S5C + 1]) if d == 0 else (lambda x: x[1:S5C + 1][::-1])
        wo_r = pick_m(cl_r).transpose(1, 3, 0, 2).reshape(S5G, S5P, S5C * 16)
        wo_i = -pick_m(cl_i).transpose(1, 3, 0, 2).reshape(S5G, S5P, S5C * 16)
        wout = jnp.concatenate([wo_r, wo_i], axis=1)
        pick_e = (lambda x: x[:S5C][::-1]) if d == 0 else (lambda x: x[:S5C])
        lr, li = pick_e(pr), pick_e(pi)
        wi_r = (lr[..., None] * br[None] - li[..., None] * bi[None])
        wi_i = (lr[..., None] * bi[None] + li[..., None] * br[None])
        win = jnp.concatenate([wi_r.transpose(1, 0, 3, 2).reshape(S5G, S5C * 16, S5P),
                               wi_i.transpose(1, 0, 3, 2).reshape(S5G, S5C * 16, S5P)], axis=2)
        ar, ai = pr[S5C], pi[S5C]
        a1 = jnp.concatenate([ar, ar], axis=1)
        a2 = jnp.concatenate([-ai, ai], axis=1)
        return toep, wout, win, a1, a2

    tf, of, wf, a1f, a2f = build(0)
    tb, ob, wb, a1b, a2b = build(1)
    wz = jnp.concatenate([wf, wb], axis=2)
    wy = jnp.concatenate([tf + tb, of, ob], axis=1)
    split = lambda w: jnp.stack([w.astype(BF16), (w - w.astype(BF16).astype(F32)).astype(BF16)])
    return split(wz), split(wy), jnp.stack([a1f, a1b], axis=1), jnp.stack([a2f, a2b], axis=1)


def _gelu_tanh(x):
    return 0.5 * x * (1.0 + jnp.tanh(math.sqrt(2.0 / math.pi) * (x + 0.044715 * x * x * x)))


def _outproj_kernel(x_ref, mod_ref, gf_ref, gb_ref, gz_ref, ys_ref, us_ref, hi_ref, hf_ref, hb_ref, hg_ref,
                    rf_ref, rb_ref, rg_ref, vec_ref, glu_ref, ind_ref, w_ref, ln_ref, o_ref, *, alpha):
    ind = ind_ref[...]
    vec = vec_ref[...]

    def head_rms(o):
        return o * lax.rsqrt(_sum01_r(o * o, ind, 2) * (1.0 / HEAD) + RMS_EPS)

    m_gdn = head_rms(gf_ref[0] + gb_ref[0]) * vec[0:1] * _silu(gz_ref[0])
    u = us_ref[0].astype(F32)
    ys = _gelu_tanh(ys_ref[0].astype(F32) + vec[2:3] * u)
    m_s5 = ys * _sigmoid(_dot(ys, glu_ref[...]) + vec[3:4])
    m_hg = head_rms(hi_ref[0] + hf_ref[0] + hb_ref[0]) * vec[1:2] * _silu(hg_ref[0])
    m_rt = head_rms(rf_ref[0] + rb_ref[0]) * _silu(rg_ref[0])
    acc = None
    for k, mk in enumerate((m_gdn, m_s5, m_hg, m_rt)):
        part = _dot(mk.astype(BF16), w_ref[k * WM:(k + 1) * WM, :])
        acc = part if acc is None else acc + part
    m = mod_ref[0, 0]
    ln = ln_ref[...]
    o_ref[0] = _layer_norm(alpha * x_ref[0] + m[2:3] * acc, ln[0:1], ln[1:2])


def _outproj(xs, mod, g_of, g_ob, pg, ys, ps, h_oi, h_of, h_ob, ph, r_of, r_ob, pr, vec, glu_w, ind_sum, w_out, ln,
             alpha):
    b, t, d = xs.shape
    tile = lambda c: pl.BlockSpec((1, TT, WM), lambda bi, i, c=c: (bi, i, c))
    full = lambda a: pl.BlockSpec(a.shape, lambda bi, i: (0,) * a.ndim)
    return pl.pallas_call(
        functools.partial(_outproj_kernel, alpha=alpha),
        grid=(b, t // TT),
        in_specs=[pl.BlockSpec((1, TT, d), lambda bi, i: (bi, i, 0)),
                  pl.BlockSpec((1, 1, 6, d), lambda bi, i: (bi, jnp.minimum(i, 1), 0, 0)),
                  tile(0), tile(0), tile(3), tile(0), tile(0), tile(0), tile(0), tile(0), tile(4),
                  tile(0), tile(0), tile(3),
                  full(vec), full(glu_w), full(ind_sum), full(w_out), full(ln)],
        out_specs=pl.BlockSpec((1, TT, d), lambda bi, i: (bi, i, 0)),
        out_shape=jax.ShapeDtypeStruct(xs.shape, F32),
        compiler_params=_cp("parallel", "parallel"),
        name="outproj",
    )(xs, mod, g_of, g_ob, pg, ys, ps, h_oi, h_of, h_ob, ph, r_of, r_ob, pr, vec, glu_w, ind_sum, w_out, ln)


def _ffn_kernel(x_ref, mod_ref, w1_ref, w3_ref, w2_ref, ln_ref, o_ref, *, alpha):
    m = mod_ref[0, 0]
    x = x_ref[0]
    h = (x * (1.0 + m[4:5]) + m[3:4]).astype(BF16)
    act = (_silu(_dot(h, w1_ref[...])) * _dot(h, w3_ref[...])).astype(BF16)
    y = _dot(act, w2_ref[...])
    ln = ln_ref[...]
    o_ref[0] = _layer_norm(alpha * x + m[5:6] * y, ln[0:1], ln[1:2])


def _ffn(xs, mod, w1, w3, w2, ln, alpha):
    b, t, d = xs.shape
    full = lambda a: pl.BlockSpec(a.shape, lambda bi, i: (0,) * a.ndim)
    return pl.pallas_call(
        functools.partial(_ffn_kernel, alpha=alpha),
        grid=(b, t // TT),
        in_specs=[pl.BlockSpec((1, TT, d), lambda bi, i: (bi, i, 0)),
                  pl.BlockSpec((1, 1, 6, d), lambda bi, i: (bi, jnp.minimum(i, 1), 0, 0)),
                  full(w1), full(w3), full(w2), full(ln)],
        out_specs=pl.BlockSpec((1, TT, d), lambda bi, i: (bi, i, 0)),
        out_shape=jax.ShapeDtypeStruct(xs.shape, F32),
        compiler_params=_cp("parallel", "parallel"),
        name="ffn_dense",
    )(xs, mod, w1, w3, w2, ln)


def _router_kernel(x_ref, mod_ref, r_ref, h_ref, rt_ref):
    m = mod_ref[0, 0]
    h = x_ref[0] * (1.0 + m[4:5]) + m[3:4]
    h_ref[0] = h
    lane = lax.broadcasted_iota(jnp.int32, (TT, 128), 1)
    logits = jnp.where(lane < N_EXP, _dot(h, r_ref[...], HI), -jnp.inf)
    m1 = jnp.max(logits, axis=-1, keepdims=True)
    i1 = jnp.min(jnp.where(logits == m1, lane, 128), axis=-1, keepdims=True)
    rest = jnp.where(lane == i1, -jnp.inf, logits)
    m2 = jnp.max(rest, axis=-1, keepdims=True)
    i2 = jnp.min(jnp.where(rest == m2, lane, 128), axis=-1, keepdims=True)
    e = jnp.exp(m2 - m1)
    g1 = 1.0 / (1.0 + e)
    g2 = e / (1.0 + e)
    rt_ref[0] = jnp.where(lane == 0, i1.astype(F32),
                          jnp.where(lane == 1, i2.astype(F32),
                                    jnp.where(lane == 2, g1, jnp.where(lane == 3, g2, 0.0))))


def _router(xs, mod, router_pad):
    b, t, d = xs.shape
    return pl.pallas_call(
        _router_kernel,
        grid=(b, t // TT),
        in_specs=[pl.BlockSpec((1, TT, d), lambda bi, i: (bi, i, 0)),
                  pl.BlockSpec((1, 1, 6, d), lambda bi, i: (bi, jnp.minimum(i, 1), 0, 0)),
                  pl.BlockSpec(router_pad.shape, lambda bi, i: (0, 0))],
        out_specs=[pl.BlockSpec((1, TT, d), lambda bi, i: (bi, i, 0)),
                   pl.BlockSpec((1, TT, 128), lambda bi, i: (bi, i, 0))],
        out_shape=[jax.ShapeDtypeStruct(xs.shape, F32), jax.ShapeDtypeStruct((b, t, 128), F32)],
        compiler_params=_cp("parallel", "parallel"),
        name="moe_router",
    )(xs, mod, router_pad)


def _row_copy(src_hbm, row, dst_ref, slot, sem):
    return pltpu.make_async_copy(src_hbm.at[pl.ds(row, 1), :], dst_ref.at[pl.ds(slot, 1), :], sem)


def _dispatch_kernel(dest_ref, h_ref, init_hbm, o_hbm, sem, *, tiles_per_batch):
    del init_hbm
    tile = pl.program_id(0) * tiles_per_batch + pl.program_id(1)
    base = tile * (2 * TT)

    for k in range(2):
        def start(t, c, k=k):
            row = dest_ref[base + k * TT + t]
            pltpu.make_async_copy(h_ref.at[0, pl.ds(t, 1), :], o_hbm.at[pl.ds(row, 1), :], sem).start()
            return c

        lax.fori_loop(0, TT, start, 0, unroll=8)
    for _ in range(2):
        pltpu.make_async_copy(h_ref.at[0], o_hbm.at[pl.ds(0, TT), :], sem).wait()


def _dispatch_rows(h, dest_tiles, n_rows):
    b, t, d = h.shape
    return pl.pallas_call(
        functools.partial(_dispatch_kernel, tiles_per_batch=t // TT),
        grid_spec=pltpu.PrefetchScalarGridSpec(
            num_scalar_prefetch=1,
            grid=(b, t // TT),
            in_specs=[pl.BlockSpec((1, TT, d), lambda bi, i, dr: (bi, i, 0)),
                      pl.BlockSpec(memory_space=pl.ANY)],
            out_specs=pl.BlockSpec(memory_space=pl.ANY),
            scratch_shapes=[pltpu.SemaphoreType.DMA(())]),
        out_shape=jax.ShapeDtypeStruct((n_rows, d), h.dtype),
        input_output_aliases={2: 0},
        compiler_params=_cp("arbitrary", "arbitrary"),
        name="moe_dispatch",
    )(dest_tiles, h, jnp.zeros((n_rows, d), h.dtype))


def _experts_kernel(be_ref, x_ref, w1_ref, w3_ref, w2_ref, y_ref):
    f = pl.program_id(1)
    x = x_ref[...].astype(BF16)
    act = (_silu(_dot(x, w1_ref[0])) * _dot(x, w3_ref[0])).astype(BF16)
    y = _dot(act, w2_ref[0])

    @pl.when(f == 0)
    def _():
        y_ref[...] = y

    @pl.when(f != 0)
    def _():
        y_ref[...] += y


def _experts(xs_sorted, block_expert, w1, w3, w2):
    n_rows, d = xs_sorted.shape
    ff = w1.shape[2]
    nf = 2
    tf = ff // nf
    return pl.pallas_call(
        _experts_kernel,
        grid_spec=pltpu.PrefetchScalarGridSpec(
            num_scalar_prefetch=1,
            grid=(n_rows // MOE_RB, nf),
            in_specs=[pl.BlockSpec((MOE_RB, d), lambda j, f, be: (j, 0)),
                      pl.BlockSpec((1, d, tf), lambda j, f, be: (be[j], 0, f)),
                      pl.BlockSpec((1, d, tf), lambda j, f, be: (be[j], 0, f)),
                      pl.BlockSpec((1, tf, d), lambda j, f, be: (be[j], f, 0))],
            out_specs=pl.BlockSpec((MOE_RB, d), lambda j, f, be: (j, 0))),
        out_shape=jax.ShapeDtypeStruct((n_rows, d), F32),
        compiler_params=_cp("parallel", "arbitrary"),
        name="moe_experts",
    )(block_expert, xs_sorted, w1, w3, w2)


def _combine_kernel(dest_ref, x_ref, mod_ref, rt_ref, ln_ref, y_hbm, o_ref, buf, sem, *, alpha, tiles_per_batch,
                    first_tile):
    tile = pl.program_id(0) * tiles_per_batch + pl.program_id(1) + first_tile
    base = tile * (2 * TT)

    def start(r, c):
        _row_copy(y_hbm, dest_ref[base + r], buf, r, sem).start()
        return c

    lax.fori_loop(0, 2 * TT, start, 0, unroll=8)
    pltpu.make_async_copy(y_hbm.at[pl.ds(0, 2 * TT), :], buf, sem).wait()
    rt = rt_ref[0]
    y = rt[:, 2:3] * buf[0:TT, :] + rt[:, 3:4] * buf[TT:2 * TT, :]
    m = mod_ref[0, 0]
    ln = ln_ref[...]
    o_ref[0] = _layer_norm(alpha * x_ref[0] + m[5:6] * y, ln[0:1], ln[1:2])


def _combine(xs, mod, rt, ln, ys_sorted, dest_tiles, alpha, first_tile):
    b, t, d = xs.shape
    ft = first_tile
    return pl.pallas_call(
        functools.partial(_combine_kernel, alpha=alpha, tiles_per_batch=t // TT, first_tile=ft),
        grid_spec=pltpu.PrefetchScalarGridSpec(
            num_scalar_prefetch=1,
            grid=(b, t // TT - ft),
            in_specs=[pl.BlockSpec((1, TT, d), lambda bi, i, dr: (bi, i + ft, 0)),
                      pl.BlockSpec((1, 1, 6, d), lambda bi, i, dr: (bi, jnp.minimum(i + ft, 1), 0, 0)),
                      pl.BlockSpec((1, TT, 128), lambda bi, i, dr: (bi, i + ft, 0)),
                      pl.BlockSpec(ln.shape, lambda bi, i, dr: (0, 0)),
                      pl.BlockSpec(memory_space=pl.ANY)],
            out_specs=pl.BlockSpec((1, TT, d), lambda bi, i, dr: (bi, i, 0)),
            scratch_shapes=[pltpu.VMEM((2 * TT, d), F32), pltpu.SemaphoreType.DMA(())]),
        out_shape=jax.ShapeDtypeStruct((b, t - ft * TT, d), F32),
        compiler_params=_cp("arbitrary", "arbitrary"),
        name="moe_combine",
    )(dest_tiles, xs, mod, rt, ln, ys_sorted)


def _moe(xs, mod, router_pad, w1, w3, w2, ln, alpha, first_tile):
    b, t, d = xs.shape
    n_tok = b * t
    h, rt = _router(xs, mod, router_pad)
    e_idx = rt[..., 0:2].astype(jnp.int32).reshape(n_tok * 2)
    onehot = (e_idx[:, None] == jnp.arange(N_EXP, dtype=jnp.int32)[None, :]).astype(jnp.int32)
    seg = 2 * TT
    local = jnp.einsum('ij,tjk->tik', jnp.asarray(np.tril(np.ones((seg, seg))), F32),
                       onehot.astype(F32).reshape(-1, seg, N_EXP), precision=HI)
    seg_tot = local[:, -1, :]
    seg_off = jnp.cumsum(seg_tot, axis=0) - seg_tot
    csum = (local + seg_off[:, None, :]).astype(jnp.int32).reshape(n_tok * 2, N_EXP)
    counts = csum[-1]
    padded = (counts + MOE_RB - 1) // MOE_RB * MOE_RB
    pad_end = jnp.cumsum(padded)
    pad_start = pad_end - padded
    dest = jnp.sum(onehot * (csum - 1 + pad_start[None, :]), axis=1)
    n_blocks = -(-(n_tok * 2) // MOE_RB) + N_EXP
    n_rows = n_blocks * MOE_RB
    block_expert = jnp.minimum(
        jnp.sum(jnp.arange(n_blocks, dtype=jnp.int32)[:, None] * MOE_RB >= pad_end[None, :], axis=1),
        N_EXP - 1).astype(jnp.int32)
    dest_tiles = dest.reshape(n_tok // TT, TT, 2).transpose(0, 2, 1).reshape(n_tok * 2)
    xs_sorted = _dispatch_rows(h, dest_tiles, n_rows)
    ys_sorted = _experts(xs_sorted, block_expert, w1, w3, w2)
    return _combine(xs, mod, rt, ln, ys_sorted, dest_tiles, alpha, first_tile)


def _ret_constants(decay_param):
    log_gamma = -jnp.exp(decay_param)
    idx = jnp.arange(TT, dtype=F32)
    diff = idx[:, None] - idx[None, :]
    lg = log_gamma[:, :, None, None]
    dec_f = jnp.exp(jnp.where(diff >= 0, diff * lg[0], -jnp.inf))
    dec_b = jnp.exp(jnp.where(diff <= 0, -diff * lg[1], -jnp.inf))
    dec = jnp.stack([dec_f, dec_b])
    rep = lambda a: jnp.repeat(a, HEAD, axis=-1)
    qdec = jnp.stack([rep(jnp.exp((idx[:, None] + 1.0) * log_gamma[0][None, :])),
                      rep(jnp.exp((TT - idx[:, None]) * log_gamma[1][None, :]))])
    kdec = jnp.stack([rep(jnp.exp((TT - 1.0 - idx[:, None]) * log_gamma[0][None, :])),
                      rep(jnp.exp(idx[:, None] * log_gamma[1][None, :]))])
    cdec = jnp.exp(TT * log_gamma)
    return dec, qdec, kdec, cdec


def _rotary_tables(n_ctx_tok, n_lat, grid_w):
    rows = jnp.repeat(jnp.arange(n_lat // grid_w, dtype=F32), grid_w)
    cols = jnp.tile(jnp.arange(grid_w, dtype=F32), n_lat // grid_w)
    quarter = HEAD // 4
    inv_freq = ROPE_BASE ** (-jnp.arange(quarter, dtype=F32) / quarter)
    ang = jnp.concatenate([rows[:, None] * inv_freq, cols[:, None] * inv_freq], axis=-1)
    cos, sin = jnp.cos(ang), jnp.sin(ang)
    cos_h = jnp.concatenate([cos, cos], axis=-1)
    sin_h = jnp.concatenate([-sin, sin], axis=-1)
    cos_t = jnp.concatenate([jnp.ones((n_ctx_tok, HEAD), F32), cos_h], axis=0)
    sin_t = jnp.concatenate([jnp.zeros((n_ctx_tok, HEAD), F32), sin_h], axis=0)
    return jnp.tile(cos_t, (1, NH)), jnp.tile(sin_t, (1, NH))


def _swap_halves_cols(w):
    dm = w.shape[0]
    return w.reshape(dm, NH, 2, HEAD // 2)[:, :, ::-1, :].reshape(dm, WM)


def _prep_w_in(w):
    dm = w.shape[0]
    r0 = 3344 - 768
    rq, rk = w[:, r0:r0 + WM], w[:, r0 + WM:r0 + 2 * WM]
    return jnp.concatenate([w[:, :1040], jnp.zeros((dm, W_AB - 16), w.dtype), w[:, 1040:],
                            _swap_halves_cols(rq), _swap_halves_cols(rk)], axis=1).astype(BF16)


def kernel(x, c, ctx, c_ctx, ada_w, ada_b, w_in, w_out, ln_g, ln_b, gdn_conv_w, gdn_a_log, gdn_dt_bias, gdn_norm_w, s5_a_re, s5_a_im, s5_log_step, s5_b_re, s5_b_im, s5_c_re, s5_c_im, s5_d, s5_glu_w, s5_glu_b, hgrn_lower_bounds, hgrn_norm_w, ret_decay, ffn_w1, ffn_w3, ffn_w2, moe_router, moe_w1, moe_w3, moe_w2):
    bsz, n_lat, d = x.shape
    n_ctx_tok = ctx.shape[1]
    depth = ada_w.shape[0]
    grid_w = 64
    assert n_ctx_tok == TT and n_lat % TT == 0 and bsz <= 7
    t = n_ctx_tok + n_lat
    alpha = (2.0 * depth) ** 0.25

    xs = jnp.concatenate([ctx, x], axis=1)
    cs = jnp.concatenate([c, c_ctx[None, :], jnp.zeros((8 - bsz - 1, d), F32)], axis=0)
    mod_all = _ada_mod(cs, ada_w, ada_b)
    lat_mod = mod_all[:, :bsz].reshape(depth, bsz, 6, d)
    ctx_mod = jnp.broadcast_to(mod_all[:, bsz].reshape(depth, 1, 6, d), (depth, bsz, 6, d))
    mod_tab = jnp.stack([ctx_mod, lat_mod], axis=2)

    lb_all = jnp.cumsum(jax.nn.softmax(hgrn_lower_bounds.astype(F32), axis=0), axis=0)
    lb_all = lb_all - lb_all[0]
    ind_sum = jnp.asarray(np.kron(np.eye(NH), np.ones((HEAD, HEAD))), BF16)
    chunks_eye = np.eye(TT // CH)
    tri_bd = jnp.asarray(np.kron(chunks_eye, np.tril(np.ones((CH, CH)))), BF16)
    ones_bd = jnp.asarray(np.kron(chunks_eye, np.ones((CH, CH))), BF16)
    ones_c = jnp.ones((CH, HEAD), BF16)
    mall_np, masks_np = _hgrn_constants()
    mall, masks = jnp.asarray(mall_np, BF16), jnp.asarray(masks_np)
    cos_t, sin_t = _rotary_tables(n_ctx_tok, n_lat, grid_w)
    nch = t // S5C

    for layer in range(depth):
        mod = mod_tab[layer]
        pg, pab, ps, ph, pr = _inproj(xs, mod, _prep_w_in(w_in[layer]))

        pabt = pab[..., :16].transpose(0, 2, 1)
        g_loc = _gdn_local(pg, pab, pabt, gdn_conv_w[layer], ind_sum, -jnp.exp(gdn_a_log[layer]),
                           gdn_dt_bias[layer], tri_bd, ones_bd)
        g_of, g_ob = _gdn_scan(g_loc, ind_sum, n_ctx_tok // CH)

        wz, wy, a1, a2 = _s5_weights(s5_a_re[layer], s5_a_im[layer], s5_log_step[layer], s5_b_re[layer],
                                     s5_b_im[layer], s5_c_re[layer], s5_c_im[layer])
        u4 = ps.reshape(bsz, nch, S5C, S5G, 16).transpose(0, 3, 1, 2, 4).reshape(bsz, S5G, nch, S5C * 16)
        y4 = _s5_scan(u4, wz, wy, a1, a2, n_ctx_tok // S5C)
        ys = y4.reshape(bsz, S5G, nch, S5C, 16).transpose(0, 2, 3, 1, 4).reshape(bsz, t, WM)

        lb = lb_all[layer][None, :]
        lbp = jnp.concatenate([jnp.log(lb), jnp.log1p(-lb), 1.0 - lb, jnp.zeros((5, WM), F32)], axis=0)
        h_oi, h_qgf, h_qgb, h_kvf, h_kvb, h_ecf, h_ecb = _hgrn_local(ph, lbp, mall, masks, ind_sum, ones_c)
        h_of, h_ob = _hgrn_scan(h_qgf, h_qgb, h_kvf, h_kvb, h_ecf, h_ecb, n_ctx_tok // CH)

        dec, qdec, kdec, cdec = _ret_constants(ret_decay[layer])
        r_of, r_ob = _ret_scan(pr, cos_t, sin_t, dec, qdec, kdec, cdec)

        vec = jnp.concatenate([jnp.tile(gdn_norm_w[layer], NH)[None], jnp.tile(hgrn_norm_w[layer], NH)[None],
                               s5_d[layer][None], s5_glu_b[layer][None], jnp.zeros((4, WM), F32)], axis=0)
        xs = _outproj(xs, mod, g_of, g_ob, pg, ys, ps, h_oi, h_of, h_ob, ph, r_of, r_ob, pr, vec, s5_glu_w[layer],
                      ind_sum, w_out[layer].astype(BF16), jnp.stack([ln_g[layer, 0], ln_b[layer, 0]]), alpha)

        j = layer // 2
        ln2 = jnp.stack([ln_g[layer, 1], ln_b[layer, 1]])
        if layer % 2 == 0:
            xs = _ffn(xs, mod, ffn_w1[j].astype(BF16), ffn_w3[j].astype(BF16), ffn_w2[j].astype(BF16), ln2, alpha)
        else:
            router_pad = jnp.concatenate([moe_router[j], jnp.zeros((d, 128 - N_EXP), F32)], axis=1)
            xs = _moe(xs, mod, router_pad, moe_w1[j].astype(BF16), moe_w3[j].astype(BF16),
                      moe_w2[j].astype(BF16), ln2, alpha, first_tile=int(layer == depth - 1))
    return xs if depth % 2 == 0 else xs[:, n_ctx_tok:, :]
```

```python
import functools
import math

import numpy as np
import jax
import jax.numpy as jnp
from jax import lax
from jax.experimental import pallas as pl
from jax.experimental.pallas import tpu as pltpu

F32 = jnp.float32
BF16 = jnp.bfloat16
HI = lax.Precision.HIGHEST

HEAD = 64
NH = 4
WM = NH * HEAD
TT = 256
CH = 64
S5C = 16
S5G = 16
S5P = 64
S5GB = 8
N_EXP = 8
MOE_RB = 512
DMA_GROUP = 8
LN_EPS = 1e-5
RMS_EPS = 1e-6
ROPE_BASE = 10000.0
VMEM_LIMIT = 56 * 1024 * 1024


def _cp(*sem):
    return pltpu.CompilerParams(dimension_semantics=sem, vmem_limit_bytes=VMEM_LIMIT)


def _sigmoid(x):
    return 1.0 / (1.0 + jnp.exp(-x))


def _silu(x):
    return x * _sigmoid(x)


def _softplus(x):
    return jnp.maximum(x, 0.0) + jnp.log1p(jnp.exp(-jnp.abs(x)))


def _dot(a, b, precision=None):
    return jnp.dot(a, b, preferred_element_type=F32, precision=precision)


def _dot_nt(a, b, precision=None):
    return lax.dot_general(a, b, (((1,), (1,)), ((), ())), preferred_element_type=F32, precision=precision)


def _dot_tn(a, b, precision=None):
    return lax.dot_general(a, b, (((0,), (0,)), ((), ())), preferred_element_type=F32, precision=precision)


def _layer_norm(y, g, b):
    mu = jnp.mean(y, axis=-1, keepdims=True)
    yc = y - mu
    var = jnp.mean(yc * yc, axis=-1, keepdims=True)
    return yc * lax.rsqrt(var + LN_EPS) * g + b


def _rev_index(n, n_ctx, n_all):
    return jnp.where(n < n_ctx, n_ctx - 1 - n, n_all + n_ctx - 1 - n)


def _ada_kernel(c_ref, w_ref, b_ref, o_ref):
    o_ref[0] = _dot(_silu(c_ref[...]), w_ref[0], HI) + b_ref[0]


def _ada_mod(cs, ada_w, ada_b):
    depth, d, d6 = ada_w.shape
    tn = 1024
    return pl.pallas_call(
        _ada_kernel,
        grid=(depth, d6 // tn),
        in_specs=[pl.BlockSpec((8, d), lambda l, j: (0, 0)),
                  pl.BlockSpec((1, d, tn), lambda l, j: (l, 0, j)),
                  pl.BlockSpec((1, 1, tn), lambda l, j: (l, 0, j))],
        out_specs=pl.BlockSpec((1, 8, tn), lambda l, j: (l, 0, j)),
        out_shape=jax.ShapeDtypeStruct((depth, 8, d6), F32),
        compiler_params=_cp("parallel", "parallel"),
        name="ada_mod",
    )(cs, ada_w, ada_b.reshape(depth, 1, d6))


W_PG, W_AB, W_S5, W_PH, W_PR = 1024, 128, 256, 1280, 1536
P_OFF = np.cumsum([0, W_PG, W_AB, W_S5, W_PH, W_PR])


def _inproj_kernel(x_ref, mod_ref, w_ref, pg_ref, pab_ref, ps_ref, ph_ref, pr_ref):
    m = mod_ref[0, 0]
    h = (x_ref[0] * (1.0 + m[1:2]) + m[0:1]).astype(BF16)
    for k, o_ref in enumerate((pg_ref, pab_ref, ps_ref, ph_ref, pr_ref)):
        o_ref[0] = _dot(h, w_ref[:, P_OFF[k]:P_OFF[k + 1]]).astype(o_ref.dtype)


def _inproj(xs, mod, w):
    b, t, d = xs.shape
    widths = (W_PG, W_AB, W_S5, W_PH, W_PR)
    return pl.pallas_call(
        _inproj_kernel,
        grid=(b, t // TT),
        in_specs=[pl.BlockSpec((1, TT, d), lambda bi, i: (bi, i, 0)),
                  pl.BlockSpec((1, 1, 6, d), lambda bi, i: (bi, jnp.minimum(i, 1), 0, 0)),
                  pl.BlockSpec(w.shape, lambda bi, i: (0, 0))],
        out_specs=[pl.BlockSpec((1, TT, wd), lambda bi, i: (bi, i, 0)) for wd in widths],
        out_shape=[jax.ShapeDtypeStruct((b, t, wd), BF16 if wd == W_S5 else F32) for wd in widths],
        compiler_params=_cp("parallel", "parallel"),
        name="inproj",
    )(xs, mod, w)


def _pieces(x, n):
    out, r = [], x
    for i in range(n):
        p = r.astype(BF16)
        out.append(p)
        if i + 1 < n:
            r = r - p.astype(F32)
    return out


def _dot_pieces(a_parts, b_parts, dot=_dot):
    n = max(len(a_parts), len(b_parts))
    acc = None
    for i, ap in enumerate(a_parts):
        for j, bp in enumerate(b_parts):
            if i + j < n:
                t = dot(ap, bp)
                acc = t if acc is None else acc + t
    return acc


def _sum01_l(m01, x, n):
    return _dot_pieces([m01], _pieces(x, n))


def _sum01_r(x, m01, n, dot=_dot):
    return _dot_pieces(_pieces(x, n), [m01], dot)


GDN_PIECES = 1
GDN_BASE = 16


def _unit_tri_inverse(a_list, eye, masks):
    n = GDN_PIECES
    ident = jnp.where(eye, 1.0, 0.0)
    ds = [jnp.where(masks[0], a, 0.0) for a in a_list]
    ts = [ident - d for d in ds]
    ps = [_pieces(d, n) for d in ds]
    size = 2
    while size < GDN_BASE:
        ps = [_pieces(_dot_pieces(p, p), n) for p in ps]
        ts = [t + _dot_pieces(_pieces(t, n), p) for t, p in zip(ts, ps)]
        size *= 2
    for off_mask in masks[1:]:
        tps = [_pieces(t, n) for t in ts]
        mids = [_pieces(_dot_pieces(_pieces(jnp.where(off_mask, a, 0.0), n), tp), n) for a, tp in zip(a_list, tps)]
        ts = [t - _dot_pieces(tp, mid) for t, tp, mid in zip(ts, tps, mids)]
    return ts


def _gdn_local_kernel(na_ref, dtb_ref, p_ref, pv_ref, nx_ref, cw_ref, ind_ref, ab_ref, abt_ref, tri_ref, trit_ref,
                      ones_ref, uf_ref, ub_ref, wf_ref, wb_ref, qkf_ref, qkb_ref, qgf_ref, qgb_ref, kdf_ref, kdb_ref,
                      eg_ref):
    i = pl.program_id(1)
    nt = pl.num_programs(1)
    x = p_ref[0]
    prev = jnp.where(i >= 2, pv_ref[0][7:8], 0.0)
    nxt = jnp.where((i >= 1) & (i < nt - 1), nx_ref[0][0:1], 0.0)
    row1 = lax.broadcasted_iota(jnp.int32, (TT, 1), 0)
    xm = jnp.where(row1 == 0, prev, pltpu.roll(x, 1, 0))
    xp = jnp.where(row1 == TT - 1, nxt, pltpu.roll(x, TT - 1, 0))
    cw = cw_ref[...]
    y = _silu(cw[0:1] * xm + cw[1:2] * x + cw[2:3] * xp)
    q, k, v = y[:, :WM], y[:, WM:2 * WM], y[:, 2 * WM:]
    ind = ind_ref[...]
    q = q * lax.rsqrt(_sum01_r(q * q, ind, 2) + RMS_EPS) * HEAD ** -0.5
    k = k * lax.rsqrt(_sum01_r(k * k, ind, 2) + RMS_EPS)

    ab = ab_ref[0]
    abt = abt_ref[0]
    ones_bd = ones_ref[...]
    row = lax.broadcasted_iota(jnp.int32, (TT, TT), 0)
    col = lax.broadcasted_iota(jnp.int32, (TT, TT), 1)
    same = (row // CH) == (col // CH)
    eye = row == col
    in_block = lambda n: (row // n) == (col // n)
    inv_masks, n = [in_block(GDN_BASE)], GDN_BASE
    while n < CH:
        inv_masks.append(in_block(2 * n) & jnp.logical_not(in_block(n)))
        n *= 2
    out_refs = ((uf_ref, wf_ref, qkf_ref, qgf_ref, kdf_ref), (ub_ref, wb_ref, qkb_ref, qgb_ref, kdb_ref))
    e_last, a_list, rhs = [], [], []
    for d in range(2):
        incl = same & ((row >= col) if d == 0 else (row <= col))
        strict = same & ((row > col) if d == 0 else (row < col))
        tri_c = tri_ref[...] if d == 0 else trit_ref[...]
        a_col, b_col = ab[:, 4 * d:4 * d + 4], ab[:, 8 + 4 * d:12 + 4 * d]
        a_row = abt[4 * d:4 * d + 4, :]
        g_col = jnp.concatenate([na_ref[d, h] * _softplus(a_col[:, h:h + 1] + dtb_ref[d, h]) for h in range(NH)], axis=1)
        g_row = jnp.concatenate([na_ref[d, h] * _softplus(a_row[h:h + 1, :] + dtb_ref[d, h]) for h in range(NH)], axis=0)
        gc_col = _sum01_l(tri_c, g_col, 3)
        gc_row = _sum01_r(g_row, tri_c, 3, _dot_nt)
        gl_col = _sum01_l(ones_bd, g_col, 3)
        e_last.append(jnp.exp(gl_col))
        qks, qgs, kds = [], [], []
        for h in range(NH):
            sl = slice(h * HEAD, (h + 1) * HEAD)
            qh, kh, vh = q[:, sl], k[:, sl], v[:, sl]
            beta = _sigmoid(b_col[:, h:h + 1])
            gcc = gc_col[:, h:h + 1]
            decay = jnp.exp(jnp.where(incl, gcc - gc_row[h:h + 1, :], -jnp.inf))
            kb = kh * beta
            a_list.append(_dot_nt(kb, kh) * jnp.where(strict, decay, 0.0))
            rhs.append(_pieces(jnp.concatenate([vh * beta, kb * jnp.exp(gcc)], axis=1), GDN_PIECES))
            qk = _dot_nt(qh, kh) * decay
            qks.append(qk[:, 0:CH] + qk[:, CH:2 * CH] + qk[:, 2 * CH:3 * CH] + qk[:, 3 * CH:4 * CH])
            qgs.append(qh * jnp.exp(gcc))
            kds.append(kh * jnp.exp(gl_col[:, h:h + 1] - gcc))
        u_ref, w_ref, qk_ref, qg_ref, kd_ref = out_refs[d]
        qk_ref[0] = jnp.concatenate(qks, axis=1).astype(BF16)
        qg_ref[0] = jnp.concatenate(qgs, axis=1).astype(BF16)
        kdt = jnp.concatenate(kds, axis=1).T
        for c in range(TT // CH):
            kd_ref[0, c] = kdt[:, c * CH:(c + 1) * CH].astype(BF16)
    xs = [_dot_pieces(_pieces(t, GDN_PIECES), r) for t, r in zip(_unit_tri_inverse(a_list, eye, inv_masks), rhs)]
    for d in range(2):
        u_ref, w_ref = out_refs[d][:2]
        u_ref[0] = jnp.concatenate([xx[:, :HEAD] for xx in xs[d * NH:(d + 1) * NH]], axis=1)
        w_ref[0] = jnp.concatenate([xx[:, HEAD:] for xx in xs[d * NH:(d + 1) * NH]], axis=1).astype(BF16)
    eg_ref[0] = jnp.concatenate(e_last + [jnp.zeros((TT, 128 - 2 * NH), F32)], axis=1)


def _gdn_local(pg, pab, pabt, conv_w, ind_sum, neg_a, dt_bias, tri_bd, ones_bd):
    b, t, _ = pg.shape
    w3 = 3 * WM
    n8 = t // 8
    smem = pl.BlockSpec(memory_space=pltpu.SMEM)
    full = lambda a: pl.BlockSpec(a.shape, lambda bi, i: (0,) * a.ndim)
    tile = pl.BlockSpec((1, TT, WM), lambda bi, i: (bi, i, 0))
    f32o = jax.ShapeDtypeStruct((b, t, WM), F32)
    b16o = jax.ShapeDtypeStruct((b, t, WM), BF16)
    return pl.pallas_call(
        _gdn_local_kernel,
        grid=(b, t // TT),
        in_specs=[smem, smem,
                  pl.BlockSpec((1, TT, w3), lambda bi, i: (bi, i, 0)),
                  pl.BlockSpec((1, 8, w3), lambda bi, i: (bi, jnp.maximum(i * (TT // 8) - 1, 0), 0)),
                  pl.BlockSpec((1, 8, w3), lambda bi, i: (bi, jnp.minimum((i + 1) * (TT // 8), n8 - 1), 0)),
                  pl.BlockSpec((3, w3), lambda bi, i: (0, 0)),
                  full(ind_sum),
                  pl.BlockSpec((1, TT, W_AB), lambda bi, i: (bi, i, 0)),
                  pl.BlockSpec((1, 16, TT), lambda bi, i: (bi, 0, i)),
                  full(tri_bd), full(tri_bd), full(ones_bd)],
        out_specs=[tile] * 8 + [pl.BlockSpec((1, TT // CH, WM, CH), lambda bi, i: (bi, i, 0, 0))] * 2
                  + [pl.BlockSpec((1, TT, 128), lambda bi, i: (bi, i, 0))],
        out_shape=[f32o, f32o] + [b16o] * 6 + [jax.ShapeDtypeStruct((b, t // CH, WM, CH), BF16)] * 2
                  + [jax.ShapeDtypeStruct((b, t, 128), F32)],
        compiler_params=_cp("parallel", "parallel"),
        name="gdn_local",
    )(neg_a, dt_bias, pg, pg, pg, conv_w, ind_sum, pab, pabt, tri_bd, tri_bd.T, ones_bd)


def _gdn_scan_kernel(uf_ref, ub_ref, wf_ref, wb_ref, qkf_ref, qkb_ref, qgf_ref, qgb_ref, kdf_ref, kdb_ref,
                     egf_ref, egb_ref, mask_ref, of_ref, ob_ref, s_ref):
    @pl.when(pl.program_id(0) == 0)
    def _():
        s_ref[...] = jnp.zeros_like(s_ref)

    dirs = ((uf_ref, wf_ref, qkf_ref, qgf_ref, kdf_ref, egf_ref, of_ref),
            (ub_ref, wb_ref, qkb_ref, qgb_ref, kdb_ref, egb_ref, ob_ref))
    mask_b = mask_ref[...]
    mask_f = mask_b.astype(F32)
    chains = [(b, d) + refs for b in range(uf_ref.shape[0]) for d, refs in enumerate(dirs)]
    olds = [s_ref[b, d] for b, d, *_ in chains]
    wss = [_dot(jnp.concatenate([w_ref[b], qg_ref[b]], axis=0), s.astype(BF16))
           for (b, d, u_ref, w_ref, qk_ref, qg_ref, kdt_ref, eg_ref, o_ref), s in zip(chains, olds)]
    vbs = [(c[2][c[0]] - ws[:CH]).astype(BF16) for c, ws in zip(chains, wss)]
    for (b, d, u_ref, w_ref, qk_ref, qg_ref, kdt_ref, eg_ref, o_ref), ws, vb in zip(chains, wss, vbs):
        v_bd = jnp.concatenate([vb] * NH, axis=0) * mask_b
        o_ref[b] = ws[CH:] + _dot(qk_ref[b], v_bd)
    for (b, d, u_ref, w_ref, qk_ref, qg_ref, kdt_ref, eg_ref, o_ref), s, vb in zip(chains, olds, vbs):
        eg = eg_ref[b][0:1, :]
        e_row = jnp.concatenate(
            [jnp.broadcast_to(eg[:, 4 * d + h:4 * d + h + 1], (1, HEAD)) for h in range(NH)], axis=1)
        s_ref[b, d] = s * e_row + _dot(kdt_ref[b, 0], vb) * mask_f


def _gdn_scan(loc, head_mask, n_ctx):
    uf, ub, wf, wb, qkf, qkb, qgf, qgb, kdf, kdb, eg = loc
    b, t, _ = uf.shape
    nc = t // CH
    fwd = lambda n: (0, n, 0)
    bwd = lambda n: (0, _rev_index(n, n_ctx, nc), 0)
    fwd4 = lambda n: (0, n, 0, 0)
    bwd4 = lambda n: (0, _rev_index(n, n_ctx, nc), 0, 0)
    blk = lambda im, w=WM: pl.BlockSpec((b, CH, w), im)
    kdt = lambda im: pl.BlockSpec((b, 1, WM, CH), im)
    return pl.pallas_call(
        _gdn_scan_kernel,
        grid=(nc,),
        in_specs=[blk(fwd), blk(bwd)] * 4 + [kdt(fwd4), kdt(bwd4), blk(fwd, 128), blk(bwd, 128),
                                              pl.BlockSpec(head_mask.shape, lambda n: (0, 0))],
        out_specs=[blk(fwd), blk(bwd)],
        out_shape=[jax.ShapeDtypeStruct((b, t, WM), F32)] * 2,
        scratch_shapes=[pltpu.VMEM((b, 2, WM, WM), F32)],
        compiler_params=_cp("arbitrary"),
        name="gdn_scan",
    )(uf, ub, wf, wb, qkf, qkb, qgf, qgb, kdf, kdb, eg, eg, head_mask)


HG_LEVELS = (32, 16, 8, 4, 2, 1)


def _hgrn_constants():
    idx = np.arange(CH)
    i, t = idx[:, None], idx[None, :]
    blocks = [(t <= i), (t > i)]
    masks = []
    for s in HG_LEVELS:
        m = (idx // (2 * s)) * 2 * s + s
        sec = (idx % (2 * s)) >= s
        mi = m[:, None]
        blocks.append(sec[:, None] & (t >= mi) & (t <= i))
        blocks.append((~sec)[:, None] & (t > i) & (t <= mi - 1))
        same = (idx[:, None] // (2 * s)) == (idx[None, :] // (2 * s))
        masks.append(same & sec[:, None] & (~sec)[None, :])
    mall_f = np.concatenate(blocks, axis=0).astype(np.float32)
    masks_f = np.stack(masks).astype(np.float32)
    nb = len(blocks)
    mall_b = mall_f.reshape(nb, CH, CH)[:, ::-1, ::-1].reshape(nb * CH, CH)
    masks_b = masks_f[:, ::-1, ::-1]
    eye = np.eye(TT // CH, dtype=np.float32)
    bd = lambda m: np.stack([np.kron(eye, m[lv]) for lv in range(len(HG_LEVELS))])
    return np.stack([mall_f, mall_b]), np.stack([bd(masks_f), bd(masks_b)])


def _hgrn_local_kernel(ph_ref, lbp_ref, mall_ref, mask_ref, ind_ref, ones_ref,
                       oi_ref, qgf_ref, qgb_ref, kvf_ref, kvb_ref, ecf_ref, ecb_ref):
    ph = ph_ref[0]
    lbp = lbp_ref[...]
    ind = ind_ref[...]
    ones = ones_ref[...]
    log_lb, log_1m_lb, one_m_lb = lbp[0:1], lbp[1:2], lbp[2:3]
    q = _silu(ph[:, :WM])
    v = ph[:, 3 * WM:4 * WM]
    ncl = TT // CH
    o_sum = None
    for d, (qg_ref, kv_ref, ec_ref) in enumerate(((qgf_ref, kvf_ref, ecf_ref), (qgb_ref, kvb_ref, ecb_ref))):
        fz = ph[:, WM * (1 + d):WM * (2 + d)]
        lsig = jnp.minimum(fz, 0.0) - jnp.log1p(jnp.exp(-jnp.abs(fz)))
        bb = log_1m_lb + lsig
        logf = jnp.maximum(log_lb, bb) + jnp.log1p(jnp.exp(-jnp.abs(log_lb - bb)))
        k = one_m_lb / (1.0 + jnp.exp(fz))
        e_c = [jnp.exp(_sum01_l(mall_ref[d], logf[c * CH:(c + 1) * CH], 2)) for c in range(ncl)]
        blk = lambda r: jnp.concatenate([e_c[c][r * CH:(r + 1) * CH] for c in range(ncl)], axis=0)
        qg_ref[0] = (q * blk(0)).astype(BF16)
        kd = k * blk(1)
        o_d = _sum01_r(q * k, ind, 2) * v
        q_lv = [q * blk(2 + 2 * lv) for lv in range(len(HG_LEVELS))]
        k_lv = [k * blk(3 + 2 * lv) for lv in range(len(HG_LEVELS))]
        heads = [slice(h * HEAD, (h + 1) * HEAD) for h in range(NH)]
        atts = [None] * NH
        for lv in range(len(HG_LEVELS)):
            for h, sl in enumerate(heads):
                term = _dot_nt(q_lv[lv][:, sl], k_lv[lv][:, sl]) * mask_ref[d, lv]
                atts[h] = term if atts[h] is None else atts[h] + term
        o_d = o_d + jnp.concatenate([_dot(atts[h], v[:, sl]) for h, sl in enumerate(heads)], axis=1)
        o_sum = o_d if o_sum is None else o_sum + o_d
        for c in range(ncl):
            rows = slice(c * CH, (c + 1) * CH)
            kv = _dot_tn(kd[rows], v[rows])
            tot = _sum01_r(logf[rows], ones, 3, _dot_tn)
            kv_ref[0, c] = jnp.concatenate([kv[h * HEAD:(h + 1) * HEAD, h * HEAD:(h + 1) * HEAD] for h in range(NH)], axis=1)
            ec_ref[0, c] = jnp.concatenate([jnp.exp(tot[h * HEAD:(h + 1) * HEAD]) for h in range(NH)], axis=1)
    oi_ref[0] = o_sum


def _hgrn_local(ph, lbp, mall, masks, ind_sum, ones_c):
    b, t, _ = ph.shape
    ncl = TT // CH
    full = lambda a: pl.BlockSpec(a.shape, lambda bi, i: (0,) * a.ndim)
    tile = pl.BlockSpec((1, TT, WM), lambda bi, i: (bi, i, 0))
    st_spec = pl.BlockSpec((1, ncl, HEAD, WM), lambda bi, i: (bi, i, 0, 0))
    st_shape = jax.ShapeDtypeStruct((b, t // CH, HEAD, WM), F32)
    return pl.pallas_call(
        _hgrn_local_kernel,
        grid=(b, t // TT),
        in_specs=[pl.BlockSpec((1, TT, W_PH), lambda bi, i: (bi, i, 0)),
                  full(lbp), full(mall), full(masks), full(ind_sum), full(ones_c)],
        out_specs=[tile, tile, tile, st_spec, st_spec, st_spec, st_spec],
        out_shape=[jax.ShapeDtypeStruct((b, t, WM), F32), jax.ShapeDtypeStruct((b, t, WM), BF16),
                   jax.ShapeDtypeStruct((b, t, WM), BF16), st_shape, st_shape, st_shape, st_shape],
        compiler_params=_cp("parallel", "parallel"),
        name="hgrn_local",
    )(ph, lbp, mall, masks, ind_sum, ones_c)


def _hgrn_scan_kernel(qgf_ref, qgb_ref, kvf_ref, kvb_ref, ecf_ref, ecb_ref, of_ref, ob_ref, s_ref):
    @pl.when(pl.program_id(0) == 0)
    def _():
        s_ref[...] = jnp.zeros_like(s_ref)

    dirs = ((qgf_ref, kvf_ref, ecf_ref, of_ref), (qgb_ref, kvb_ref, ecb_ref, ob_ref))
    for b in range(qgf_ref.shape[0]):
        for d, (qg_ref, kv_ref, ec_ref, o_ref) in enumerate(dirs):
            s = s_ref[b, d]
            sb = s.astype(BF16)
            qg = qg_ref[b]
            o_ref[b] = jnp.concatenate(
                [_dot(qg[:, h * HEAD:(h + 1) * HEAD], sb[:, h * HEAD:(h + 1) * HEAD]) for h in range(NH)], axis=1)
            s_ref[b, d] = s * ec_ref[b, 0] + kv_ref[b, 0]


def _hgrn_scan(qgf, qgb, kvf, kvb, ecf, ecb, n_ctx):
    b, t, _ = qgf.shape
    nc = t // CH
    fwd3 = lambda n: (0, n, 0)
    bwd3 = lambda n: (0, _rev_index(n, n_ctx, nc), 0)
    fwd4 = lambda n: (0, n, 0, 0)
    bwd4 = lambda n: (0, _rev_index(n, n_ctx, nc), 0, 0)
    tok = lambda im: pl.BlockSpec((b, CH, WM), im)
    st = lambda im: pl.BlockSpec((b, 1, HEAD, WM), im)
    return pl.pallas_call(
        _hgrn_scan_kernel,
        grid=(nc,),
        in_specs=[tok(fwd3), tok(bwd3), st(fwd4), st(bwd4), st(fwd4), st(bwd4)],
        out_specs=[tok(fwd3), tok(bwd3)],
        out_shape=[jax.ShapeDtypeStruct((b, t, WM), F32)] * 2,
        scratch_shapes=[pltpu.VMEM((b, 2, HEAD, WM), F32)],
        compiler_params=_cp("arbitrary"),
        name="hgrn_scan",
    )(qgf, qgb, kvf, kvb, ecf, ecb)


def _ret_direction(d, pr, cos, sin, dec_ref, qd, kd, cdec_ref, s_ref):
    q = (pr[:, :WM] * cos + pr[:, 4 * WM:5 * WM] * sin)
    k = (pr[:, WM:2 * WM] * cos + pr[:, 5 * WM:6 * WM] * sin) * HEAD ** -0.5
    v = pr[:, 2 * WM:3 * WM]
    q_in = q * qd
    k_in = k * kd
    outs = []
    for h in range(NH):
        sl = slice(h * HEAD, (h + 1) * HEAD)
        s = s_ref[d, h]
        att = _dot_nt(q[:, sl], k[:, sl]) * dec_ref[d, h]
        outs.append(_dot(att, v[:, sl]) + _dot(q_in[:, sl], s))
        s_ref[d, h] = s * cdec_ref[d, h] + _dot_tn(k_in[:, sl], v[:, sl])
    return jnp.concatenate(outs, axis=1)


def _ret_scan_kernel(cdec_ref, pf_ref, pb_ref, cf_ref, sf_ref, cb_ref, sb_ref, dec_ref, qd_ref, kd_ref,
                     of_ref, ob_ref, s_ref):
    @pl.when(pl.program_id(1) == 0)
    def _():
        s_ref[...] = jnp.zeros_like(s_ref)

    of_ref[0] = _ret_direction(0, pf_ref[0], cf_ref[...], sf_ref[...], dec_ref, qd_ref[0], kd_ref[0], cdec_ref, s_ref)
    ob_ref[0] = _ret_direction(1, pb_ref[0], cb_ref[...], sb_ref[...], dec_ref, qd_ref[1], kd_ref[1], cdec_ref, s_ref)


def _ret_scan(pr, cos_t, sin_t, dec, qdec, kdec, cdec):
    b, t, _ = pr.shape
    nt = t // TT
    fwd3 = lambda bi, n: (bi, n, 0)
    bwd3 = lambda bi, n: (bi, _rev_index(n, 1, nt), 0)
    fwd2 = lambda bi, n: (n, 0)
    bwd2 = lambda bi, n: (_rev_index(n, 1, nt), 0)
    return pl.pallas_call(
        _ret_scan_kernel,
        grid=(b, nt),
        in_specs=[pl.BlockSpec(memory_space=pltpu.SMEM),
                  pl.BlockSpec((1, TT, W_PR), fwd3), pl.BlockSpec((1, TT, W_PR), bwd3),
                  pl.BlockSpec((TT, WM), fwd2), pl.BlockSpec((TT, WM), fwd2),
                  pl.BlockSpec((TT, WM), bwd2), pl.BlockSpec((TT, WM), bwd2),
                  pl.BlockSpec(dec.shape, lambda bi, n: (0, 0, 0, 0)),
                  pl.BlockSpec(qdec.shape, lambda bi, n: (0, 0, 0)),
                  pl.BlockSpec(kdec.shape, lambda bi, n: (0, 0, 0))],
        out_specs=[pl.BlockSpec((1, TT, WM), fwd3), pl.BlockSpec((1, TT, WM), bwd3)],
        out_shape=[jax.ShapeDtypeStruct((b, t, WM), F32)] * 2,
        scratch_shapes=[pltpu.VMEM((2, NH, HEAD, HEAD), F32)],
        compiler_params=_cp("parallel", "arbitrary"),
        name="ret_scan",
    )(cdec, pr, pr, cos_t, sin_t, cos_t, sin_t, dec, qdec, kdec)


def _s5_kernel(u_ref, wz_ref, wy_ref, a1_ref, a2_ref, y_ref, z_ref, hp_ref, *, n_ctx):
    nch = u_ref.shape[2]
    w_in = u_ref.shape[3]
    for g in range(S5GB):
        z_ref[g] = _dot_pieces([u_ref[0, g]], [wz_ref[0, g], wz_ref[1, g]])
    a1 = a1_ref[...]
    a2 = a2_ref[...]

    def step(s, hs):
        tiles = (s, _rev_index(s, n_ctx // 8, nch // 8))
        new = []
        for g in range(S5GB):
            for d in range(2):
                h, hx = hs[2 * g + d]
                r0 = pl.multiple_of(tiles[d] * 8, 8)
                lanes = slice(128 * d, 128 * (d + 1))
                z = z_ref[g, pl.ds(r0, 8), lanes]
                zx = pltpu.roll(z, S5P, 1)
                c1, c2 = a1[g, d:d + 1], a2[g, d:d + 1]
                entering = [None] * 8
                for j in (range(8) if d == 0 else range(7, -1, -1)):
                    entering[j] = h
                    h, hx = c1 * h + c2 * hx + z[j:j + 1], c1 * hx - c2 * h + zx[j:j + 1]
                hp_ref[g, pl.ds(r0, 8), lanes] = jnp.concatenate(entering, axis=0)
                new.append((h, hx))
        return tuple(new)

    zero = jnp.zeros((1, 128), F32)
    lax.fori_loop(0, nch // 8, step, tuple((zero, zero) for _ in range(2 * S5GB)))
    for g in range(S5GB):
        y = (_dot_pieces([u_ref[0, g]], [wy_ref[0, g, :w_in], wy_ref[1, g, :w_in]])
             + _dot_pieces(_pieces(hp_ref[g], 2), [wy_ref[0, g, w_in:], wy_ref[1, g, w_in:]]))
        y_ref[0, g] = y.astype(y_ref.dtype)


def _s5_scan(u4, wz, wy, a1, a2, n_ctx):
    b, g, nch, w = u4.shape
    return pl.pallas_call(
        functools.partial(_s5_kernel, n_ctx=n_ctx),
        grid=(b, g // S5GB),
        in_specs=[pl.BlockSpec((1, S5GB, nch, w), lambda bi, gi: (bi, gi, 0, 0)),
                  pl.BlockSpec((2, S5GB, w, w), lambda bi, gi: (0, gi, 0, 0)),
                  pl.BlockSpec((2, S5GB, 2 * w, w), lambda bi, gi: (0, gi, 0, 0)),
                  pl.BlockSpec((S5GB, 2, 128), lambda bi, gi: (gi, 0, 0)),
                  pl.BlockSpec((S5GB, 2, 128), lambda bi, gi: (gi, 0, 0))],
        out_specs=pl.BlockSpec((1, S5GB, nch, w), lambda bi, gi: (bi, gi, 0, 0)),
        out_shape=jax.ShapeDtypeStruct(u4.shape, u4.dtype),
        scratch_shapes=[pltpu.VMEM((S5GB, nch, w), F32), pltpu.VMEM((S5GB, nch, w), F32)],
        compiler_params=_cp("parallel", "parallel"),
        name="s5_scan",
    )(u4, wz, wy, a1, a2)


def _s5_weights(a_re, a_im, log_step, b_re, b_im, c_re, c_im):
    step = jnp.exp(log_step)[..., None]
    e_re, e_im = a_re * step, a_im * step
    def lam_pow(n):
        n = n[..., None, None, None] if n.ndim else n
        mag = jnp.exp(e_re * n)
        return mag * jnp.cos(e_im * n), mag * jnp.sin(e_im * n)
    l1r, l1i = lam_pow(jnp.asarray(1.0, F32))
    den = a_re * a_re + a_im * a_im
    fr = ((l1r - 1.0) * a_re + l1i * a_im) / den
    fi = (l1i * a_re - (l1r - 1.0) * a_im) / den
    bbr = fr[..., None] * b_re - fi[..., None] * b_im
    bbi = fr[..., None] * b_im + fi[..., None] * b_re
    j = jnp.arange(S5C, dtype=F32)
    es = functools.partial(jnp.einsum, precision=HI)

    def build(d):
        cr, ci = c_re[d], c_im[d]
        br, bi = bbr[d], bbi[d]
        sel = lambda x: x[:, d] if x.ndim == 4 else x
        pr, pi = lam_pow(jnp.arange(S5C + 1, dtype=F32))
        pr, pi = pr[:, d], pi[:, d]
        cl_r = cr[None] * pr[:, :, None, :] - ci[None] * pi[:, :, None, :]
        cl_i = cr[None] * pi[:, :, None, :] + ci[None] * pr[:, :, None, :]
        kk = es('ngop,gpi->ngoi', cl_r[:S5C], br) - es('ngop,gpi->ngoi', cl_i[:S5C], bi)
        ji, jo = np.arange(S5C)[:, None], np.arange(S5C)[None, :]
        lag = (jo - ji) if d == 0 else (ji - jo)
        place = jnp.asarray(lag[None] == np.arange(S5C)[:, None, None], F32)
        kt = es('lij,lgoc->ijgoc', place, kk)
        toep = kt.transpose(2, 0, 4, 1, 3).reshape(S5G, S5C * 16, S5C * 16)
        pick_m = (lambda x: x[1:S5C + 1]) if d == 0 else (lambda x: x[1:S5C + 1][::-1])
        wo_r = pick_m(cl_r).transpose(1, 3, 0, 2).reshape(S5G, S5P, S5C * 16)
        wo_i = -pick_m(cl_i).transpose(1, 3, 0, 2).reshape(S5G, S5P, S5C * 16)
        wout = jnp.concatenate([wo_r, wo_i], axis=1)
        pick_e = (lambda x: x[:S5C][::-1]) if d == 0 else (lambda x: x[:S5C])
        lr, li = pick_e(pr), pick_e(pi)
        wi_r = (lr[..., None] * br[None] - li[..., None] * bi[None])
        wi_i = (lr[..., None] * bi[None] + li[..., None] * br[None])
        win = jnp.concatenate([wi_r.transpose(1, 0, 3, 2).reshape(S5G, S5C * 16, S5P),
                               wi_i.transpose(1, 0, 3, 2).reshape(S5G, S5C * 16, S5P)], axis=2)
        ar, ai = pr[S5C], pi[S5C]
        a1 = jnp.concatenate([ar, ar], axis=1)
        a2 = jnp.concatenate([-ai, ai], axis=1)
        return toep, wout, win, a1, a2

    tf, of, wf, a1f, a2f = build(0)
    tb, ob, wb, a1b, a2b = build(1)
    wz = jnp.concatenate([wf, wb], axis=2)
    wy = jnp.concatenate([tf + tb, of, ob], axis=1)
    split = lambda w: jnp.stack([w.astype(BF16), (w - w.astype(BF16).astype(F32)).astype(BF16)])
    return split(wz), split(wy), jnp.stack([a1f, a1b], axis=1), jnp.stack([a2f, a2b], axis=1)


def _gelu_tanh(x):
    return 0.5 * x * (1.0 + jnp.tanh(math.sqrt(2.0 / math.pi) * (x + 0.044715 * x * x * x)))


def _outproj_kernel(x_ref, mod_ref, gf_ref, gb_ref, gz_ref, ys_ref, us_ref, hi_ref, hf_ref, hb_ref, hg_ref,
                    rf_ref, rb_ref, rg_ref, vec_ref, glu_ref, ind_ref, w_ref, ln_ref, o_ref, *, alpha):
    ind = ind_ref[...]
    vec = vec_ref[...]

    def head_rms(o):
        return o * lax.rsqrt(_sum01_r(o * o, ind, 2) * (1.0 / HEAD) + RMS_EPS)

    m_gdn = head_rms(gf_ref[0] + gb_ref[0]) * vec[0:1] * _silu(gz_ref[0])
    u = us_ref[0].astype(F32)
    ys = _gelu_tanh(ys_ref[0].astype(F32) + vec[2:3] * u)
    m_s5 = ys * _sigmoid(_dot(ys, glu_ref[...]) + vec[3:4])
    m_hg = head_rms(hi_ref[0] + hf_ref[0] + hb_ref[0]) * vec[1:2] * _silu(hg_ref[0])
    m_rt = head_rms(rf_ref[0] + rb_ref[0]) * _silu(rg_ref[0])
    acc = None
    for k, mk in enumerate((m_gdn, m_s5, m_hg, m_rt)):
        part = _dot(mk.astype(BF16), w_ref[k * WM:(k + 1) * WM, :])
        acc = part if acc is None else acc + part
    m = mod_ref[0, 0]
    ln = ln_ref[...]
    o_ref[0] = _layer_norm(alpha * x_ref[0] + m[2:3] * acc, ln[0:1], ln[1:2])


def _outproj(xs, mod, g_of, g_ob, pg, ys, ps, h_oi, h_of, h_ob, ph, r_of, r_ob, pr, vec, glu_w, ind_sum, w_out, ln,
             alpha):
    b, t, d = xs.shape
    tile = lambda c: pl.BlockSpec((1, TT, WM), lambda bi, i, c=c: (bi, i, c))
    full = lambda a: pl.BlockSpec(a.shape, lambda bi, i: (0,) * a.ndim)
    return pl.pallas_call(
        functools.partial(_outproj_kernel, alpha=alpha),
        grid=(b, t // TT),
        in_specs=[pl.BlockSpec((1, TT, d), lambda bi, i: (bi, i, 0)),
                  pl.BlockSpec((1, 1, 6, d), lambda bi, i: (bi, jnp.minimum(i, 1), 0, 0)),
                  tile(0), tile(0), tile(3), tile(0), tile(0), tile(0), tile(0), tile(0), tile(4),
                  tile(0), tile(0), tile(3),
                  full(vec), full(glu_w), full(ind_sum), full(w_out), full(ln)],
        out_specs=pl.BlockSpec((1, TT, d), lambda bi, i: (bi, i, 0)),
        out_shape=jax.ShapeDtypeStruct(xs.shape, F32),
        compiler_params=_cp("parallel", "parallel"),
        name="outproj",
    )(xs, mod, g_of, g_ob, pg, ys, ps, h_oi, h_of, h_ob, ph, r_of, r_ob, pr, vec, glu_w, ind_sum, w_out, ln)


def _ffn_kernel(x_ref, mod_ref, w1_ref, w3_ref, w2_ref, ln_ref, o_ref, *, alpha):
    m = mod_ref[0, 0]
    x = x_ref[0]
    h = (x * (1.0 + m[4:5]) + m[3:4]).astype(BF16)
    act = (_silu(_dot(h, w1_ref[...])) * _dot(h, w3_ref[...])).astype(BF16)
    y = _dot(act, w2_ref[...])
    ln = ln_ref[...]
    o_ref[0] = _layer_norm(alpha * x + m[5:6] * y, ln[0:1], ln[1:2])


def _ffn(xs, mod, w1, w3, w2, ln, alpha):
    b, t, d = xs.shape
    full = lambda a: pl.BlockSpec(a.shape, lambda bi, i: (0,) * a.ndim)
    return pl.pallas_call(
        functools.partial(_ffn_kernel, alpha=alpha),
        grid=(b, t // TT),
        in_specs=[pl.BlockSpec((1, TT, d), lambda bi, i: (bi, i, 0)),
                  pl.BlockSpec((1, 1, 6, d), lambda bi, i: (bi, jnp.minimum(i, 1), 0, 0)),
                  full(w1), full(w3), full(w2), full(ln)],
        out_specs=pl.BlockSpec((1, TT, d), lambda bi, i: (bi, i, 0)),
        out_shape=jax.ShapeDtypeStruct(xs.shape, F32),
        compiler_params=_cp("parallel", "parallel"),
        name="ffn_dense",
    )(xs, mod, w1, w3, w2, ln)


def _router_kernel(x_ref, mod_ref, r_ref, h_ref, rt_ref):
    m = mod_ref[0, 0]
    h = x_ref[0] * (1.0 + m[4:5]) + m[3:4]
    h_ref[0] = h
    lane = lax.broadcasted_iota(jnp.int32, (TT, 128), 1)
    logits = jnp.where(lane < N_EXP, _dot(h, r_ref[...], HI), -jnp.inf)
    m1 = jnp.max(logits, axis=-1, keepdims=True)
    i1 = jnp.min(jnp.where(logits == m1, lane, 128), axis=-1, keepdims=True)
    rest = jnp.where(lane == i1, -jnp.inf, logits)
    m2 = jnp.max(rest, axis=-1, keepdims=True)
    i2 = jnp.min(jnp.where(rest == m2, lane, 128), axis=-1, keepdims=True)
    e = jnp.exp(m2 - m1)
    g1 = 1.0 / (1.0 + e)
    g2 = e / (1.0 + e)
    rt_ref[0] = jnp.where(lane == 0, i1.astype(F32),
                          jnp.where(lane == 1, i2.astype(F32),
                                    jnp.where(lane == 2, g1, jnp.where(lane == 3, g2, 0.0))))


def _router(xs, mod, router_pad):
    b, t, d = xs.shape
    return pl.pallas_call(
        _router_kernel,
        grid=(b, t // TT),
        in_specs=[pl.BlockSpec((1, TT, d), lambda bi, i: (bi, i, 0)),
                  pl.BlockSpec((1, 1, 6, d), lambda bi, i: (bi, jnp.minimum(i, 1), 0, 0)),
                  pl.BlockSpec(router_pad.shape, lambda bi, i: (0, 0))],
        out_specs=[pl.BlockSpec((1, TT, d), lambda bi, i: (bi, i, 0)),
                   pl.BlockSpec((1, TT, 128), lambda bi, i: (bi, i, 0))],
        out_shape=[jax.ShapeDtypeStruct(xs.shape, F32), jax.ShapeDtypeStruct((b, t, 128), F32)],
        compiler_params=_cp("parallel", "parallel"),
        name="moe_router",
    )(xs, mod, router_pad)


def _row_copy(src_hbm, row, dst_ref, slot, sem):
    return pltpu.make_async_copy(src_hbm.at[pl.ds(row, 1), :], dst_ref.at[pl.ds(slot, 1), :], sem)


def _dispatch_kernel(dest_ref, h_ref, init_hbm, o_hbm, sem, *, tiles_per_batch):
    del init_hbm
    tile = pl.program_id(0) * tiles_per_batch + pl.program_id(1)
    base = tile * (2 * TT)

    for k in range(2):
        def start(g, c, k=k):
            t0 = g * DMA_GROUP
            for j in range(DMA_GROUP):
                row = dest_ref[base + k * TT + t0 + j]
                pltpu.make_async_copy(h_ref.at[0, pl.ds(t0 + j, 1), :], o_hbm.at[pl.ds(row, 1), :],
                                      sem).start(priority=j % 2)
            return c

        lax.fori_loop(0, TT // DMA_GROUP, start, 0)
    for _ in range(2):
        pltpu.make_async_copy(h_ref.at[0], o_hbm.at[pl.ds(0, TT), :], sem).wait()


def _dispatch_rows(h, dest_tiles, n_rows):
    b, t, d = h.shape
    return pl.pallas_call(
        functools.partial(_dispatch_kernel, tiles_per_batch=t // TT),
        grid_spec=pltpu.PrefetchScalarGridSpec(
            num_scalar_prefetch=1,
            grid=(b, t // TT),
            in_specs=[pl.BlockSpec((1, TT, d), lambda bi, i, dr: (bi, i, 0)),
                      pl.BlockSpec(memory_space=pl.ANY)],
            out_specs=pl.BlockSpec(memory_space=pl.ANY),
            scratch_shapes=[pltpu.SemaphoreType.DMA(())]),
        out_shape=jax.ShapeDtypeStruct((n_rows, d), h.dtype),
        input_output_aliases={2: 0},
        compiler_params=_cp("arbitrary", "arbitrary"),
        name="moe_dispatch",
    )(dest_tiles, h, jnp.zeros((n_rows, d), h.dtype))


def _experts_kernel(be_ref, x_ref, w1_ref, w3_ref, w2_ref, y_ref):
    f = pl.program_id(1)
    x = x_ref[...].astype(BF16)
    act = (_silu(_dot(x, w1_ref[0])) * _dot(x, w3_ref[0])).astype(BF16)
    y = _dot(act, w2_ref[0])

    @pl.when(f == 0)
    def _():
        y_ref[...] = y

    @pl.when(f != 0)
    def _():
        y_ref[...] += y


def _experts(xs_sorted, block_expert, w1, w3, w2):
    n_rows, d = xs_sorted.shape
    ff = w1.shape[2]
    nf = 2
    tf = ff // nf
    return pl.pallas_call(
        _experts_kernel,
        grid_spec=pltpu.PrefetchScalarGridSpec(
            num_scalar_prefetch=1,
            grid=(n_rows // MOE_RB, nf),
            in_specs=[pl.BlockSpec((MOE_RB, d), lambda j, f, be: (j, 0)),
                      pl.BlockSpec((1, d, tf), lambda j, f, be: (be[j], 0, f)),
                      pl.BlockSpec((1, d, tf), lambda j, f, be: (be[j], 0, f)),
                      pl.BlockSpec((1, tf, d), lambda j, f, be: (be[j], f, 0))],
            out_specs=pl.BlockSpec((MOE_RB, d), lambda j, f, be: (j, 0))),
        out_shape=jax.ShapeDtypeStruct((n_rows, d), F32),
        compiler_params=_cp("parallel", "arbitrary"),
        name="moe_experts",
    )(block_expert, xs_sorted, w1, w3, w2)


def _combine_kernel(dest_ref, x_ref, mod_ref, rt_ref, ln_ref, y_hbm, o_ref, buf, sem, *, alpha, tiles_per_batch,
                    first_tile):
    tile = pl.program_id(0) * tiles_per_batch + pl.program_id(1) + first_tile
    base = tile * (2 * TT)

    def start(g, c):
        r0 = g * DMA_GROUP
        for j in range(DMA_GROUP):
            _row_copy(y_hbm, dest_ref[base + r0 + j], buf, r0 + j, sem).start(priority=j % 2)
        return c

    lax.fori_loop(0, 2 * TT // DMA_GROUP, start, 0)
    pltpu.make_async_copy(y_hbm.at[pl.ds(0, 2 * TT), :], buf, sem).wait()
    rt = rt_ref[0]
    y = rt[:, 2:3] * buf[0:TT, :] + rt[:, 3:4] * buf[TT:2 * TT, :]
    m = mod_ref[0, 0]
    ln = ln_ref[...]
    o_ref[0] = _layer_norm(alpha * x_ref[0] + m[5:6] * y, ln[0:1], ln[1:2])


def _combine(xs, mod, rt, ln, ys_sorted, dest_tiles, alpha, first_tile):
    b, t, d = xs.shape
    ft = first_tile
    return pl.pallas_call(
        functools.partial(_combine_kernel, alpha=alpha, tiles_per_batch=t // TT, first_tile=ft),
        grid_spec=pltpu.PrefetchScalarGridSpec(
            num_scalar_prefetch=1,
            grid=(b, t // TT - ft),
            in_specs=[pl.BlockSpec((1, TT, d), lambda bi, i, dr: (bi, i + ft, 0)),
                      pl.BlockSpec((1, 1, 6, d), lambda bi, i, dr: (bi, jnp.minimum(i + ft, 1), 0, 0)),
                      pl.BlockSpec((1, TT, 128), lambda bi, i, dr: (bi, i + ft, 0)),
                      pl.BlockSpec(ln.shape, lambda bi, i, dr: (0, 0)),
                      pl.BlockSpec(memory_space=pl.ANY)],
            out_specs=pl.BlockSpec((1, TT, d), lambda bi, i, dr: (bi, i, 0)),
            scratch_shapes=[pltpu.VMEM((2 * TT, d), F32), pltpu.SemaphoreType.DMA(())]),
        out_shape=jax.ShapeDtypeStruct((b, t - ft * TT, d), F32),
        compiler_params=_cp("arbitrary", "arbitrary"),
        name="moe_combine",
    )(dest_tiles, xs, mod, rt, ln, ys_sorted)


def _moe(xs, mod, router_pad, w1, w3, w2, ln, alpha, first_tile):
    b, t, d = xs.shape
    n_tok = b * t
    h, rt = _router(xs, mod, router_pad)
    e_idx = rt[..., 0:2].astype(jnp.int32).reshape(n_tok * 2)
    onehot = (e_idx[:, None] == jnp.arange(N_EXP, dtype=jnp.int32)[None, :]).astype(jnp.int32)
    seg = 2 * TT
    local = jnp.einsum('ij,tjk->tik', jnp.asarray(np.tril(np.ones((seg, seg))), F32),
                       onehot.astype(F32).reshape(-1, seg, N_EXP), precision=HI)
    seg_tot = local[:, -1, :]
    seg_off = jnp.cumsum(seg_tot, axis=0) - seg_tot
    csum = (local + seg_off[:, None, :]).astype(jnp.int32).reshape(n_tok * 2, N_EXP)
    counts = csum[-1]
    padded = (counts + MOE_RB - 1) // MOE_RB * MOE_RB
    pad_end = jnp.cumsum(padded)
    pad_start = pad_end - padded
    dest = jnp.sum(onehot * (csum - 1 + pad_start[None, :]), axis=1)
    n_blocks = -(-(n_tok * 2) // MOE_RB) + N_EXP
    n_rows = n_blocks * MOE_RB
    block_expert = jnp.minimum(
        jnp.sum(jnp.arange(n_blocks, dtype=jnp.int32)[:, None] * MOE_RB >= pad_end[None, :], axis=1),
        N_EXP - 1).astype(jnp.int32)
    dest_tiles = dest.reshape(n_tok // TT, TT, 2).transpose(0, 2, 1).reshape(n_tok * 2)
    xs_sorted = _dispatch_rows(h, dest_tiles, n_rows)
    ys_sorted = _experts(xs_sorted, block_expert, w1, w3, w2)
    return _combine(xs, mod, rt, ln, ys_sorted, dest_tiles, alpha, first_tile)


def _ret_constants(decay_param):
    log_gamma = -jnp.exp(decay_param)
    idx = jnp.arange(TT, dtype=F32)
    diff = idx[:, None] - idx[None, :]
    lg = log_gamma[:, :, None, None]
    dec_f = jnp.exp(jnp.where(diff >= 0, diff * lg[0], -jnp.inf))
    dec_b = jnp.exp(jnp.where(diff <= 0, -diff * lg[1], -jnp.inf))
    dec = jnp.stack([dec_f, dec_b])
    rep = lambda a: jnp.repeat(a, HEAD, axis=-1)
    qdec = jnp.stack([rep(jnp.exp((idx[:, None] + 1.0) * log_gamma[0][None, :])),
                      rep(jnp.exp((TT - idx[:, None]) * log_gamma[1][None, :]))])
    kdec = jnp.stack([rep(jnp.exp((TT - 1.0 - idx[:, None]) * log_gamma[0][None, :])),
                      rep(jnp.exp(idx[:, None] * log_gamma[1][None, :]))])
    cdec = jnp.exp(TT * log_gamma)
    return dec, qdec, kdec, cdec


def _rotary_tables(n_ctx_tok, n_lat, grid_w):
    rows = jnp.repeat(jnp.arange(n_lat // grid_w, dtype=F32), grid_w)
    cols = jnp.tile(jnp.arange(grid_w, dtype=F32), n_lat // grid_w)
    quarter = HEAD // 4
    inv_freq = ROPE_BASE ** (-jnp.arange(quarter, dtype=F32) / quarter)
    ang = jnp.concatenate([rows[:, None] * inv_freq, cols[:, None] * inv_freq], axis=-1)
    cos, sin = jnp.cos(ang), jnp.sin(ang)
    cos_h = jnp.concatenate([cos, cos], axis=-1)
    sin_h = jnp.concatenate([-sin, sin], axis=-1)
    cos_t = jnp.concatenate([jnp.ones((n_ctx_tok, HEAD), F32), cos_h], axis=0)
    sin_t = jnp.concatenate([jnp.zeros((n_ctx_tok, HEAD), F32), sin_h], axis=0)
    return jnp.tile(cos_t, (1, NH)), jnp.tile(sin_t, (1, NH))


def _swap_halves_cols(w):
    dm = w.shape[0]
    return w.reshape(dm, NH, 2, HEAD // 2)[:, :, ::-1, :].reshape(dm, WM)


def _prep_w_in(w):
    dm = w.shape[0]
    r0 = 3344 - 768
    rq, rk = w[:, r0:r0 + WM], w[:, r0 + WM:r0 + 2 * WM]
    return jnp.concatenate([w[:, :1040], jnp.zeros((dm, W_AB - 16), w.dtype), w[:, 1040:],
                            _swap_halves_cols(rq), _swap_halves_cols(rk)], axis=1).astype(BF16)


def kernel(x, c, ctx, c_ctx, ada_w, ada_b, w_in, w_out, ln_g, ln_b, gdn_conv_w, gdn_a_log, gdn_dt_bias, gdn_norm_w, s5_a_re, s5_a_im, s5_log_step, s5_b_re, s5_b_im, s5_c_re, s5_c_im, s5_d, s5_glu_w, s5_glu_b, hgrn_lower_bounds, hgrn_norm_w, ret_decay, ffn_w1, ffn_w3, ffn_w2, moe_router, moe_w1, moe_w3, moe_w2):
    bsz, n_lat, d = x.shape
    n_ctx_tok = ctx.shape[1]
    depth = ada_w.shape[0]
    grid_w = 64
    assert n_ctx_tok == TT and n_lat % TT == 0 and bsz <= 7
    t = n_ctx_tok + n_lat
    alpha = (2.0 * depth) ** 0.25

    xs = jnp.concatenate([ctx, x], axis=1)
    cs = jnp.concatenate([c, c_ctx[None, :], jnp.zeros((8 - bsz - 1, d), F32)], axis=0)
    mod_all = _ada_mod(cs, ada_w, ada_b)
    lat_mod = mod_all[:, :bsz].reshape(depth, bsz, 6, d)
    ctx_mod = jnp.broadcast_to(mod_all[:, bsz].reshape(depth, 1, 6, d), (depth, bsz, 6, d))
    mod_tab = jnp.stack([ctx_mod, lat_mod], axis=2)

    lb_all = jnp.cumsum(jax.nn.softmax(hgrn_lower_bounds.astype(F32), axis=0), axis=0)
    lb_all = lb_all - lb_all[0]
    ind_sum = jnp.asarray(np.kron(np.eye(NH), np.ones((HEAD, HEAD))), BF16)
    chunks_eye = np.eye(TT // CH)
    tri_bd = jnp.asarray(np.kron(chunks_eye, np.tril(np.ones((CH, CH)))), BF16)
    ones_bd = jnp.asarray(np.kron(chunks_eye, np.ones((CH, CH))), BF16)
    ones_c = jnp.ones((CH, HEAD), BF16)
    mall_np, masks_np = _hgrn_constants()
    mall, masks = jnp.asarray(mall_np, BF16), jnp.asarray(masks_np)
    cos_t, sin_t = _rotary_tables(n_ctx_tok, n_lat, grid_w)
    nch = t // S5C

    for layer in range(depth):
        mod = mod_tab[layer]
        pg, pab, ps, ph, pr = _inproj(xs, mod, _prep_w_in(w_in[layer]))

        pabt = pab[..., :16].transpose(0, 2, 1)
        g_loc = _gdn_local(pg, pab, pabt, gdn_conv_w[layer], ind_sum, -jnp.exp(gdn_a_log[layer]),
                           gdn_dt_bias[layer], tri_bd, ones_bd)
        g_of, g_ob = _gdn_scan(g_loc, ind_sum, n_ctx_tok // CH)

        wz, wy, a1, a2 = _s5_weights(s5_a_re[layer], s5_a_im[layer], s5_log_step[layer], s5_b_re[layer],
                                     s5_b_im[layer], s5_c_re[layer], s5_c_im[layer])
        u4 = ps.reshape(bsz, nch, S5C, S5G, 16).transpose(0, 3, 1, 2, 4).reshape(bsz, S5G, nch, S5C * 16)
        y4 = _s5_scan(u4, wz, wy, a1, a2, n_ctx_tok // S5C)
        ys = y4.reshape(bsz, S5G, nch, S5C, 16).transpose(0, 2, 3, 1, 4).reshape(bsz, t, WM)

        lb = lb_all[layer][None, :]
        lbp = jnp.concatenate([jnp.log(lb), jnp.log1p(-lb), 1.0 - lb, jnp.zeros((5, WM), F32)], axis=0)
        h_oi, h_qgf, h_qgb, h_kvf, h_kvb, h_ecf, h_ecb = _hgrn_local(ph, lbp, mall, masks, ind_sum, ones_c)
        h_of, h_ob = _hgrn_scan(h_qgf, h_qgb, h_kvf, h_kvb, h_ecf, h_ecb, n_ctx_tok // CH)

        dec, qdec, kdec, cdec = _ret_constants(ret_decay[layer])
        r_of, r_ob = _ret_scan(pr, cos_t, sin_t, dec, qdec, kdec, cdec)

        vec = jnp.concatenate([jnp.tile(gdn_norm_w[layer], NH)[None], jnp.tile(hgrn_norm_w[layer], NH)[None],
                               s5_d[layer][None], s5_glu_b[layer][None], jnp.zeros((4, WM), F32)], axis=0)
        xs = _outproj(xs, mod, g_of, g_ob, pg, ys, ps, h_oi, h_of, h_ob, ph, r_of, r_ob, pr, vec, s5_glu_w[layer],
                      ind_sum, w_out[layer].astype(BF16), jnp.stack([ln_g[layer, 0], ln_b[layer, 0]]), alpha)

        j = layer // 2
        ln2 = jnp.stack([ln_g[layer, 1], ln_b[layer, 1]])
        if layer % 2 == 0:
            xs = _ffn(xs, mod, ffn_w1[j].astype(BF16), ffn_w3[j].astype(BF16), ffn_w2[j].astype(BF16), ln2, alpha)
        else:
            router_pad = jnp.concatenate([moe_router[j], jnp.zeros((d, 128 - N_EXP), F32)], axis=1)
            xs = _moe(xs, mod, router_pad, moe_w1[j].astype(BF16), moe_w3[j].astype(BF16),
                      moe_w2[j].astype(BF16), ln2, alpha, first_tile=int(layer == depth - 1))
    return xs if depth % 2 == 0 else xs[:, n_ctx_tok:, :]
```
